```python
import jax, jax.numpy as jnp
from jax import lax
import numpy as np

D_MODEL = 1024
BATCH = 16
SEQ = 256
DEPTH = 1
DEC_BATCH = 4
DEC_SEQ = 4096
PAST_LEN = 512

GRID_W = 64
H_A = 4
DK_A = 128
DV_A = 128
W_A = H_A * DV_A
H_B = 4
DK_B = 128
DV_B = 128
W_B = H_B * DV_B
QKV_B = 2 * H_B * DK_B + W_B
CONV_W = 3
CHUNK = 64
D_FF = -(-(8 * D_MODEL) // (3 * 256)) * 256
EPS = 1e-6
PROJ_SIZES = (H_A * DK_A, H_A * DK_A, H_A * DK_A, W_A, W_A, H_B * DK_B, H_B * DK_B, W_B, W_B, 2 * H_B, 2 * H_B)
D_PROJ = sum(PROJ_SIZES)

kernel_name = 'hybrid_hgrn2_gdn_diffusion_step'


def rms_norm(x, w):
    xf = x.astype(jnp.float32)
    y = xf * lax.rsqrt(jnp.mean(xf * xf, axis=-1, keepdims=True) + EPS)
    return (y * w.astype(jnp.float32)).astype(x.dtype)


def head_rms_norm(o, w):
    o = o * lax.rsqrt(jnp.mean(o * o, axis=-1, keepdims=True) + EPS)
    return o.reshape(o.shape[0], o.shape[1], -1) * w.astype(jnp.float32)


def l2norm(x):
    x = x.astype(jnp.float32)
    return x * lax.rsqrt(jnp.sum(x * x, axis=-1, keepdims=True) + EPS)


def _flip(a):
    return jnp.flip(a, axis=1)


def _chunks(a):
    B, T, H, d = a.shape
    return a.reshape(B, T // CHUNK, CHUNK, H, d).transpose(0, 3, 1, 2, 4).astype(jnp.float32)


def _unchunks(o):
    B, H, N, C, d = o.shape
    return o.transpose(0, 2, 3, 1, 4).reshape(B, N * C, H, d)


def to_columns(a, rows):
    B, T, C = a.shape
    return a.reshape(B, rows, GRID_W, C).transpose(0, 2, 1, 3)


def from_columns(a, rows):
    B, T, C = a.shape
    return a.reshape(B, GRID_W, rows, C).transpose(0, 2, 1, 3).reshape(B, T, C)


def centred_conv(x, w):
    pad = CONV_W // 2
    L = x.shape[-2]
    xp = jnp.pad(x, [(0, 0)] * (x.ndim - 2) + [(pad, pad), (0, 0)])
    return sum(xp[..., j:j + L, :] * w[j] for j in range(CONV_W))


def hgrn2_scan(q, k, v, g, s0):
    q, k, v, g = (_chunks(a) for a in (q, k, v, g))
    G = jnp.cumsum(g, axis=-2)
    G_last = G[..., -1:, :]
    q_dec = q * jnp.exp(G)
    k_dec = k * jnp.exp(-G)
    k_tail = k * jnp.exp(G_last - G)
    causal = jnp.tril(jnp.ones((CHUNK, CHUNK), dtype=bool))
    attn = jnp.where(causal, jnp.einsum('bhntk,bhnsk->bhnts', q_dec, k_dec), 0.0)
    o_intra = jnp.einsum('bhnts,bhnsv->bhntv', attn, v)
    u = jnp.einsum('bhnsk,bhnsv->bhnkv', k_tail, v)
    decay = jnp.exp(G_last[..., 0, :])

    def step(s, inp):
        d_c, u_c = inp
        return d_c[..., None] * s + u_c, s

    s_fin, s_in = lax.scan(step, s0.astype(jnp.float32), (jnp.moveaxis(decay, 2, 0), jnp.moveaxis(u, 2, 0)))
    s_in = jnp.moveaxis(s_in, 0, 2)
    o = o_intra + jnp.einsum('bhntk,bhnkv->bhntv', q_dec, s_in)
    return _unchunks(o), s_fin


def gated_delta_scan(q, k, v, beta, g, s0):
    dv = v.shape[-1]
    q, k, v = (_chunks(a) for a in (q, k, v))
    beta, g = (_chunks(a[..., None])[..., 0] for a in (beta, g))
    G = jnp.cumsum(g, axis=-1)
    causal = jnp.tril(jnp.ones((CHUNK, CHUNK), dtype=bool))
    strict = jnp.tril(jnp.ones((CHUNK, CHUNK), dtype=bool), -1)
    diff = G[..., :, None] - G[..., None, :]
    decay_mask = jnp.where(causal, jnp.exp(jnp.where(causal, diff, 0.0)), 0.0)
    k_beta = k * beta[..., None]
    lower = jnp.where(strict, jnp.einsum('bhntk,bhnsk->bhnts', k_beta, k) * decay_mask, 0.0)
    rhs = jnp.concatenate([v * beta[..., None], k_beta * jnp.exp(G)[..., None]], axis=-1)
    sol = lax.linalg.triangular_solve(lower + jnp.eye(CHUNK, dtype=jnp.float32), rhs,
                                      left_side=True, lower=True, unit_diagonal=True)
    u, w = sol[..., :dv], sol[..., dv:]
    attn = jnp.einsum('bhntk,bhnsk->bhnts', q, k) * decay_mask
    q_dec = q * jnp.exp(G)[..., None]
    G_last = G[..., -1:]
    k_tail = k * jnp.exp(G_last - G)[..., None]
    decay = jnp.exp(G_last[..., 0])

    def step(s, inp):
        u_c, w_c, attn_c, q_c, kt_c, d_c = inp
        v_new = u_c - jnp.einsum('bhck,bhkv->bhcv', w_c, s)
        o_c = jnp.einsum('bhtk,bhkv->bhtv', q_c, s) + jnp.einsum('bhts,bhsv->bhtv', attn_c, v_new)
        s = s * d_c[..., None, None] + jnp.einsum('bhsk,bhsv->bhkv', kt_c, v_new)
        return s, o_c

    xs = tuple(jnp.moveaxis(a, 2, 0) for a in (u, w, attn, q_dec, k_tail, decay))
    s_fin, o = lax.scan(step, s0.astype(jnp.float32), xs)
    return _unchunks(jnp.moveaxis(o, 0, 2)), s_fin


def _mixer(h, st_a, st_b, p, grid_rows):
    B, T, _ = h.shape
    split_at = np.cumsum(PROJ_SIZES)[:-1].tolist()
    a_q, a_ff, a_fb, a_i, a_g, b_q, b_k, b_v, b_z, b_beta, b_a = jnp.split(h @ p['w_in'], split_at, axis=-1)
    heads = lambda a, n: a.reshape(B, T, n, -1)

    q_a = heads(jax.nn.silu(a_q), H_A)
    v_a = heads(a_i, H_A)
    lb = p['lb']
    f_fwd = lb[0] + (1.0 - lb[0]) * jax.nn.sigmoid(a_ff.astype(jnp.float32))
    f_bwd = lb[1] + (1.0 - lb[1]) * jax.nn.sigmoid(a_fb.astype(jnp.float32))
    oa_f, sa_f = hgrn2_scan(q_a, heads(1.0 - f_fwd, H_A), v_a, heads(jnp.log(f_fwd), H_A), st_a[:, 0])
    oa_b, sa_b = hgrn2_scan(_flip(q_a), _flip(heads(1.0 - f_bwd, H_A)), _flip(v_a),
                            _flip(heads(jnp.log(f_bwd), H_A)), st_a[:, 1])
    o_a = head_rms_norm(oa_f + _flip(oa_b), p['hgrn_out_norm']) * jax.nn.silu(a_g.astype(jnp.float32))

    qkv = jnp.concatenate([b_q, b_k, b_v], axis=-1)
    gates = jnp.concatenate([b_beta, b_a], axis=-1)
    if grid_rows is None:
        qkv = qkv[:, None]
    else:
        qkv = to_columns(qkv, grid_rows)
        gates = to_columns(gates, grid_rows).reshape(B, T, -1)
    qkv = jax.nn.silu(centred_conv(qkv, p['conv_w'])).reshape(B, T, -1)
    q_b, k_b, v_b = jnp.split(qkv, [H_B * DK_B, 2 * H_B * DK_B], axis=-1)
    q_b = l2norm(heads(q_b, H_B)) * (DK_B ** -0.5)
    k_b = l2norm(heads(k_b, H_B))
    v_b = heads(v_b, H_B)
    gates = gates.astype(jnp.float32)
    beta = jax.nn.sigmoid(gates[..., :2 * H_B]).reshape(B, T, 2, H_B)
    g = -jnp.exp(p['A_log'].astype(jnp.float32)) * jax.nn.softplus(
        gates[..., 2 * H_B:].reshape(B, T, 2, H_B) + p['dt_bias'].astype(jnp.float32))
    ob_f, sb_f = gated_delta_scan(q_b, k_b, v_b, beta[:, :, 0], g[:, :, 0], st_b[:, 0])
    ob_b, sb_b = gated_delta_scan(_flip(q_b), _flip(k_b), _flip(v_b), _flip(beta[:, :, 1]),
                                  _flip(g[:, :, 1]), st_b[:, 1])
    o_b = ob_f + _flip(ob_b)
    if grid_rows is not None:
        o_b = from_columns(o_b.reshape(B, T, -1), grid_rows).reshape(B, T, H_B, DV_B)
    o_b = head_rms_norm(o_b, p['gdn_out_norm']) * jax.nn.silu(b_z.astype(jnp.float32))

    y = jnp.concatenate([o_a, o_b], axis=-1).astype(h.dtype) @ p['w_out']
    return y, jnp.stack([sa_f, sa_b], axis=1), jnp.stack([sb_f, sb_b], axis=1)


def _layer(x, mod, st_a, st_b, p, grid_rows):
    sh1, sc1, g1, sh2, sc2, g2 = jnp.split(mod[:, None, :], 6, axis=-1)
    h = rms_norm(x, p['norm1']) * (1 + sc1) + sh1
    y, st_a, st_b = _mixer(h, st_a, st_b, p, grid_rows)
    x = x + g1 * y
    h = rms_norm(x, p['norm2']) * (1 + sc2) + sh2
    ff = (jax.nn.silu(h @ p['w_gate']) * (h @ p['w_up'])) @ p['w_down']
    return x + g2 * ff, st_a, st_b


def setup_inputs(seed: int = 0) -> dict:
    key = jax.random.key(seed)
    ks = jax.random.split(key, 22)
    nrm = lambda k, s, sc: jax.random.normal(k, s, jnp.float32) * sc
    dt = jnp.exp(jax.random.uniform(ks[14], (DEPTH, 2, H_B), jnp.float32, np.log(1e-3), np.log(1e-1)))
    return {
        'x_prompt': nrm(ks[0], (BATCH, SEQ, D_MODEL), 1.0),
        'x_sample': nrm(ks[1], (DEC_BATCH, DEC_SEQ, D_MODEL), 1.0),
        'c': nrm(ks[2], (DEC_BATCH, D_MODEL), 1.0),
        'state_hgrn': nrm(ks[3], (DEC_BATCH, DEPTH, 2, H_A, DK_A, DV_A), 0.5),
        'state_gdn': nrm(ks[4], (DEC_BATCH, DEPTH, 2, H_B, DK_B, DV_B), 0.5),
        'c_ctx': nrm(ks[5], (D_MODEL,), 1.0),
        'w_ada': nrm(ks[6], (DEPTH, D_MODEL, 6 * D_MODEL), 0.5 * D_MODEL ** -0.5),
        'b_ada': nrm(ks[7], (DEPTH, 6 * D_MODEL), 0.02),
        'norm1': 1.0 + nrm(ks[8], (DEPTH, D_MODEL), 0.02),
        'norm2': 1.0 + nrm(ks[9], (DEPTH, D_MODEL), 0.02),
        'w_in': nrm(ks[10], (DEPTH, D_MODEL, D_PROJ), D_MODEL ** -0.5),
        'conv_w': nrm(ks[11], (DEPTH, CONV_W, QKV_B), CONV_W ** -0.5),
        'hgrn_lb': nrm(ks[12], (DEPTH + 1, 2, H_A * DK_A), 0.1),
        'gdn_A_log': jnp.log(jax.random.uniform(ks[13], (DEPTH, 2, H_B), jnp.float32, 1.0, 16.0)),
        'gdn_dt_bias': dt + jnp.log(-jnp.expm1(-dt)),
        'hgrn_out_norm': 1.0 + nrm(ks[15], (DEPTH, W_A), 0.02),
        'gdn_out_norm': 1.0 + nrm(ks[16], (DEPTH, W_B), 0.02),
        'w_out': nrm(ks[17], (DEPTH, W_A + W_B, D_MODEL), (W_A + W_B) ** -0.5),
        'w_gate': nrm(ks[18], (DEPTH, D_MODEL, D_FF), D_MODEL ** -0.5),
        'w_up': nrm(ks[19], (DEPTH, D_MODEL, D_FF), D_MODEL ** -0.5),
        'w_down': nrm(ks[20], (DEPTH, D_FF, D_MODEL), D_FF ** -0.5),
        'norm_f': 1.0 + nrm(ks[21], (D_MODEL,), 0.02),
    }


def reference(x_prompt, x_sample, c, state_hgrn, state_gdn, c_ctx, w_ada, b_ada, norm1, norm2, w_in,
              conv_w, hgrn_lb, gdn_A_log, gdn_dt_bias, hgrn_out_norm, gdn_out_norm, w_out, w_gate, w_up,
              w_down, norm_f):
    rows = x_sample.shape[1] // GRID_W
    lb_all = jnp.cumsum(jax.nn.softmax(hgrn_lb.astype(jnp.float32), axis=0), axis=0)
    s_ctx = jax.nn.silu(c_ctx)[None]
    s_lat = jax.nn.silu(c)
    xp, xs = x_prompt, x_sample
    zeros_a = jnp.zeros((x_prompt.shape[0], 2, H_A, DK_A, DV_A), jnp.float32)
    zeros_b = jnp.zeros((x_prompt.shape[0], 2, H_B, DK_B, DV_B), jnp.float32)
    new_a, new_b = [], []
    for l in range(DEPTH):
        p = {'norm1': norm1[l], 'norm2': norm2[l], 'w_in': w_in[l], 'conv_w': conv_w[l], 'lb': lb_all[l],
             'A_log': gdn_A_log[l], 'dt_bias': gdn_dt_bias[l], 'hgrn_out_norm': hgrn_out_norm[l],
             'gdn_out_norm': gdn_out_norm[l], 'w_out': w_out[l], 'w_gate': w_gate[l], 'w_up': w_up[l],
             'w_down': w_down[l]}
        m_ctx = s_ctx @ w_ada[l] + b_ada[l]
        m_lat = s_lat @ w_ada[l] + b_ada[l]
        xp, sa, sb = _layer(xp, m_ctx, zeros_a, zeros_b, p, None)
        xs, _, _ = _layer(xs, m_lat, state_hgrn[:, l], state_gdn[:, l], p, rows)
        new_a.append(sa)
        new_b.append(sb)
    new_state_hgrn = jnp.stack(new_a, axis=1)
    new_state_gdn = jnp.stack(new_b, axis=1)
    y_prompt = rms_norm(xp, norm_f)
    y_sample = rms_norm(xs, norm_f)
    return (y_prompt, y_sample, new_state_hgrn, new_state_gdn)
```

```python
import functools

import jax
import jax.numpy as jnp
from jax import lax
from jax.experimental import pallas as pl
from jax.experimental.pallas import tpu as pltpu

F32 = jnp.float32
BF16 = jnp.bfloat16

D_MODEL = 1024
N_HEADS = 4
D_HEAD = 128
W_GROUP = N_HEADS * D_HEAD
CHUNK = 64
GRID_W = 64
CONV_W = 3
EPS = 1e-6
N_MAIN_GROUPS = 9
GATE_LANES = 128
VMEM_LIMIT = 56 * 1024 * 1024


def _sigmoid(x):
    return 1.0 / (1.0 + jnp.exp(-x))


def _silu(x):
    return x * _sigmoid(x)


def _dot(a, b):
    return jnp.dot(a.astype(BF16), b.astype(BF16), preferred_element_type=F32)


def _dot_nt(a, b):
    return lax.dot_general(a.astype(BF16), b.astype(BF16), (((1,), (1,)), ((), ())),
                           preferred_element_type=F32)


def _dot_f32(a, b):
    return jnp.dot(a, b, precision=lax.Precision.HIGHEST, preferred_element_type=F32)


def _tri_masks(direction):
    r = lax.broadcasted_iota(jnp.int32, (CHUNK, CHUNK), 0)
    c = lax.broadcasted_iota(jnp.int32, (CHUNK, CHUNK), 1)
    if direction == 0:
        return c <= r, c < r
    return c >= r, c > r


def _mod_kernel(c_ref, w_ref, b_ref, o_ref):
    s = _silu(c_ref[...])
    o_ref[...] = _dot(s, w_ref[...]) + b_ref[...]


def _modulation(cvec, w_ada, b_ada):
    n_rows, d = cvec.shape
    n_out = w_ada.shape[1]
    tn = 1536
    return pl.pallas_call(
        _mod_kernel,
        grid=(n_out // tn,),
        in_specs=[pl.BlockSpec((n_rows, d), lambda j: (0, 0)),
                  pl.BlockSpec((d, tn), lambda j: (0, j)),
                  pl.BlockSpec((1, tn), lambda j: (0, j))],
        out_specs=pl.BlockSpec((n_rows, tn), lambda j: (0, j)),
        out_shape=jax.ShapeDtypeStruct((n_rows, n_out), F32),
        compiler_params=pltpu.CompilerParams(dimension_semantics=("arbitrary",),
                                             vmem_limit_bytes=VMEM_LIMIT),
        name="mod",
    )(cvec, w_ada, b_ada.reshape(1, n_out))


def _inproj_kernel(x_ref, mod_ref, n1_ref, lb_ref, gp_ref, w_ref, wg_ref,
                   qa_ref, ff_ref, fb_ref, va_ref, ga_ref, qkv_ref, zb_ref, gates_ref):
    x = x_ref[...]
    m = mod_ref[0]
    y = x * lax.rsqrt(jnp.mean(x * x, axis=-1, keepdims=True) + EPS) * n1_ref[...]
    hb = (y * (1.0 + m[1:2]) + m[0:1]).astype(BF16)

    def proj(j):
        return jnp.dot(hb, w_ref[:, j * W_GROUP:(j + 1) * W_GROUP], preferred_element_type=F32)

    lbp = lb_ref[...]
    e = jnp.exp(lbp - jnp.max(lbp, axis=0, keepdims=True))
    lb = e[0:1] / jnp.sum(e, axis=0, keepdims=True)
    lb_f, lb_b = lb[:, :W_GROUP], lb[:, W_GROUP:]

    qa_ref[...] = _silu(proj(0))
    ff_ref[...] = lb_f + (1.0 - lb_f) * _sigmoid(proj(1))
    fb_ref[...] = lb_b + (1.0 - lb_b) * _sigmoid(proj(2))
    va_ref[...] = proj(3)
    ga_ref[...] = _silu(proj(4))
    for j in range(3):
        qkv_ref[:, j * W_GROUP:(j + 1) * W_GROUP] = proj(5 + j)
    zb_ref[...] = _silu(proj(8))

    raw = jnp.dot(hb, wg_ref[...], preferred_element_type=F32)
    gp = gp_ref[...]
    z = raw + gp[1:2]
    softplus = jnp.maximum(z, 0.0) + jnp.log(1.0 + jnp.exp(-jnp.abs(z)))
    lane = lax.broadcasted_iota(jnp.int32, raw.shape, 1)
    gates_ref[...] = jnp.where(lane < 2 * N_HEADS, _sigmoid(raw),
                               jnp.where(lane < 4 * N_HEADS, -jnp.exp(gp[0:1]) * softplus, 0.0))


def _inproj(x2d, mod3, mod_row_of_tile, norm1, lbp, gparams, w_main, w_gates, tm):
    n_tok = x2d.shape[0]
    tok = lambda width: pl.BlockSpec((tm, width), lambda i: (i, 0))
    const = lambda shape: pl.BlockSpec(shape, lambda i: (0,) * len(shape))
    out_widths = [W_GROUP] * 5 + [3 * W_GROUP, W_GROUP, GATE_LANES]
    return pl.pallas_call(
        _inproj_kernel,
        grid=(n_tok // tm,),
        in_specs=[tok(D_MODEL),
                  pl.BlockSpec((1, 6, D_MODEL), lambda i: (mod_row_of_tile(i), 0, 0)),
                  const((1, D_MODEL)), const(lbp.shape), const(gparams.shape),
                  const(w_main.shape), const(w_gates.shape)],
        out_specs=[tok(w) for w in out_widths],
        out_shape=[jax.ShapeDtypeStruct((n_tok, w), F32) for w in out_widths],
        compiler_params=pltpu.CompilerParams(dimension_semantics=("arbitrary",),
                                             vmem_limit_bytes=VMEM_LIMIT),
        name="inproj",
    )(x2d, mod3, norm1, lbp, gparams, w_main, w_gates)


def _hgrn_kernel(*refs, n_chunks, has_s0, emit_state):
    it = iter(refs)
    qf_ref, ff_ref, vf_ref, qb_ref, fb_ref, vb_ref = (next(it) for _ in range(6))
    s0_ref = next(it) if has_s0 else None
    of_ref, ob_ref = next(it), next(it)
    st_ref = next(it) if emit_state else None
    s_ref = next(it)
    n = pl.program_id(1)

    @pl.when(n == 0)
    def _():
        if has_s0:
            s_ref[...] = s0_ref[0]
        else:
            s_ref[...] = jnp.zeros_like(s_ref)

    for d, (q_ref, f_ref, v_ref, o_ref) in enumerate(((qf_ref, ff_ref, vf_ref, of_ref),
                                                      (qb_ref, fb_ref, vb_ref, ob_ref))):
        incl, _ = _tri_masks(d)
        tri = incl.astype(F32)
        for h in range(N_HEADS):
            sl = slice(h * D_HEAD, (h + 1) * D_HEAD)
            q, f, v = q_ref[:, sl], f_ref[:, sl], v_ref[:, sl]
            k = 1.0 - f
            g = jnp.log(f)
            G = _dot_f32(tri, g)
            g_last_row = G[CHUNK - 1:CHUNK] if d == 0 else G[0:1]
            g_last_col = jnp.sum(g.T, axis=1, keepdims=True)
            q_dec = q * jnp.exp(G)
            k_dec = k * jnp.exp(-G)
            k_tail = k * jnp.exp(g_last_row - G)
            attn = jnp.where(incl, _dot_nt(q_dec, k_dec), 0.0)
            s = s_ref[d, h]
            o_ref[:, sl] = _dot(attn, v) + _dot(q_dec, s)
            s_ref[d, h] = jnp.exp(g_last_col) * s + _dot(k_tail.T, v)

    if emit_state:
        @pl.when(n == n_chunks - 1)
        def _():
            st_ref[0] = s_ref[...]


def _hgrn_scan(qa, f_fwd, f_bwd, va, s0, batch, emit_state):
    n_tok = qa.shape[0]
    n_chunks = n_tok // batch // CHUNK
    fwd = pl.BlockSpec((CHUNK, W_GROUP), lambda b, n: (b * n_chunks + n, 0))
    bwd = pl.BlockSpec((CHUNK, W_GROUP), lambda b, n: (b * n_chunks + n_chunks - 1 - n, 0))
    state_spec = pl.BlockSpec((1, 2, N_HEADS, D_HEAD, D_HEAD), lambda b, n: (b, 0, 0, 0, 0))
    in_specs, args = [fwd, fwd, fwd, bwd, bwd, bwd], [qa, f_fwd, va, qa, f_bwd, va]
    if s0 is not None:
        in_specs.append(state_spec)
        args.append(s0)
    out_specs = [fwd, bwd]
    out_shape = [jax.ShapeDtypeStruct((n_tok, W_GROUP), F32)] * 2
    if emit_state:
        out_specs.append(state_spec)
        out_shape.append(jax.ShapeDtypeStruct((batch, 2, N_HEADS, D_HEAD, D_HEAD), F32))
    return pl.pallas_call(
        functools.partial(_hgrn_kernel, n_chunks=n_chunks, has_s0=s0 is not None, emit_state=emit_state),
        grid=(batch, n_chunks),
        in_specs=in_specs, out_specs=out_specs, out_shape=out_shape,
        scratch_shapes=[pltpu.VMEM((2, N_HEADS, D_HEAD, D_HEAD), F32)],
        compiler_params=pltpu.CompilerParams(dimension_semantics=("arbitrary", "arbitrary"),
                                             vmem_limit_bytes=VMEM_LIMIT),
        name="hgrn",
    )(*args)


def _gdn_prepare(qkv_ref, cw_ref):
    x = qkv_ref[0]
    n_rows = x.shape[0]
    row = lax.broadcasted_iota(jnp.int32, x.shape, 0)
    x_prev = jnp.where(row == 0, 0.0, pltpu.roll(x, 1, axis=0))
    x_next = jnp.where(row == n_rows - 1, 0.0, pltpu.roll(x, n_rows - 1, axis=0))
    cw = cw_ref[...]
    y = _silu(x_prev * cw[0:1] + x * cw[1:2] + x_next * cw[2:3])
    heads = []
    for h in range(N_HEADS):
        q = y[:, h * D_HEAD:(h + 1) * D_HEAD]
        k = y[:, W_GROUP + h * D_HEAD:W_GROUP + (h + 1) * D_HEAD]
        v = y[:, 2 * W_GROUP + h * D_HEAD:2 * W_GROUP + (h + 1) * D_HEAD]
        q = q * lax.rsqrt(jnp.sum(q * q, axis=-1, keepdims=True) + EPS) * (D_HEAD ** -0.5)
        k = k * lax.rsqrt(jnp.sum(k * k, axis=-1, keepdims=True) + EPS)
        heads.append((q, k, v))
    return heads


def _gdn_chunk(q, k, v, beta, g_cum, g_cum_row, direction, s):
    incl, strict = _tri_masks(direction)
    diff = g_cum - g_cum_row
    decay_mask = jnp.where(incl, jnp.exp(jnp.where(incl, diff, 0.0)), 0.0)
    k_beta = k * beta
    lower = jnp.where(strict, _dot_nt(k_beta, k) * decay_mask, 0.0)
    e_g = jnp.exp(g_cum)
    x = jnp.concatenate([v * beta, k_beta * e_g], axis=-1)
    x = x - _dot_f32(lower, x)
    p = lower
    for _ in range(5):
        p = _dot_f32(p, p)
        x = x + _dot_f32(p, x)
    u, w = x[:, :D_HEAD], x[:, D_HEAD:]
    attn = _dot_nt(q, k) * decay_mask
    g_last = g_cum[CHUNK - 1:CHUNK] if direction == 0 else g_cum[0:1]
    q_dec = q * e_g
    k_tail = k * jnp.exp(g_last - g_cum)
    v_new = u - _dot(w, s)
    o = _dot(q_dec, s) + _dot(attn, v_new)
    s_new = s * jnp.exp(g_last) + _dot(k_tail.T, v_new)
    return o, s_new


def _gdn_kernel(*refs, n_steps, chunks_per_block, has_s0, emit_state):
    it = iter(refs)
    xf_ref, gf_ref, xb_ref, gb_ref, cw_ref = (next(it) for _ in range(5))
    s0_ref = next(it) if has_s0 else None
    of_ref, ob_ref = next(it), next(it)
    st_ref = next(it) if emit_state else None
    s_ref = next(it)
    n = pl.program_id(1)

    @pl.when(n == 0)
    def _():
        if has_s0:
            s_ref[...] = s0_ref[0]
        else:
            s_ref[...] = jnp.zeros_like(s_ref)

    for d, (x_ref, g_ref, o_ref) in enumerate(((xf_ref, gf_ref, of_ref), (xb_ref, gb_ref, ob_ref))):
        heads = _gdn_prepare(x_ref, cw_ref)
        tri = _tri_masks(d)[0].astype(F32)
        order = range(chunks_per_block) if d == 0 else range(chunks_per_block - 1, -1, -1)
        for ci in order:
            rows = slice(ci * CHUNK, (ci + 1) * CHUNK)
            gates = g_ref[0, rows, :]
            cum = _dot_f32(tri, gates)
            cum_t = cum.T
            for h in range(N_HEADS):
                jb, jg = d * N_HEADS + h, 2 * N_HEADS + d * N_HEADS + h
                q, k, v = (a[rows] for a in heads[h])
                o, s_new = _gdn_chunk(q, k, v, gates[:, jb:jb + 1], cum[:, jg:jg + 1],
                                      cum_t[jg:jg + 1, :], d, s_ref[d, h])
                o_ref[0, rows, h * D_HEAD:(h + 1) * D_HEAD] = o
                s_ref[d, h] = s_new

    if emit_state:
        @pl.when(n == n_steps - 1)
        def _():
            st_ref[0] = s_ref[...]


def _gdn_scan(qkv3, gates3, conv_w, s0, block_rows, columns, emit_state):
    batch = qkv3.shape[0]
    n_row_blocks = qkv3.shape[1] // block_rows
    n_steps = n_row_blocks * columns
    assert n_row_blocks == 1 or columns == 1

    def fwd(b, n):
        return (b, n, 0) if columns == 1 else (b, 0, n)

    def bwd(b, n):
        return fwd(b, n_steps - 1 - n)

    spec = lambda width, imap: pl.BlockSpec((1, block_rows, width), imap)
    state_spec = pl.BlockSpec((1, 2, N_HEADS, D_HEAD, D_HEAD), lambda b, n: (b, 0, 0, 0, 0))
    in_specs = [spec(3 * W_GROUP, fwd), spec(GATE_LANES, fwd), spec(3 * W_GROUP, bwd), spec(GATE_LANES, bwd),
                pl.BlockSpec(conv_w.shape, lambda b, n: (0, 0))]
    args = [qkv3, gates3, qkv3, gates3, conv_w]
    if s0 is not None:
        in_specs.append(state_spec)
        args.append(s0)
    o_shape = jax.ShapeDtypeStruct((batch, qkv3.shape[1], columns * W_GROUP), F32)
    out_specs, out_shape = [spec(W_GROUP, fwd), spec(W_GROUP, bwd)], [o_shape, o_shape]
    if emit_state:
        out_specs.append(state_spec)
        out_shape.append(jax.ShapeDtypeStruct((batch, 2, N_HEADS, D_HEAD, D_HEAD), F32))
    return pl.pallas_call(
        functools.partial(_gdn_kernel, n_steps=n_steps, chunks_per_block=block_rows // CHUNK,
                          has_s0=s0 is not None, emit_state=emit_state),
        grid=(batch, n_steps),
        in_specs=in_specs, out_specs=out_specs, out_shape=out_shape,
        scratch_shapes=[pltpu.VMEM((2, N_HEADS, D_HEAD, D_HEAD), F32)],
        compiler_params=pltpu.CompilerParams(dimension_semantics=("arbitrary", "arbitrary"),
                                             vmem_limit_bytes=VMEM_LIMIT),
        name="gdn",
    )(*args)


def _mixout_kernel(x_ref, oaf_ref, oab_ref, obf_ref, obb_ref, ga_ref, zb_ref, mod_ref,
                   na_ref, nb_ref, wo_ref, n2_ref, x1_ref, h2_ref):
    def normed(of_ref, ob_ref, w_ref, gate_ref):
        parts = []
        for h in range(N_HEADS):
            sl = slice(h * D_HEAD, (h + 1) * D_HEAD)
            o = of_ref[:, sl] + ob_ref[:, sl]
            o = o * lax.rsqrt(jnp.mean(o * o, axis=-1, keepdims=True) + EPS)
            parts.append(o * w_ref[:, sl] * gate_ref[:, sl])
        return parts

    mixed = jnp.concatenate(normed(oaf_ref, oab_ref, na_ref, ga_ref)
                            + normed(obf_ref, obb_ref, nb_ref, zb_ref), axis=-1)
    m = mod_ref[0]
    x1 = x_ref[...] + m[2:3] * _dot(mixed, wo_ref[...])
    x1_ref[...] = x1
    y = x1 * lax.rsqrt(jnp.mean(x1 * x1, axis=-1, keepdims=True) + EPS) * n2_ref[...]
    h2_ref[...] = (y * (1.0 + m[4:5]) + m[3:4]).astype(BF16)


def _mixout(x2d, oaf, oab, obf, obb, ga, zb, mod3, mod_row_of_tile, norm_a, norm_b, w_out, norm2, tm):
    n_tok = x2d.shape[0]
    tok = lambda width: pl.BlockSpec((tm, width), lambda i: (i, 0))
    const = lambda shape: pl.BlockSpec(shape, lambda i: (0,) * len(shape))
    return pl.pallas_call(
        _mixout_kernel,
        grid=(n_tok // tm,),
        in_specs=[tok(D_MODEL)] + [tok(W_GROUP)] * 6
                 + [pl.BlockSpec((1, 6, D_MODEL), lambda i: (mod_row_of_tile(i), 0, 0)),
                    const((1, W_GROUP)), const((1, W_GROUP)), const(w_out.shape), const((1, D_MODEL))],
        out_specs=[tok(D_MODEL), tok(D_MODEL)],
        out_shape=[jax.ShapeDtypeStruct((n_tok, D_MODEL), F32), jax.ShapeDtypeStruct((n_tok, D_MODEL), BF16)],
        compiler_params=pltpu.CompilerParams(dimension_semantics=("arbitrary",),
                                             vmem_limit_bytes=VMEM_LIMIT),
        name="mixout",
    )(x2d, oaf, oab, obf, obb, ga, zb, mod3, norm_a, norm_b, w_out, norm2)


def _ffn_kernel(h_ref, x1_ref, mod_ref, wg_ref, wu_ref, wd_ref, nf_ref, y_ref):
    h = h_ref[...]
    gate = jnp.dot(h, wg_ref[...], preferred_element_type=F32)
    up = jnp.dot(h, wu_ref[...], preferred_element_type=F32)
    ff = _dot(_silu(gate) * up, wd_ref[...])
    x2 = x1_ref[...] + mod_ref[0][5:6] * ff
    y_ref[...] = x2 * lax.rsqrt(jnp.mean(x2 * x2, axis=-1, keepdims=True) + EPS) * nf_ref[...]


def _ffn(h2, x1, mod3, mod_row_of_tile, w_gate, w_up, w_down, norm_f, tm):
    n_tok = h2.shape[0]
    tok = pl.BlockSpec((tm, D_MODEL), lambda i: (i, 0))
    const = lambda shape: pl.BlockSpec(shape, lambda i: (0,) * len(shape))
    return pl.pallas_call(
        _ffn_kernel,
        grid=(n_tok // tm,),
        in_specs=[tok, tok, pl.BlockSpec((1, 6, D_MODEL), lambda i: (mod_row_of_tile(i), 0, 0)),
                  const(w_gate.shape), const(w_up.shape), const(w_down.shape), const((1, D_MODEL))],
        out_specs=tok,
        out_shape=jax.ShapeDtypeStruct((n_tok, D_MODEL), F32),
        compiler_params=pltpu.CompilerParams(dimension_semantics=("arbitrary",),
                                             vmem_limit_bytes=VMEM_LIMIT),
        name="ffn",
    )(h2, x1, mod3, w_gate, w_up, w_down, norm_f)


def _stream(x, mod3, mod_row_of_tile, s0_a, s0_b, p, latent):
    batch, seq, _ = x.shape
    x2d = x.reshape(batch * seq, D_MODEL)
    tm = 256
    qa, f_fwd, f_bwd, va, ga, qkv, zb, gates = _inproj(
        x2d, mod3, functools.partial(mod_row_of_tile, tm=tm), p["norm1"], p["lbp"], p["gparams"],
        p["w_main"], p["w_gates"], tm)

    hg = _hgrn_scan(qa, f_fwd, f_bwd, va, s0_a, batch, emit_state=not latent)
    if latent:
        rows = seq // GRID_W
        assert rows == CHUNK
        gd = _gdn_scan(qkv.reshape(batch, rows, GRID_W * 3 * W_GROUP), gates.reshape(batch, rows, GRID_W * GATE_LANES),
                       p["conv_w"], s0_b, block_rows=rows, columns=GRID_W, emit_state=False)
    else:
        gd = _gdn_scan(qkv.reshape(batch, seq, 3 * W_GROUP), gates.reshape(batch, seq, GATE_LANES),
                       p["conv_w"], s0_b, block_rows=seq, columns=1, emit_state=True)
    oaf, oab = hg[0], hg[1]
    obf, obb = (o.reshape(batch * seq, W_GROUP) for o in gd[:2])

    x1, h2 = _mixout(x2d, oaf, oab, obf, obb, ga, zb, mod3, functools.partial(mod_row_of_tile, tm=tm),
                     p["norm_a"], p["norm_b"], p["w_out"], p["norm2"], tm)
    tm_ffn = 512
    y = _ffn(h2, x1, mod3, functools.partial(mod_row_of_tile, tm=tm_ffn), p["w_gate"], p["w_up"], p["w_down"],
             p["norm_f"], tm_ffn)
    states = (None, None) if latent else (hg[2], gd[2])
    return y.reshape(batch, seq, D_MODEL), states


def kernel(x_prompt, x_sample, c, state_hgrn, state_gdn, c_ctx, w_ada, b_ada, norm1, norm2, w_in, conv_w,
           hgrn_lb, gdn_A_log, gdn_dt_bias, hgrn_out_norm, gdn_out_norm, w_out, w_gate, w_up, w_down, norm_f):
    depth = w_in.shape[0]
    assert depth == 1 and hgrn_lb.shape[0] == 2
    dec_batch, dec_seq, _ = x_sample.shape
    l = 0

    n_main = N_MAIN_GROUPS * W_GROUP
    pad8 = jnp.zeros((1, 2 * N_HEADS), F32)
    gparams = jnp.concatenate(
        [jnp.concatenate([pad8, a.reshape(1, 2 * N_HEADS).astype(F32),
                          jnp.zeros((1, GATE_LANES - 4 * N_HEADS), F32)], axis=1)
         for a in (gdn_A_log[l], gdn_dt_bias[l])], axis=0)
    p = {
        "norm1": norm1[l].reshape(1, D_MODEL), "norm2": norm2[l].reshape(1, D_MODEL),
        "lbp": hgrn_lb.reshape(2, 2 * W_GROUP), "gparams": gparams,
        "w_main": w_in[l][:, :n_main].astype(BF16),
        "w_gates": jnp.pad(w_in[l][:, n_main:], ((0, 0), (0, GATE_LANES - 4 * N_HEADS))).astype(BF16),
        "conv_w": conv_w[l],
        "norm_a": hgrn_out_norm[l].reshape(1, W_GROUP), "norm_b": gdn_out_norm[l].reshape(1, W_GROUP),
        "w_out": w_out[l].astype(BF16), "w_gate": w_gate[l].astype(BF16), "w_up": w_up[l].astype(BF16),
        "w_down": w_down[l].astype(BF16), "norm_f": norm_f.reshape(1, D_MODEL),
    }

    n_mod_rows = 8
    cvec = jnp.concatenate([c_ctx[None], c, jnp.zeros((n_mod_rows - 1 - dec_batch, D_MODEL), F32)], axis=0)
    mod3 = _modulation(cvec, w_ada[l], b_ada[l]).reshape(n_mod_rows, 6, D_MODEL)

    y_prompt, (new_a, new_b) = _stream(x_prompt, mod3, lambda i, tm: 0, None, None, p, latent=False)
    y_sample, _ = _stream(x_sample, mod3, lambda i, tm: 1 + i // (dec_seq // tm), state_hgrn[:, l],
                          state_gdn[:, l], p, latent=True)
    return y_prompt, y_sample, new_a[:, None], new_b[:, None]
```

```python
import functools

import jax
import jax.numpy as jnp
from jax import lax
from jax.experimental import pallas as pl
from jax.experimental.pallas import tpu as pltpu

F32 = jnp.float32
BF16 = jnp.bfloat16

D_MODEL = 1024
N_HEADS = 4
D_HEAD = 128
W_GROUP = N_HEADS * D_HEAD
CHUNK = 64
GRID_W = 64
CONV_W = 3
EPS = 1e-6
N_MAIN_GROUPS = 9
GATE_LANES = 128
VMEM_LIMIT = 56 * 1024 * 1024


def _sigmoid(x):
    return 1.0 / (1.0 + jnp.exp(-x))


def _silu(x):
    return x * _sigmoid(x)


def _dot(a, b):
    return jnp.dot(a.astype(BF16), b.astype(BF16), preferred_element_type=F32)


def _dot_nt(a, b):
    return lax.dot_general(a.astype(BF16), b.astype(BF16), (((1,), (1,)), ((), ())),
                           preferred_element_type=F32)


def _split3(x):
    x1 = x.astype(BF16)
    r = x - x1.astype(F32)
    x2 = r.astype(BF16)
    x3 = (r - x2.astype(F32)).astype(BF16)
    return x1, x2, x3


def _cumsum_rows(tri3, x):
    return jnp.dot(tri3, jnp.concatenate(_split3(x), axis=0), preferred_element_type=F32)


def _tri3(direction):
    tri = _tri_masks(direction)[0].astype(BF16)
    return jnp.concatenate([tri, tri, tri], axis=1)


def _tri_masks(direction):
    r = lax.broadcasted_iota(jnp.int32, (CHUNK, CHUNK), 0)
    c = lax.broadcasted_iota(jnp.int32, (CHUNK, CHUNK), 1)
    if direction == 0:
        return c <= r, c < r
    return c >= r, c > r


def _mod_kernel(c_ref, w_ref, b_ref, o_ref):
    s = _silu(c_ref[...])
    o_ref[...] = _dot(s, w_ref[...]) + b_ref[...]


def _modulation(cvec, w_ada, b_ada):
    n_rows, d = cvec.shape
    n_out = w_ada.shape[1]
    tn = 1536
    return pl.pallas_call(
        _mod_kernel,
        grid=(n_out // tn,),
        in_specs=[pl.BlockSpec((n_rows, d), lambda j: (0, 0)),
                  pl.BlockSpec((d, tn), lambda j: (0, j)),
                  pl.BlockSpec((1, tn), lambda j: (0, j))],
        out_specs=pl.BlockSpec((n_rows, tn), lambda j: (0, j)),
        out_shape=jax.ShapeDtypeStruct((n_rows, n_out), F32),
        compiler_params=pltpu.CompilerParams(dimension_semantics=("arbitrary",),
                                             vmem_limit_bytes=VMEM_LIMIT),
        name="mod",
    )(cvec, w_ada, b_ada.reshape(1, n_out))


def _inproj_kernel(x_ref, mod_ref, n1_ref, lb_ref, gp_ref, w_ref, wg_ref,
                   qa_ref, ff_ref, fb_ref, va_ref, ga_ref, qkv_ref, zb_ref, gates_ref):
    x = x_ref[...]
    m = mod_ref[0]
    y = x * lax.rsqrt(jnp.mean(x * x, axis=-1, keepdims=True) + EPS) * n1_ref[...]
    hb = (y * (1.0 + m[1:2]) + m[0:1]).astype(BF16)

    def proj(j):
        return jnp.dot(hb, w_ref[:, j * W_GROUP:(j + 1) * W_GROUP], preferred_element_type=F32)

    lbp = lb_ref[...]
    e = jnp.exp(lbp - jnp.max(lbp, axis=0, keepdims=True))
    lb = e[0:1] / jnp.sum(e, axis=0, keepdims=True)
    lb_f, lb_b = lb[:, :W_GROUP], lb[:, W_GROUP:]

    qa_ref[...] = _silu(proj(0))
    ff_ref[...] = lb_f + (1.0 - lb_f) * _sigmoid(proj(1))
    fb_ref[...] = lb_b + (1.0 - lb_b) * _sigmoid(proj(2))
    va_ref[...] = proj(3)
    ga_ref[...] = _silu(proj(4))
    for j in range(3):
        qkv_ref[:, j * W_GROUP:(j + 1) * W_GROUP] = proj(5 + j)
    zb_ref[...] = _silu(proj(8))

    raw = jnp.dot(hb, wg_ref[...], preferred_element_type=F32)
    gp = gp_ref[...]
    z = raw + gp[1:2]
    softplus = jnp.maximum(z, 0.0) + jnp.log(1.0 + jnp.exp(-jnp.abs(z)))
    lane = lax.broadcasted_iota(jnp.int32, raw.shape, 1)
    gates_ref[...] = jnp.where(lane < 2 * N_HEADS, _sigmoid(raw),
                               jnp.where(lane < 4 * N_HEADS, -jnp.exp(gp[0:1]) * softplus, 0.0))


def _inproj(x2d, mod3, mod_row_of_tile, norm1, lbp, gparams, w_main, w_gates, tm):
    n_tok = x2d.shape[0]
    tok = lambda width: pl.BlockSpec((tm, width), lambda i: (i, 0))
    const = lambda shape: pl.BlockSpec(shape, lambda i: (0,) * len(shape))
    out_widths = [W_GROUP] * 5 + [3 * W_GROUP, W_GROUP, GATE_LANES]
    return pl.pallas_call(
        _inproj_kernel,
        grid=(n_tok // tm,),
        in_specs=[tok(D_MODEL),
                  pl.BlockSpec((1, 6, D_MODEL), lambda i: (mod_row_of_tile(i), 0, 0)),
                  const((1, D_MODEL)), const(lbp.shape), const(gparams.shape),
                  const(w_main.shape), const(w_gates.shape)],
        out_specs=[tok(w) for w in out_widths],
        out_shape=[jax.ShapeDtypeStruct((n_tok, w), F32) for w in out_widths],
        compiler_params=pltpu.CompilerParams(dimension_semantics=("arbitrary",),
                                             vmem_limit_bytes=VMEM_LIMIT),
        name="inproj",
    )(x2d, mod3, norm1, lbp, gparams, w_main, w_gates)


def _hgrn_kernel(*refs, n_chunks, has_s0, emit_state):
    it = iter(refs)
    qf_ref, ff_ref, vf_ref, qb_ref, fb_ref, vb_ref = (next(it) for _ in range(6))
    s0_ref = next(it) if has_s0 else None
    of_ref, ob_ref = next(it), next(it)
    st_ref = next(it) if emit_state else None
    s_ref = next(it)
    n = pl.program_id(1)

    @pl.when(n == 0)
    def _():
        if has_s0:
            s_ref[...] = s0_ref[0]
        else:
            s_ref[...] = jnp.zeros_like(s_ref)

    chains = []
    for d, (q_ref, f_ref, v_ref, o_ref) in enumerate(((qf_ref, ff_ref, vf_ref, of_ref),
                                                      (qb_ref, fb_ref, vb_ref, ob_ref))):
        incl, _ = _tri_masks(d)
        g_all = jnp.log(f_ref[...])
        cum_all = _cumsum_rows(_tri3(d), g_all)
        for h in range(N_HEADS):
            sl = slice(h * D_HEAD, (h + 1) * D_HEAD)
            k = 1.0 - f_ref[:, sl]
            g, G = g_all[:, sl], cum_all[:, sl]
            g_last_row = G[CHUNK - 1:CHUNK] if d == 0 else G[0:1]
            chains.append(dict(
                d=d, h=h, sl=sl, o_ref=o_ref, incl=incl, vb=v_ref[:, sl].astype(BF16),
                decay=jnp.exp(jnp.sum(g.T, axis=1, keepdims=True)),
                q_dec=(q_ref[:, sl] * jnp.exp(G)).astype(BF16), k_dec=k * jnp.exp(-G),
                k_tail_t=(k * jnp.exp(g_last_row - G)).T.astype(BF16)))
    attns = [jnp.where(ch["incl"], _dot_nt(ch["q_dec"], ch["k_dec"]), 0.0).astype(BF16) for ch in chains]
    states = [s_ref[ch["d"], ch["h"]] for ch in chains]
    outs = [_dot(jnp.concatenate([ch["q_dec"], attn], axis=1), jnp.concatenate([s.astype(BF16), ch["vb"]], axis=0))
            for ch, attn, s in zip(chains, attns, states)]
    upds = [_dot(ch["k_tail_t"], ch["vb"]) for ch in chains]
    for ch, o, s, u in zip(chains, outs, states, upds):
        ch["o_ref"][:, ch["sl"]] = o
        s_ref[ch["d"], ch["h"]] = ch["decay"] * s + u

    if emit_state:
        @pl.when(n == n_chunks - 1)
        def _():
            st_ref[0] = s_ref[...]


def _hgrn_scan(qa, f_fwd, f_bwd, va, s0, batch, emit_state):
    n_tok = qa.shape[0]
    n_chunks = n_tok // batch // CHUNK
    fwd = pl.BlockSpec((CHUNK, W_GROUP), lambda b, n: (b * n_chunks + n, 0))
    bwd = pl.BlockSpec((CHUNK, W_GROUP), lambda b, n: (b * n_chunks + n_chunks - 1 - n, 0))
    state_spec = pl.BlockSpec((1, 2, N_HEADS, D_HEAD, D_HEAD), lambda b, n: (b, 0, 0, 0, 0))
    in_specs, args = [fwd, fwd, fwd, bwd, bwd, bwd], [qa, f_fwd, va, qa, f_bwd, va]
    if s0 is not None:
        in_specs.append(state_spec)
        args.append(s0)
    out_specs = [fwd, bwd]
    out_shape = [jax.ShapeDtypeStruct((n_tok, W_GROUP), F32)] * 2
    if emit_state:
        out_specs.append(state_spec)
        out_shape.append(jax.ShapeDtypeStruct((batch, 2, N_HEADS, D_HEAD, D_HEAD), F32))
    return pl.pallas_call(
        functools.partial(_hgrn_kernel, n_chunks=n_chunks, has_s0=s0 is not None, emit_state=emit_state),
        grid=(batch, n_chunks),
        in_specs=in_specs, out_specs=out_specs, out_shape=out_shape,
        scratch_shapes=[pltpu.VMEM((2, N_HEADS, D_HEAD, D_HEAD), F32)],
        compiler_params=pltpu.CompilerParams(dimension_semantics=("arbitrary", "arbitrary"),
                                             vmem_limit_bytes=VMEM_LIMIT),
        name="hgrn",
    )(*args)


def _gdn_prepare(qkv_ref, cw_ref):
    x = qkv_ref[0]
    n_rows = x.shape[0]
    row = lax.broadcasted_iota(jnp.int32, x.shape, 0)
    x_prev = jnp.where(row == 0, 0.0, pltpu.roll(x, 1, axis=0))
    x_next = jnp.where(row == n_rows - 1, 0.0, pltpu.roll(x, n_rows - 1, axis=0))
    cw = cw_ref[...]
    y = _silu(x_prev * cw[0:1] + x * cw[1:2] + x_next * cw[2:3])
    heads = []
    for h in range(N_HEADS):
        q = y[:, h * D_HEAD:(h + 1) * D_HEAD]
        k = y[:, W_GROUP + h * D_HEAD:W_GROUP + (h + 1) * D_HEAD]
        v = y[:, 2 * W_GROUP + h * D_HEAD:2 * W_GROUP + (h + 1) * D_HEAD]
        q = q * lax.rsqrt(jnp.sum(q * q, axis=-1, keepdims=True) + EPS) * (D_HEAD ** -0.5)
        k = k * lax.rsqrt(jnp.sum(k * k, axis=-1, keepdims=True) + EPS)
        heads.append((q, k, v))
    return heads


INVERSE_BASE_BLOCK = 8


def _same_block(block):
    r = lax.broadcasted_iota(jnp.int32, (CHUNK, CHUNK), 0) // block
    c = lax.broadcasted_iota(jnp.int32, (CHUNK, CHUNK), 1) // block
    return r == c


def _unit_triangular_inverses(lowers):
    r = lax.broadcasted_iota(jnp.int32, (CHUNK, CHUNK), 0)
    c = lax.broadcasted_iota(jnp.int32, (CHUNK, CHUNK), 1)
    eye = jnp.where(r == c, 1.0, 0.0)
    diag = _same_block(INVERSE_BASE_BLOCK)
    ds = [jnp.where(diag, lo, 0.0) for lo in lowers]
    ts = [eye - d for d in ds]
    ps = [_dot(d, d) for d in ds]
    power = 4
    while power < INVERSE_BASE_BLOCK:
        tps = [_dot(jnp.concatenate([t.astype(BF16), p.astype(BF16)], axis=0), p) for t, p in zip(ts, ps)]
        ts = [t + tp[:CHUNK] for t, tp in zip(ts, tps)]
        ps = [tp[CHUNK:] for tp in tps]
        power *= 2
    ts = [t + _dot(t, p) for t, p in zip(ts, ps)]
    block = INVERSE_BASE_BLOCK
    while block < CHUNK:
        off_mask = _same_block(2 * block) & jnp.logical_not(_same_block(block))
        ws = [_dot(t, jnp.where(off_mask, lo, 0.0)) for t, lo in zip(ts, lowers)]
        ts = [t - _dot(w, t) for t, w in zip(ts, ws)]
        block *= 2
    return ts


def _gdn_local(chains):
    for ch in chains:
        incl, strict = _tri_masks(ch["d"])
        diff = ch["g_cum"] - ch["g_cum_row"]
        ch["decay_mask"] = jnp.where(incl, jnp.exp(jnp.where(incl, diff, 0.0)), 0.0)
        ch["k_beta"] = ch["k"] * ch["beta"]
        ch["strict"] = strict
    kqs = [_dot_nt(jnp.concatenate([ch["k_beta"].astype(BF16), ch["q"].astype(BF16)], axis=0), ch["k"])
           for ch in chains]
    lowers = [jnp.where(ch["strict"], kq[:CHUNK] * ch["decay_mask"], 0.0) for ch, kq in zip(chains, kqs)]
    ts = _unit_triangular_inverses(lowers)
    for ch, kq, t in zip(chains, kqs, ts):
        e_g = jnp.exp(ch["g_cum"])
        ch["t"] = t
        ch["rhs"] = jnp.concatenate([ch["v"] * ch["beta"], ch["k_beta"] * e_g], axis=-1)
        ch["attn"] = (kq[CHUNK:] * ch["decay_mask"]).astype(BF16)
        g_last = ch["g_cum"][CHUNK - 1:CHUNK] if ch["d"] == 0 else ch["g_cum"][0:1]
        ch["q_dec"] = (ch["q"] * e_g).astype(BF16)
        ch["k_tail_t"] = (ch["k"] * jnp.exp(g_last - ch["g_cum"])).T.astype(BF16)
        ch["decay"] = jnp.exp(g_last)
    xs = [_dot(ch["t"], ch["rhs"]) for ch in chains]
    for ch, x in zip(chains, xs):
        ch["u"], ch["w"] = x[:, :D_HEAD], x[:, D_HEAD:].astype(BF16)


def _gdn_recurrence(chains, states):
    sbs = [s.astype(BF16) for s in states]
    wss = [_dot(ch["w"], sb) for ch, sb in zip(chains, sbs)]
    vbs = [(ch["u"] - ws).astype(BF16) for ch, ws in zip(chains, wss)]
    outs = [_dot(jnp.concatenate([ch["q_dec"], ch["attn"]], axis=1), jnp.concatenate([sb, vb], axis=0))
            for ch, sb, vb in zip(chains, sbs, vbs)]
    upd = [_dot(ch["k_tail_t"], vb) for ch, vb in zip(chains, vbs)]
    return outs, [s * ch["decay"] + u for ch, s, u in zip(chains, states, upd)]


def _gdn_kernel(*refs, n_steps, chunks_per_block, has_s0, emit_state):
    it = iter(refs)
    xf_ref, gf_ref, xb_ref, gb_ref, cw_ref = (next(it) for _ in range(5))
    s0_ref = next(it) if has_s0 else None
    of_ref, ob_ref = next(it), next(it)
    st_ref = next(it) if emit_state else None
    s_ref = next(it)
    n = pl.program_id(1)

    @pl.when(n == 0)
    def _():
        if has_s0:
            s_ref[...] = s0_ref[0]
        else:
            s_ref[...] = jnp.zeros_like(s_ref)

    chains = [[] for _ in range(chunks_per_block)]
    for d, (x_ref, g_ref) in enumerate(((xf_ref, gf_ref), (xb_ref, gb_ref))):
        heads = _gdn_prepare(x_ref, cw_ref)
        tri3 = _tri3(d)
        for pos in range(chunks_per_block):
            ci = pos if d == 0 else chunks_per_block - 1 - pos
            rows = slice(ci * CHUNK, (ci + 1) * CHUNK)
            gates = g_ref[0, rows, :]
            cum = _cumsum_rows(tri3, gates)
            cum_t = cum.T
            for h in range(N_HEADS):
                jb, jg = d * N_HEADS + h, 2 * N_HEADS + d * N_HEADS + h
                q, k, v = (a[rows] for a in heads[h])
                chains[pos].append(dict(q=q, k=k, v=v, beta=gates[:, jb:jb + 1], g_cum=cum[:, jg:jg + 1],
                                        g_cum_row=cum_t[jg:jg + 1, :], d=d, h=h, rows=rows))
    _gdn_local([ch for pos in chains for ch in pos])

    states = [s_ref[ch["d"], ch["h"]] for ch in chains[0]]
    o_refs = (of_ref, ob_ref)
    for pos in range(chunks_per_block):
        outs, states = _gdn_recurrence(chains[pos], states)
        for ch, o in zip(chains[pos], outs):
            o_refs[ch["d"]][0, ch["rows"], ch["h"] * D_HEAD:(ch["h"] + 1) * D_HEAD] = o
    for ch, s in zip(chains[0], states):
        s_ref[ch["d"], ch["h"]] = s

    if emit_state:
        @pl.when(n == n_steps - 1)
        def _():
            st_ref[0] = s_ref[...]


def _gdn_scan(qkv3, gates3, conv_w, s0, block_rows, columns, emit_state):
    batch = qkv3.shape[0]
    n_row_blocks = qkv3.shape[1] // block_rows
    n_steps = n_row_blocks * columns
    assert n_row_blocks == 1 or columns == 1

    def fwd(b, n):
        return (b, n, 0) if columns == 1 else (b, 0, n)

    def bwd(b, n):
        return fwd(b, n_steps - 1 - n)

    spec = lambda width, imap: pl.BlockSpec((1, block_rows, width), imap)
    state_spec = pl.BlockSpec((1, 2, N_HEADS, D_HEAD, D_HEAD), lambda b, n: (b, 0, 0, 0, 0))
    in_specs = [spec(3 * W_GROUP, fwd), spec(GATE_LANES, fwd), spec(3 * W_GROUP, bwd), spec(GATE_LANES, bwd),
                pl.BlockSpec(conv_w.shape, lambda b, n: (0, 0))]
    args = [qkv3, gates3, qkv3, gates3, conv_w]
    if s0 is not None:
        in_specs.append(state_spec)
        args.append(s0)
    o_shape = jax.ShapeDtypeStruct((batch, qkv3.shape[1], columns * W_GROUP), F32)
    out_specs, out_shape = [spec(W_GROUP, fwd), spec(W_GROUP, bwd)], [o_shape, o_shape]
    if emit_state:
        out_specs.append(state_spec)
        out_shape.append(jax.ShapeDtypeStruct((batch, 2, N_HEADS, D_HEAD, D_HEAD), F32))
    return pl.pallas_call(
        functools.partial(_gdn_kernel, n_steps=n_steps, chunks_per_block=block_rows // CHUNK,
                          has_s0=s0 is not None, emit_state=emit_state),
        grid=(batch, n_steps),
        in_specs=in_specs, out_specs=out_specs, out_shape=out_shape,
        scratch_shapes=[pltpu.VMEM((2, N_HEADS, D_HEAD, D_HEAD), F32)],
        compiler_params=pltpu.CompilerParams(dimension_semantics=("arbitrary", "arbitrary"),
                                             vmem_limit_bytes=VMEM_LIMIT),
        name="gdn",
    )(*args)


def _mixout_kernel(x_ref, oaf_ref, oab_ref, obf_ref, obb_ref, ga_ref, zb_ref, mod_ref,
                   na_ref, nb_ref, wo_ref, n2_ref, x1_ref, h2_ref):
    def normed(of_ref, ob_ref, w_ref, gate_ref):
        parts = []
        for h in range(N_HEADS):
            sl = slice(h * D_HEAD, (h + 1) * D_HEAD)
            o = of_ref[:, sl] + ob_ref[:, sl]
            o = o * lax.rsqrt(jnp.mean(o * o, axis=-1, keepdims=True) + EPS)
            parts.append(o * w_ref[:, sl] * gate_ref[:, sl])
        return parts

    mixed = jnp.concatenate(normed(oaf_ref, oab_ref, na_ref, ga_ref)
                            + normed(obf_ref, obb_ref, nb_ref, zb_ref), axis=-1)
    m = mod_ref[0]
    x1 = x_ref[...] + m[2:3] * _dot(mixed, wo_ref[...])
    x1_ref[...] = x1
    y = x1 * lax.rsqrt(jnp.mean(x1 * x1, axis=-1, keepdims=True) + EPS) * n2_ref[...]
    h2_ref[...] = (y * (1.0 + m[4:5]) + m[3:4]).astype(BF16)


def _mixout(x2d, oaf, oab, obf, obb, ga, zb, mod3, mod_row_of_tile, norm_a, norm_b, w_out, norm2, tm):
    n_tok = x2d.shape[0]
    tok = lambda width: pl.BlockSpec((tm, width), lambda i: (i, 0))
    const = lambda shape: pl.BlockSpec(shape, lambda i: (0,) * len(shape))
    return pl.pallas_call(
        _mixout_kernel,
        grid=(n_tok // tm,),
        in_specs=[tok(D_MODEL)] + [tok(W_GROUP)] * 6
                 + [pl.BlockSpec((1, 6, D_MODEL), lambda i: (mod_row_of_tile(i), 0, 0)),
                    const((1, W_GROUP)), const((1, W_GROUP)), const(w_out.shape), const((1, D_MODEL))],
        out_specs=[tok(D_MODEL), tok(D_MODEL)],
        out_shape=[jax.ShapeDtypeStruct((n_tok, D_MODEL), F32), jax.ShapeDtypeStruct((n_tok, D_MODEL), BF16)],
        compiler_params=pltpu.CompilerParams(dimension_semantics=("arbitrary",),
                                             vmem_limit_bytes=VMEM_LIMIT),
        name="mixout",
    )(x2d, oaf, oab, obf, obb, ga, zb, mod3, norm_a, norm_b, w_out, norm2)


def _ffn_kernel(h_ref, x1_ref, mod_ref, wg_ref, wu_ref, wd_ref, nf_ref, y_ref):
    h = h_ref[...]
    gate = jnp.dot(h, wg_ref[...], preferred_element_type=F32)
    up = jnp.dot(h, wu_ref[...], preferred_element_type=F32)
    ff = _dot(_silu(gate) * up, wd_ref[...])
    x2 = x1_ref[...] + mod_ref[0][5:6] * ff
    y_ref[...] = x2 * lax.rsqrt(jnp.mean(x2 * x2, axis=-1, keepdims=True) + EPS) * nf_ref[...]


def _ffn(h2, x1, mod3, mod_row_of_tile, w_gate, w_up, w_down, norm_f, tm):
    n_tok = h2.shape[0]
    tok = pl.BlockSpec((tm, D_MODEL), lambda i: (i, 0))
    const = lambda shape: pl.BlockSpec(shape, lambda i: (0,) * len(shape))
    return pl.pallas_call(
        _ffn_kernel,
        grid=(n_tok // tm,),
        in_specs=[tok, tok, pl.BlockSpec((1, 6, D_MODEL), lambda i: (mod_row_of_tile(i), 0, 0)),
                  const(w_gate.shape), const(w_up.shape), const(w_down.shape), const((1, D_MODEL))],
        out_specs=tok,
        out_shape=jax.ShapeDtypeStruct((n_tok, D_MODEL), F32),
        compiler_params=pltpu.CompilerParams(dimension_semantics=("arbitrary",),
                                             vmem_limit_bytes=VMEM_LIMIT),
        name="ffn",
    )(h2, x1, mod3, w_gate, w_up, w_down, norm_f)


def _stream(x, mod3, mod_row_of_tile, s0_a, s0_b, p, latent):
    batch, seq, _ = x.shape
    x2d = x.reshape(batch * seq, D_MODEL)
    tm = 256
    qa, f_fwd, f_bwd, va, ga, qkv, zb, gates = _inproj(
        x2d, mod3, functools.partial(mod_row_of_tile, tm=tm), p["norm1"], p["lbp"], p["gparams"],
        p["w_main"], p["w_gates"], tm)

    hg = _hgrn_scan(qa, f_fwd, f_bwd, va, s0_a, batch, emit_state=not latent)
    if latent:
        rows = seq // GRID_W
        assert rows == CHUNK
        gd = _gdn_scan(qkv.reshape(batch, rows, GRID_W * 3 * W_GROUP), gates.reshape(batch, rows, GRID_W * GATE_LANES),
                       p["conv_w"], s0_b, block_rows=rows, columns=GRID_W, emit_state=False)
    else:
        gd = _gdn_scan(qkv.reshape(batch, seq, 3 * W_GROUP), gates.reshape(batch, seq, GATE_LANES),
                       p["conv_w"], s0_b, block_rows=seq, columns=1, emit_state=True)
    oaf, oab = hg[0], hg[1]
    obf, obb = (o.reshape(batch * seq, W_GROUP) for o in gd[:2])

    x1, h2 = _mixout(x2d, oaf, oab, obf, obb, ga, zb, mod3, functools.partial(mod_row_of_tile, tm=tm),
                     p["norm_a"], p["norm_b"], p["w_out"], p["norm2"], tm)
    tm_ffn = 512
    y = _ffn(h2, x1, mod3, functools.partial(mod_row_of_tile, tm=tm_ffn), p["w_gate"], p["w_up"], p["w_down"],
             p["norm_f"], tm_ffn)
    states = (None, None) if latent else (hg[2], gd[2])
    return y.reshape(batch, seq, D_MODEL), states


def kernel(x_prompt, x_sample, c, state_hgrn, state_gdn, c_ctx, w_ada, b_ada, norm1, norm2, w_in, conv_w,
           hgrn_lb, gdn_A_log, gdn_dt_bias, hgrn_out_norm, gdn_out_norm, w_out, w_gate, w_up, w_down, norm_f):
    depth = w_in.shape[0]
    assert depth == 1 and hgrn_lb.shape[0] == 2
    dec_batch, dec_seq, _ = x_sample.shape
    l = 0

    n_main = N_MAIN_GROUPS * W_GROUP
    pad8 = jnp.zeros((1, 2 * N_HEADS), F32)
    gparams = jnp.concatenate(
        [jnp.concatenate([pad8, a.reshape(1, 2 * N_HEADS).astype(F32),
                          jnp.zeros((1, GATE_LANES - 4 * N_HEADS), F32)], axis=1)
         for a in (gdn_A_log[l], gdn_dt_bias[l])], axis=0)
    p = {
        "norm1": norm1[l].reshape(1, D_MODEL), "norm2": norm2[l].reshape(1, D_MODEL),
        "lbp": hgrn_lb.reshape(2, 2 * W_GROUP), "gparams": gparams,
        "w_main": w_in[l][:, :n_main].astype(BF16),
        "w_gates": jnp.pad(w_in[l][:, n_main:], ((0, 0), (0, GATE_LANES - 4 * N_HEADS))).astype(BF16),
        "conv_w": conv_w[l],
        "norm_a": hgrn_out_norm[l].reshape(1, W_GROUP), "norm_b": gdn_out_norm[l].reshape(1, W_GROUP),
        "w_out": w_out[l].astype(BF16), "w_gate": w_gate[l].astype(BF16), "w_up": w_up[l].astype(BF16),
        "w_down": w_down[l].astype(BF16), "norm_f": norm_f.reshape(1, D_MODEL),
    }

    n_mod_rows = 8
    cvec = jnp.concatenate([c_ctx[None], c, jnp.zeros((n_mod_rows - 1 - dec_batch, D_MODEL), F32)], axis=0)
    mod3 = _modulation(cvec, w_ada[l], b_ada[l]).reshape(n_mod_rows, 6, D_MODEL)

    y_prompt, (new_a, new_b) = _stream(x_prompt, mod3, lambda i, tm: 0, None, None, p, latent=False)
    y_sample, _ = _stream(x_sample, mod3, lambda i, tm: 1 + i // (dec_seq // tm), state_hgrn[:, l],
                          state_gdn[:, l], p, latent=True)
    return y_prompt, y_sample, new_a[:, None], new_b[:, None]
```

```python
import functools

import jax
import jax.numpy as jnp
from jax import lax
from jax.experimental import pallas as pl
from jax.experimental.pallas import tpu as pltpu

F32 = jnp.float32
BF16 = jnp.bfloat16

D_MODEL = 1024
N_HEADS = 4
D_HEAD = 128
W_GROUP = N_HEADS * D_HEAD
CHUNK = 64
GRID_W = 64
CONV_W = 3
EPS = 1e-6
N_MAIN_GROUPS = 9
GATE_LANES = 128
VMEM_LIMIT = 56 * 1024 * 1024


def _sigmoid(x):
    return 1.0 / (1.0 + jnp.exp(-x))


def _silu(x):
    return x * _sigmoid(x)


def _dot(a, b):
    return jnp.dot(a.astype(BF16), b.astype(BF16), preferred_element_type=F32)


def _dot_nt(a, b):
    return lax.dot_general(a.astype(BF16), b.astype(BF16), (((1,), (1,)), ((), ())),
                           preferred_element_type=F32)


def _split3(x):
    x1 = x.astype(BF16)
    r = x - x1.astype(F32)
    x2 = r.astype(BF16)
    x3 = (r - x2.astype(F32)).astype(BF16)
    return x1, x2, x3


def _cumsum_rows(tri3, x):
    return jnp.dot(tri3, jnp.concatenate(_split3(x), axis=0), preferred_element_type=F32)


def _tri3(direction):
    tri = _tri_masks(direction)[0].astype(BF16)
    return jnp.concatenate([tri, tri, tri], axis=1)


def _tri_masks(direction):
    r = lax.broadcasted_iota(jnp.int32, (CHUNK, CHUNK), 0)
    c = lax.broadcasted_iota(jnp.int32, (CHUNK, CHUNK), 1)
    if direction == 0:
        return c <= r, c < r
    return c >= r, c > r


def _mod_kernel(c_ref, w_ref, b_ref, o_ref):
    s = _silu(c_ref[...])
    o_ref[...] = _dot(s, w_ref[...]) + b_ref[...]


def _modulation(cvec, w_ada, b_ada):
    n_rows, d = cvec.shape
    n_out = w_ada.shape[1]
    tn = 1536
    return pl.pallas_call(
        _mod_kernel,
        grid=(n_out // tn,),
        in_specs=[pl.BlockSpec((n_rows, d), lambda j: (0, 0)),
                  pl.BlockSpec((d, tn), lambda j: (0, j)),
                  pl.BlockSpec((1, tn), lambda j: (0, j))],
        out_specs=pl.BlockSpec((n_rows, tn), lambda j: (0, j)),
        out_shape=jax.ShapeDtypeStruct((n_rows, n_out), F32),
        compiler_params=pltpu.CompilerParams(dimension_semantics=("arbitrary",),
                                             vmem_limit_bytes=VMEM_LIMIT),
        name="mod",
    )(cvec, w_ada, b_ada.reshape(1, n_out))


def _inproj_kernel(x_ref, mod_ref, n1_ref, lb_ref, gp_ref, w_ref, wg_ref,
                   qa_ref, ff_ref, fb_ref, va_ref, ga_ref, qkv_ref, zb_ref, gates_ref):
    x = x_ref[...]
    m = mod_ref[0]
    y = x * lax.rsqrt(jnp.mean(x * x, axis=-1, keepdims=True) + EPS) * n1_ref[...]
    hb = (y * (1.0 + m[1:2]) + m[0:1]).astype(BF16)

    def proj(j):
        return jnp.dot(hb, w_ref[:, j * W_GROUP:(j + 1) * W_GROUP], preferred_element_type=F32)

    lbp = lb_ref[...]
    e = jnp.exp(lbp - jnp.max(lbp, axis=0, keepdims=True))
    lb = e[0:1] / jnp.sum(e, axis=0, keepdims=True)
    lb_f, lb_b = lb[:, :W_GROUP], lb[:, W_GROUP:]

    qa_ref[...] = _silu(proj(0))
    ff_ref[...] = lb_f + (1.0 - lb_f) * _sigmoid(proj(1))
    fb_ref[...] = lb_b + (1.0 - lb_b) * _sigmoid(proj(2))
    va_ref[...] = proj(3)
    ga_ref[...] = _silu(proj(4))
    for j in range(3):
        qkv_ref[:, j * W_GROUP:(j + 1) * W_GROUP] = proj(5 + j)
    zb_ref[...] = _silu(proj(8))

    raw = jnp.dot(hb, wg_ref[...], preferred_element_type=F32)
    gp = gp_ref[...]
    z = raw + gp[1:2]
    softplus = jnp.maximum(z, 0.0) + jnp.log(1.0 + jnp.exp(-jnp.abs(z)))
    lane = lax.broadcasted_iota(jnp.int32, raw.shape, 1)
    gates_ref[...] = jnp.where(lane < 2 * N_HEADS, _sigmoid(raw),
                               jnp.where(lane < 4 * N_HEADS, -jnp.exp(gp[0:1]) * softplus, 0.0))


def _inproj(x2d, mod3, mod_row_of_tile, norm1, lbp, gparams, w_main, w_gates, tm):
    n_tok = x2d.shape[0]
    tok = lambda width: pl.BlockSpec((tm, width), lambda i: (i, 0))
    const = lambda shape: pl.BlockSpec(shape, lambda i: (0,) * len(shape))
    out_widths = [W_GROUP] * 5 + [3 * W_GROUP, W_GROUP, GATE_LANES]
    return pl.pallas_call(
        _inproj_kernel,
        grid=(n_tok // tm,),
        in_specs=[tok(D_MODEL),
                  pl.BlockSpec((1, 6, D_MODEL), lambda i: (mod_row_of_tile(i), 0, 0)),
                  const((1, D_MODEL)), const(lbp.shape), const(gparams.shape),
                  const(w_main.shape), const(w_gates.shape)],
        out_specs=[tok(w) for w in out_widths],
        out_shape=[jax.ShapeDtypeStruct((n_tok, w), F32) for w in out_widths],
        compiler_params=pltpu.CompilerParams(dimension_semantics=("arbitrary",),
                                             vmem_limit_bytes=VMEM_LIMIT),
        name="inproj",
    )(x2d, mod3, norm1, lbp, gparams, w_main, w_gates)


def _hgrn_kernel(*refs, n_chunks, group, has_s0, emit_state):
    it = iter(refs)
    qf_ref, ff_ref, vf_ref, qb_ref, fb_ref, vb_ref = (next(it) for _ in range(6))
    s0_ref = next(it) if has_s0 else None
    of_ref, ob_ref = next(it), next(it)
    st_ref = next(it) if emit_state else None
    s_ref = next(it)
    n = pl.program_id(1)

    @pl.when(n == 0)
    def _():
        if has_s0:
            s_ref[...] = s0_ref[...]
        else:
            s_ref[...] = jnp.zeros_like(s_ref)

    chains = []
    for b in range(group):
        for d, (q_ref, f_ref, v_ref, o_ref) in enumerate(((qf_ref, ff_ref, vf_ref, of_ref),
                                                          (qb_ref, fb_ref, vb_ref, ob_ref))):
            incl, _ = _tri_masks(d)
            g_all = jnp.log(f_ref[b])
            cum_all = _cumsum_rows(_tri3(d), g_all)
            for h in range(N_HEADS):
                sl = slice(h * D_HEAD, (h + 1) * D_HEAD)
                k = 1.0 - f_ref[b, :, sl]
                g, G = g_all[:, sl], cum_all[:, sl]
                g_last_row = G[CHUNK - 1:CHUNK] if d == 0 else G[0:1]
                chains.append(dict(
                    b=b, d=d, h=h, sl=sl, o_ref=o_ref, incl=incl, vb=v_ref[b, :, sl].astype(BF16),
                    decay=jnp.exp(jnp.sum(g.T, axis=1, keepdims=True)),
                    q_dec=(q_ref[b, :, sl] * jnp.exp(G)).astype(BF16), k_dec=k * jnp.exp(-G),
                    k_tail_t=(k * jnp.exp(g_last_row - G)).T.astype(BF16)))
    attns = [jnp.where(ch["incl"], _dot_nt(ch["q_dec"], ch["k_dec"]), 0.0).astype(BF16) for ch in chains]
    states = [s_ref[ch["b"], ch["d"], ch["h"]] for ch in chains]
    outs = [_dot(jnp.concatenate([ch["q_dec"], attn], axis=1), jnp.concatenate([s.astype(BF16), ch["vb"]], axis=0))
            for ch, attn, s in zip(chains, attns, states)]
    upds = [_dot(ch["k_tail_t"], ch["vb"]) for ch in chains]
    for ch, o, s, u in zip(chains, outs, states, upds):
        ch["o_ref"][ch["b"], :, ch["sl"]] = o
        s_ref[ch["b"], ch["d"], ch["h"]] = ch["decay"] * s + u

    if emit_state:
        @pl.when(n == n_chunks - 1)
        def _():
            st_ref[...] = s_ref[...]


def _hgrn_scan(qa, f_fwd, f_bwd, va, s0, batch, group, emit_state):
    n_chunks = qa.shape[0] // batch // CHUNK
    to3 = lambda a: a.reshape(batch, n_chunks * CHUNK, W_GROUP)
    fwd = pl.BlockSpec((group, CHUNK, W_GROUP), lambda b, n: (b, n, 0))
    bwd = pl.BlockSpec((group, CHUNK, W_GROUP), lambda b, n: (b, n_chunks - 1 - n, 0))
    state_shape = (group, 2, N_HEADS, D_HEAD, D_HEAD)
    state_spec = pl.BlockSpec(state_shape, lambda b, n: (b, 0, 0, 0, 0))
    in_specs = [fwd, fwd, fwd, bwd, bwd, bwd]
    args = [to3(qa), to3(f_fwd), to3(va), to3(qa), to3(f_bwd), to3(va)]
    if s0 is not None:
        in_specs.append(state_spec)
        args.append(s0)
    out_specs = [fwd, bwd]
    out_shape = [jax.ShapeDtypeStruct((batch, n_chunks * CHUNK, W_GROUP), F32)] * 2
    if emit_state:
        out_specs.append(state_spec)
        out_shape.append(jax.ShapeDtypeStruct((batch,) + state_shape[1:], F32))
    res = pl.pallas_call(
        functools.partial(_hgrn_kernel, n_chunks=n_chunks, group=group, has_s0=s0 is not None,
                          emit_state=emit_state),
        grid=(batch // group, n_chunks),
        in_specs=in_specs, out_specs=out_specs, out_shape=out_shape,
        scratch_shapes=[pltpu.VMEM(state_shape, F32)],
        compiler_params=pltpu.CompilerParams(dimension_semantics=("arbitrary", "arbitrary"),
                                             vmem_limit_bytes=VMEM_LIMIT),
        name="hgrn",
    )(*args)
    return [r.reshape(-1, W_GROUP) for r in res[:2]] + list(res[2:])


def _gdn_prepare(x, cw_ref):
    n_rows = x.shape[0]
    row = lax.broadcasted_iota(jnp.int32, x.shape, 0)
    x_prev = jnp.where(row == 0, 0.0, pltpu.roll(x, 1, axis=0))
    x_next = jnp.where(row == n_rows - 1, 0.0, pltpu.roll(x, n_rows - 1, axis=0))
    cw = cw_ref[...]
    y = _silu(x_prev * cw[0:1] + x * cw[1:2] + x_next * cw[2:3])
    heads = []
    for h in range(N_HEADS):
        q = y[:, h * D_HEAD:(h + 1) * D_HEAD]
        k = y[:, W_GROUP + h * D_HEAD:W_GROUP + (h + 1) * D_HEAD]
        v = y[:, 2 * W_GROUP + h * D_HEAD:2 * W_GROUP + (h + 1) * D_HEAD]
        q = q * lax.rsqrt(jnp.sum(q * q, axis=-1, keepdims=True) + EPS) * (D_HEAD ** -0.5)
        k = k * lax.rsqrt(jnp.sum(k * k, axis=-1, keepdims=True) + EPS)
        heads.append((q, k, v))
    return heads


INVERSE_BASE_BLOCK = 8


def _same_block(block):
    r = lax.broadcasted_iota(jnp.int32, (CHUNK, CHUNK), 0) // block
    c = lax.broadcasted_iota(jnp.int32, (CHUNK, CHUNK), 1) // block
    return r == c


def _unit_triangular_inverses(lowers):
    r = lax.broadcasted_iota(jnp.int32, (CHUNK, CHUNK), 0)
    c = lax.broadcasted_iota(jnp.int32, (CHUNK, CHUNK), 1)
    eye = jnp.where(r == c, 1.0, 0.0)
    diag = _same_block(INVERSE_BASE_BLOCK)
    ds = [jnp.where(diag, lo, 0.0) for lo in lowers]
    ts = [eye - d for d in ds]
    ps = [_dot(d, d) for d in ds]
    power = 4
    while power < INVERSE_BASE_BLOCK:
        tps = [_dot(jnp.concatenate([t.astype(BF16), p.astype(BF16)], axis=0), p) for t, p in zip(ts, ps)]
        ts = [t + tp[:CHUNK] for t, tp in zip(ts, tps)]
        ps = [tp[CHUNK:] for tp in tps]
        power *= 2
    ts = [t + _dot(t, p) for t, p in zip(ts, ps)]
    block = INVERSE_BASE_BLOCK
    while block < CHUNK:
        off_mask = _same_block(2 * block) & jnp.logical_not(_same_block(block))
        ws = [_dot(t, jnp.where(off_mask, lo, 0.0)) for t, lo in zip(ts, lowers)]
        ts = [t - _dot(w, t) for t, w in zip(ts, ws)]
        block *= 2
    return ts


def _gdn_local(chains):
    for ch in chains:
        incl, strict = _tri_masks(ch["d"])
        diff = ch["g_cum"] - ch["g_cum_row"]
        ch["decay_mask"] = jnp.where(incl, jnp.exp(jnp.where(incl, diff, 0.0)), 0.0)
        ch["k_beta"] = ch["k"] * ch["beta"]
        ch["strict"] = strict
    kqs = [_dot_nt(jnp.concatenate([ch["k_beta"].astype(BF16), ch["q"].astype(BF16)], axis=0), ch["k"])
           for ch in chains]
    lowers = [jnp.where(ch["strict"], kq[:CHUNK] * ch["decay_mask"], 0.0) for ch, kq in zip(chains, kqs)]
    ts = _unit_triangular_inverses(lowers)
    for ch, kq, t in zip(chains, kqs, ts):
        e_g = jnp.exp(ch["g_cum"])
        ch["t"] = t
        ch["rhs"] = jnp.concatenate([ch["v"] * ch["beta"], ch["k_beta"] * e_g], axis=-1)
        ch["attn"] = (kq[CHUNK:] * ch["decay_mask"]).astype(BF16)
        g_last = ch["g_cum"][CHUNK - 1:CHUNK] if ch["d"] == 0 else ch["g_cum"][0:1]
        ch["q_dec"] = (ch["q"] * e_g).astype(BF16)
        ch["k_tail_t"] = (ch["k"] * jnp.exp(g_last - ch["g_cum"])).T.astype(BF16)
        ch["decay"] = jnp.exp(g_last)
    xs = [_dot(ch["t"], ch["rhs"]) for ch in chains]
    for ch, x in zip(chains, xs):
        ch["u"], ch["w"] = x[:, :D_HEAD], x[:, D_HEAD:].astype(BF16)


def _gdn_recurrence(chains, states):
    sbs = [s.astype(BF16) for s in states]
    wss = [_dot(ch["w"], sb) for ch, sb in zip(chains, sbs)]
    vbs = [(ch["u"] - ws).astype(BF16) for ch, ws in zip(chains, wss)]
    outs = [_dot(jnp.concatenate([ch["q_dec"], ch["attn"]], axis=1), jnp.concatenate([sb, vb], axis=0))
            for ch, sb, vb in zip(chains, sbs, vbs)]
    upd = [_dot(ch["k_tail_t"], vb) for ch, vb in zip(chains, vbs)]
    return outs, [s * ch["decay"] + u for ch, s, u in zip(chains, states, upd)]


def _gdn_kernel(*refs, n_steps, chunks_per_block, group, has_s0, emit_state):
    it = iter(refs)
    xf_ref, gf_ref, xb_ref, gb_ref, cw_ref = (next(it) for _ in range(5))
    s0_ref = next(it) if has_s0 else None
    of_ref, ob_ref = next(it), next(it)
    st_ref = next(it) if emit_state else None
    s_ref = next(it)
    n = pl.program_id(1)

    @pl.when(n == 0)
    def _():
        if has_s0:
            s_ref[...] = s0_ref[...]
        else:
            s_ref[...] = jnp.zeros_like(s_ref)

    chains = [[] for _ in range(chunks_per_block)]
    for b in range(group):
        for d, (x_ref, g_ref) in enumerate(((xf_ref, gf_ref), (xb_ref, gb_ref))):
            heads = _gdn_prepare(x_ref[b], cw_ref)
            tri3 = _tri3(d)
            for pos in range(chunks_per_block):
                ci = pos if d == 0 else chunks_per_block - 1 - pos
                rows = slice(ci * CHUNK, (ci + 1) * CHUNK)
                gates = g_ref[b, rows, :]
                cum = _cumsum_rows(tri3, gates)
                cum_t = cum.T
                for h in range(N_HEADS):
                    jb, jg = d * N_HEADS + h, 2 * N_HEADS + d * N_HEADS + h
                    q, k, v = (a[rows] for a in heads[h])
                    chains[pos].append(dict(q=q, k=k, v=v, beta=gates[:, jb:jb + 1], g_cum=cum[:, jg:jg + 1],
                                            g_cum_row=cum_t[jg:jg + 1, :], b=b, d=d, h=h, rows=rows))
    _gdn_local([ch for pos in chains for ch in pos])

    states = [s_ref[ch["b"], ch["d"], ch["h"]] for ch in chains[0]]
    o_refs = (of_ref, ob_ref)
    for pos in range(chunks_per_block):
        outs, states = _gdn_recurrence(chains[pos], states)
        for ch, o in zip(chains[pos], outs):
            o_refs[ch["d"]][ch["b"], ch["rows"], ch["h"] * D_HEAD:(ch["h"] + 1) * D_HEAD] = o
    for ch, s in zip(chains[0], states):
        s_ref[ch["b"], ch["d"], ch["h"]] = s

    if emit_state:
        @pl.when(n == n_steps - 1)
        def _():
            st_ref[...] = s_ref[...]


def _gdn_scan(qkv3, gates3, conv_w, s0, block_rows, columns, group, emit_state):
    batch = qkv3.shape[0]
    n_row_blocks = qkv3.shape[1] // block_rows
    n_steps = n_row_blocks * columns
    assert n_row_blocks == 1 or columns == 1

    def fwd(b, n):
        return (b, n, 0) if columns == 1 else (b, 0, n)

    def bwd(b, n):
        return fwd(b, n_steps - 1 - n)

    spec = lambda width, imap: pl.BlockSpec((group, block_rows, width), imap)
    state_shape = (group, 2, N_HEADS, D_HEAD, D_HEAD)
    state_spec = pl.BlockSpec(state_shape, lambda b, n: (b, 0, 0, 0, 0))
    in_specs = [spec(3 * W_GROUP, fwd), spec(GATE_LANES, fwd), spec(3 * W_GROUP, bwd), spec(GATE_LANES, bwd),
                pl.BlockSpec(conv_w.shape, lambda b, n: (0, 0))]
    args = [qkv3, gates3, qkv3, gates3, conv_w]
    if s0 is not None:
        in_specs.append(state_spec)
        args.append(s0)
    o_shape = jax.ShapeDtypeStruct((batch, qkv3.shape[1], columns * W_GROUP), F32)
    out_specs, out_shape = [spec(W_GROUP, fwd), spec(W_GROUP, bwd)], [o_shape, o_shape]
    if emit_state:
        out_specs.append(state_spec)
        out_shape.append(jax.ShapeDtypeStruct((batch,) + state_shape[1:], F32))
    return pl.pallas_call(
        functools.partial(_gdn_kernel, n_steps=n_steps, chunks_per_block=block_rows // CHUNK, group=group,
                          has_s0=s0 is not None, emit_state=emit_state),
        grid=(batch // group, n_steps),
        in_specs=in_specs, out_specs=out_specs, out_shape=out_shape,
        scratch_shapes=[pltpu.VMEM(state_shape, F32)],
        compiler_params=pltpu.CompilerParams(dimension_semantics=("arbitrary", "arbitrary"),
                                             vmem_limit_bytes=VMEM_LIMIT),
        name="gdn",
    )(*args)


def _mixout_kernel(x_ref, oaf_ref, oab_ref, obf_ref, obb_ref, ga_ref, zb_ref, mod_ref,
                   na_ref, nb_ref, wo_ref, n2_ref, x1_ref, h2_ref):
    def normed(of_ref, ob_ref, w_ref, gate_ref):
        parts = []
        for h in range(N_HEADS):
            sl = slice(h * D_HEAD, (h + 1) * D_HEAD)
            o = of_ref[:, sl] + ob_ref[:, sl]
            o = o * lax.rsqrt(jnp.mean(o * o, axis=-1, keepdims=True) + EPS)
            parts.append(o * w_ref[:, sl] * gate_ref[:, sl])
        return parts

    mixed = jnp.concatenate(normed(oaf_ref, oab_ref, na_ref, ga_ref)
                            + normed(obf_ref, obb_ref, nb_ref, zb_ref), axis=-1)
    m = mod_ref[0]
    x1 = x_ref[...] + m[2:3] * _dot(mixed, wo_ref[...])
    x1_ref[...] = x1
    y = x1 * lax.rsqrt(jnp.mean(x1 * x1, axis=-1, keepdims=True) + EPS) * n2_ref[...]
    h2_ref[...] = (y * (1.0 + m[4:5]) + m[3:4]).astype(BF16)


def _mixout(x2d, oaf, oab, obf, obb, ga, zb, mod3, mod_row_of_tile, norm_a, norm_b, w_out, norm2, tm):
    n_tok = x2d.shape[0]
    tok = lambda width: pl.BlockSpec((tm, width), lambda i: (i, 0))
    const = lambda shape: pl.BlockSpec(shape, lambda i: (0,) * len(shape))
    return pl.pallas_call(
        _mixout_kernel,
        grid=(n_tok // tm,),
        in_specs=[tok(D_MODEL)] + [tok(W_GROUP)] * 6
                 + [pl.BlockSpec((1, 6, D_MODEL), lambda i: (mod_row_of_tile(i), 0, 0)),
                    const((1, W_GROUP)), const((1, W_GROUP)), const(w_out.shape), const((1, D_MODEL))],
        out_specs=[tok(D_MODEL), tok(D_MODEL)],
        out_shape=[jax.ShapeDtypeStruct((n_tok, D_MODEL), F32), jax.ShapeDtypeStruct((n_tok, D_MODEL), BF16)],
        compiler_params=pltpu.CompilerParams(dimension_semantics=("arbitrary",),
                                             vmem_limit_bytes=VMEM_LIMIT),
        name="mixout",
    )(x2d, oaf, oab, obf, obb, ga, zb, mod3, norm_a, norm_b, w_out, norm2)


def _ffn_kernel(h_ref, x1_ref, mod_ref, wg_ref, wu_ref, wd_ref, nf_ref, y_ref):
    h = h_ref[...]
    gate = jnp.dot(h, wg_ref[...], preferred_element_type=F32)
    up = jnp.dot(h, wu_ref[...], preferred_element_type=F32)
    ff = _dot(_silu(gate) * up, wd_ref[...])
    x2 = x1_ref[...] + mod_ref[0][5:6] * ff
    y_ref[...] = x2 * lax.rsqrt(jnp.mean(x2 * x2, axis=-1, keepdims=True) + EPS) * nf_ref[...]


def _ffn(h2, x1, mod3, mod_row_of_tile, w_gate, w_up, w_down, norm_f, tm):
    n_tok = h2.shape[0]
    tok = pl.BlockSpec((tm, D_MODEL), lambda i: (i, 0))
    const = lambda shape: pl.BlockSpec(shape, lambda i: (0,) * len(shape))
    return pl.pallas_call(
        _ffn_kernel,
        grid=(n_tok // tm,),
        in_specs=[tok, tok, pl.BlockSpec((1, 6, D_MODEL), lambda i: (mod_row_of_tile(i), 0, 0)),
                  const(w_gate.shape), const(w_up.shape), const(w_down.shape), const((1, D_MODEL))],
        out_specs=tok,
        out_shape=jax.ShapeDtypeStruct((n_tok, D_MODEL), F32),
        compiler_params=pltpu.CompilerParams(dimension_semantics=("arbitrary",),
                                             vmem_limit_bytes=VMEM_LIMIT),
        name="ffn",
    )(h2, x1, mod3, w_gate, w_up, w_down, norm_f)


def _stream(x, mod3, mod_row_of_tile, s0_a, s0_b, p, latent):
    batch, seq, _ = x.shape
    x2d = x.reshape(batch * seq, D_MODEL)
    tm = 256
    qa, f_fwd, f_bwd, va, ga, qkv, zb, gates = _inproj(
        x2d, mod3, functools.partial(mod_row_of_tile, tm=tm), p["norm1"], p["lbp"], p["gparams"],
        p["w_main"], p["w_gates"], tm)

    hg = _hgrn_scan(qa, f_fwd, f_bwd, va, s0_a, batch, group=4, emit_state=not latent)
    if latent:
        rows = seq // GRID_W
        assert rows == CHUNK
        gd = _gdn_scan(qkv.reshape(batch, rows, GRID_W * 3 * W_GROUP), gates.reshape(batch, rows, GRID_W * GATE_LANES),
                       p["conv_w"], s0_b, block_rows=rows, columns=GRID_W, group=4, emit_state=False)
    else:
        gd = _gdn_scan(qkv.reshape(batch, seq, 3 * W_GROUP), gates.reshape(batch, seq, GATE_LANES),
                       p["conv_w"], s0_b, block_rows=seq, columns=1, group=2, emit_state=True)
    oaf, oab = hg[0], hg[1]
    obf, obb = (o.reshape(batch * seq, W_GROUP) for o in gd[:2])

    x1, h2 = _mixout(x2d, oaf, oab, obf, obb, ga, zb, mod3, functools.partial(mod_row_of_tile, tm=tm),
                     p["norm_a"], p["norm_b"], p["w_out"], p["norm2"], tm)
    tm_ffn = 512
    y = _ffn(h2, x1, mod3, functools.partial(mod_row_of_tile, tm=tm_ffn), p["w_gate"], p["w_up"], p["w_down"],
             p["norm_f"], tm_ffn)
    states = (None, None) if latent else (hg[2], gd[2])
    return y.reshape(batch, seq, D_MODEL), states


def kernel(x_prompt, x_sample, c, state_hgrn, state_gdn, c_ctx, w_ada, b_ada, norm1, norm2, w_in, conv_w,
           hgrn_lb, gdn_A_log, gdn_dt_bias, hgrn_out_norm, gdn_out_norm, w_out, w_gate, w_up, w_down, norm_f):
    depth = w_in.shape[0]
    assert depth == 1 and hgrn_lb.shape[0] == 2
    dec_batch, dec_seq, _ = x_sample.shape
    l = 0

    n_main = N_MAIN_GROUPS * W_GROUP
    pad8 = jnp.zeros((1, 2 * N_HEADS), F32)
    gparams = jnp.concatenate(
        [jnp.concatenate([pad8, a.reshape(1, 2 * N_HEADS).astype(F32),
                          jnp.zeros((1, GATE_LANES - 4 * N_HEADS), F32)], axis=1)
         for a in (gdn_A_log[l], gdn_dt_bias[l])], axis=0)
    p = {
        "norm1": norm1[l].reshape(1, D_MODEL), "norm2": norm2[l].reshape(1, D_MODEL),
        "lbp": hgrn_lb.reshape(2, 2 * W_GROUP), "gparams": gparams,
        "w_main": w_in[l][:, :n_main].astype(BF16),
        "w_gates": jnp.pad(w_in[l][:, n_main:], ((0, 0), (0, GATE_LANES - 4 * N_HEADS))).astype(BF16),
        "conv_w": conv_w[l],
        "norm_a": hgrn_out_norm[l].reshape(1, W_GROUP), "norm_b": gdn_out_norm[l].reshape(1, W_GROUP),
        "w_out": w_out[l].astype(BF16), "w_gate": w_gate[l].astype(BF16), "w_up": w_up[l].astype(BF16),
        "w_down": w_down[l].astype(BF16), "norm_f": norm_f.reshape(1, D_MODEL),
    }

    n_mod_rows = 8
    cvec = jnp.concatenate([c_ctx[None], c, jnp.zeros((n_mod_rows - 1 - dec_batch, D_MODEL), F32)], axis=0)
    mod3 = _modulation(cvec, w_ada[l], b_ada[l]).reshape(n_mod_rows, 6, D_MODEL)

    y_prompt, (new_a, new_b) = _stream(x_prompt, mod3, lambda i, tm: 0, None, None, p, latent=False)
    y_sample, _ = _stream(x_sample, mod3, lambda i, tm: 1 + i // (dec_seq // tm), state_hgrn[:, l],
                          state_gdn[:, l], p, latent=True)
    return y_prompt, y_sample, new_a[:, None], new_b[:, None]
```

```python
import functools

import jax
import jax.numpy as jnp
from jax import lax
from jax.experimental import pallas as pl
from jax.experimental.pallas import tpu as pltpu

F32 = jnp.float32
BF16 = jnp.bfloat16

D_MODEL = 1024
N_HEADS = 4
D_HEAD = 128
W_GROUP = N_HEADS * D_HEAD
CHUNK = 64
GRID_W = 64
CONV_W = 3
EPS = 1e-6
N_MAIN_GROUPS = 9
GATE_LANES = 128
VMEM_LIMIT = 56 * 1024 * 1024


def _sigmoid(x):
    return 1.0 / (1.0 + jnp.exp(-x))


def _silu(x):
    return x * _sigmoid(x)


def _dot(a, b):
    return jnp.dot(a.astype(BF16), b.astype(BF16), preferred_element_type=F32)


def _dot_nt(a, b):
    return lax.dot_general(a.astype(BF16), b.astype(BF16), (((1,), (1,)), ((), ())),
                           preferred_element_type=F32)


def _split3(x):
    x1 = x.astype(BF16)
    r = x - x1.astype(F32)
    x2 = r.astype(BF16)
    x3 = (r - x2.astype(F32)).astype(BF16)
    return x1, x2, x3


def _cumsum_rows(tri3, x):
    return jnp.dot(tri3, jnp.concatenate(_split3(x), axis=0), preferred_element_type=F32)


def _tri3(direction):
    tri = _tri_masks(direction)[0].astype(BF16)
    return jnp.concatenate([tri, tri, tri], axis=1)


def _tri_masks(direction):
    r = lax.broadcasted_iota(jnp.int32, (CHUNK, CHUNK), 0)
    c = lax.broadcasted_iota(jnp.int32, (CHUNK, CHUNK), 1)
    if direction == 0:
        return c <= r, c < r
    return c >= r, c > r


def _resident(shape):
    return pl.BlockSpec(shape, lambda i: (0,) * len(shape), pipeline_mode=pl.Buffered(1))


def _mod_kernel(c_ref, w_ref, b_ref, o_ref):
    s = _silu(c_ref[...])
    o_ref[...] = _dot(s, w_ref[...]) + b_ref[...]


def _modulation(cvec, w_ada, b_ada):
    n_rows, d = cvec.shape
    n_out = w_ada.shape[1]
    tn = 1536
    return pl.pallas_call(
        _mod_kernel,
        grid=(n_out // tn,),
        in_specs=[pl.BlockSpec((n_rows, d), lambda j: (0, 0)),
                  pl.BlockSpec((d, tn), lambda j: (0, j)),
                  pl.BlockSpec((1, tn), lambda j: (0, j))],
        out_specs=pl.BlockSpec((n_rows, tn), lambda j: (0, j)),
        out_shape=jax.ShapeDtypeStruct((n_rows, n_out), F32),
        compiler_params=pltpu.CompilerParams(dimension_semantics=("arbitrary",),
                                             vmem_limit_bytes=VMEM_LIMIT),
        name="mod",
    )(cvec, w_ada, b_ada.reshape(1, n_out))


HALO = GRID_W
LANE = 128
QKV_TILES = 3 * W_GROUP // LANE
ROW_PITCH = GRID_W + 8


def _inproj_kernel(*refs, latent, seq_len):
    it = iter(refs)
    x_ref = next(it)
    xp_ref, xn_ref = (next(it), next(it)) if latent else (None, None)
    mod_ref, n1_ref, lb_ref, gp_ref, cw_ref, w_ref, wg_ref = (next(it) for _ in range(7))
    qa_ref, ff_ref, fb_ref, va_ref, ga_ref, qkv_ref, zb_ref, gates_ref = (next(it) for _ in range(8))
    qkv_scr, gates_scr = (next(it), next(it)) if latent else (None, None)
    tm = x_ref.shape[0]
    m = mod_ref[0]

    def normed(x):
        y = x * lax.rsqrt(jnp.mean(x * x, axis=-1, keepdims=True) + EPS) * n1_ref[...]
        return (y * (1.0 + m[1:2]) + m[0:1]).astype(BF16)

    hb = normed(x_ref[...])

    w_qkv = w_ref[:, 5 * W_GROUP:8 * W_GROUP]
    if latent:
        i = pl.program_id(0)
        tiles_per_seq = seq_len // tm
        ext = jnp.dot(jnp.concatenate([normed(xp_ref[...]), hb, normed(xn_ref[...])], axis=0), w_qkv,
                      preferred_element_type=F32)
        above = jnp.where(i % tiles_per_seq == 0, 0.0, ext[:HALO])
        below = jnp.where(i % tiles_per_seq == tiles_per_seq - 1, 0.0, ext[HALO + tm:])
        cur = ext[HALO:HALO + tm]
        prev = jnp.concatenate([above, ext[HALO:tm]], axis=0)
        nxt = jnp.concatenate([ext[2 * HALO:HALO + tm], below], axis=0)
    else:
        cur = jnp.dot(hb, w_qkv, preferred_element_type=F32)
        pos = lax.broadcasted_iota(jnp.int32, cur.shape, 0) % seq_len
        prev = jnp.where(pos == 0, 0.0, pltpu.roll(cur, 1, axis=0))
        nxt = jnp.where(pos == seq_len - 1, 0.0, pltpu.roll(cur, tm - 1, axis=0))
    cw = cw_ref[...]
    y = _silu(prev * cw[0:1] + cur * cw[1:2] + nxt * cw[2:3])
    tiles = []
    for j in range(QKV_TILES):
        t = y[:, j * LANE:(j + 1) * LANE]
        if j < 2 * N_HEADS:
            t = t * lax.rsqrt(jnp.sum(t * t, axis=-1, keepdims=True) + EPS)
        if j < N_HEADS:
            t = t * (D_HEAD ** -0.5)
        tiles.append(t)

    raw = jnp.dot(hb, wg_ref[...], preferred_element_type=F32)
    gp = gp_ref[...]
    z = raw + gp[1:2]
    softplus = jnp.maximum(z, 0.0) + jnp.log(1.0 + jnp.exp(-jnp.abs(z)))
    lane = lax.broadcasted_iota(jnp.int32, raw.shape, 1)
    gates = jnp.where(lane < 2 * N_HEADS, _sigmoid(raw),
                      jnp.where(lane < 4 * N_HEADS, -jnp.exp(gp[0:1]) * softplus, 0.0))

    if not latent:
        for j, t in enumerate(tiles):
            qkv_ref[:, j * LANE:(j + 1) * LANE] = t
        gates_ref[...] = gates
    else:
        n_rows = tm // GRID_W
        for r in range(n_rows):
            rows, dst = slice(r * GRID_W, (r + 1) * GRID_W), slice(r * ROW_PITCH, r * ROW_PITCH + GRID_W)
            for j, t in enumerate(tiles):
                qkv_scr[j, dst, :] = t[rows]
            gates_scr[dst, :] = gates[rows]
        for c in range(GRID_W):
            for j in range(QKV_TILES):
                qkv_ref[0, :, (c * QKV_TILES + j) * LANE:(c * QKV_TILES + j + 1) * LANE] = (
                    qkv_scr[j, pl.ds(c, n_rows, stride=ROW_PITCH), :])
            gates_ref[0, :, c * GATE_LANES:(c + 1) * GATE_LANES] = gates_scr[pl.ds(c, n_rows, stride=ROW_PITCH), :]

    def proj(j):
        return jnp.dot(hb, w_ref[:, j * W_GROUP:(j + 1) * W_GROUP], preferred_element_type=F32)

    lbp = lb_ref[...]
    e = jnp.exp(lbp - jnp.max(lbp, axis=0, keepdims=True))
    lb = e[0:1] / jnp.sum(e, axis=0, keepdims=True)
    lb_f, lb_b = lb[:, :W_GROUP], lb[:, W_GROUP:]

    qa_ref[...] = _silu(proj(0))
    ff_ref[...] = lb_f + (1.0 - lb_f) * _sigmoid(proj(1))
    fb_ref[...] = lb_b + (1.0 - lb_b) * _sigmoid(proj(2))
    va_ref[...] = proj(3)
    ga_ref[...] = _silu(proj(4))
    zb_ref[...] = _silu(proj(8))


def _inproj(x2d, mod3, mod_row_of_tile, p, tm, latent, seq_len):
    n_tok = x2d.shape[0]
    tok = lambda width: pl.BlockSpec((tm, width), lambda i: (i, 0))
    widths = [W_GROUP] * 5 + [3 * W_GROUP, W_GROUP, GATE_LANES]
    out_specs = [tok(w) for w in widths]
    out_shape = [jax.ShapeDtypeStruct((n_tok, w), F32) for w in widths]
    in_specs, args, scratch = [tok(D_MODEL)], [x2d], []
    if latent:
        n_halo_blocks, per_tile, rows = n_tok // HALO, tm // HALO, tm // GRID_W
        tiles_per_seq = seq_len // tm
        in_specs += [pl.BlockSpec((HALO, D_MODEL), lambda i: (jnp.maximum(i * per_tile - 1, 0), 0)),
                     pl.BlockSpec((HALO, D_MODEL), lambda i: (jnp.minimum((i + 1) * per_tile, n_halo_blocks - 1), 0))]
        args += [x2d, x2d]
        col = lambda width: pl.BlockSpec((1, rows, GRID_W * width),
                                         lambda i: (i // tiles_per_seq, i % tiles_per_seq, 0))
        col_shape = lambda width: jax.ShapeDtypeStruct((n_tok // seq_len, seq_len // GRID_W, GRID_W * width), F32)
        for k, width in ((5, 3 * W_GROUP), (7, GATE_LANES)):
            out_specs[k], out_shape[k] = col(width), col_shape(width)
        scratch = [pltpu.VMEM((QKV_TILES, rows * ROW_PITCH, LANE), F32), pltpu.VMEM((rows * ROW_PITCH, GATE_LANES), F32)]
    consts = [p["norm1"], p["lbp"], p["gparams"], p["conv_w"], p["w_main"], p["w_gates"]]
    in_specs += [pl.BlockSpec((1, 6, D_MODEL), lambda i: (mod_row_of_tile(i), 0, 0))] + [_resident(a.shape) for a in consts]
    return pl.pallas_call(
        functools.partial(_inproj_kernel, latent=latent, seq_len=seq_len),
        grid=(n_tok // tm,),
        in_specs=in_specs, out_specs=out_specs, out_shape=out_shape, scratch_shapes=scratch,
        compiler_params=pltpu.CompilerParams(dimension_semantics=("arbitrary",),
                                             vmem_limit_bytes=VMEM_LIMIT),
        name="inproj",
    )(*args, mod3, *consts)


def _hgrn_kernel(*refs, n_chunks, group, has_s0, emit_state):
    it = iter(refs)
    qf_ref, ff_ref, vf_ref, qb_ref, fb_ref, vb_ref = (next(it) for _ in range(6))
    s0_ref = next(it) if has_s0 else None
    of_ref, ob_ref = next(it), next(it)
    st_ref = next(it) if emit_state else None
    s_ref = next(it)
    n = pl.program_id(1)

    @pl.when(n == 0)
    def _():
        if has_s0:
            s_ref[...] = s0_ref[...]
        else:
            s_ref[...] = jnp.zeros_like(s_ref)

    chains = []
    for b in range(group):
        for d, (q_ref, f_ref, v_ref, o_ref) in enumerate(((qf_ref, ff_ref, vf_ref, of_ref),
                                                          (qb_ref, fb_ref, vb_ref, ob_ref))):
            incl, _ = _tri_masks(d)
            g_all = jnp.log(f_ref[b])
            cum_all = _cumsum_rows(_tri3(d), g_all)
            for h in range(N_HEADS):
                sl = slice(h * D_HEAD, (h + 1) * D_HEAD)
                k = 1.0 - f_ref[b, :, sl]
                g, G = g_all[:, sl], cum_all[:, sl]
                g_last_row = G[CHUNK - 1:CHUNK] if d == 0 else G[0:1]
                chains.append(dict(
                    b=b, d=d, h=h, sl=sl, o_ref=o_ref, incl=incl, vb=v_ref[b, :, sl].astype(BF16),
                    decay=jnp.exp(jnp.sum(g.T, axis=1, keepdims=True)),
                    q_dec=(q_ref[b, :, sl] * jnp.exp(G)).astype(BF16), k_dec=k * jnp.exp(-G),
                    k_tail_t=(k * jnp.exp(g_last_row - G)).T.astype(BF16)))
    attns = [jnp.where(ch["incl"], _dot_nt(ch["q_dec"], ch["k_dec"]), 0.0).astype(BF16) for ch in chains]
    states = [s_ref[ch["b"], ch["d"], ch["h"]] for ch in chains]
    outs = [_dot(jnp.concatenate([ch["q_dec"], attn], axis=1), jnp.concatenate([s.astype(BF16), ch["vb"]], axis=0))
            for ch, attn, s in zip(chains, attns, states)]
    upds = [_dot(ch["k_tail_t"], ch["vb"]) for ch in chains]
    for ch, o, s, u in zip(chains, outs, states, upds):
        ch["o_ref"][ch["b"], :, ch["sl"]] = o
        s_ref[ch["b"], ch["d"], ch["h"]] = ch["decay"] * s + u

    if emit_state:
        @pl.when(n == n_chunks - 1)
        def _():
            st_ref[...] = s_ref[...]


def _hgrn_scan(qa, f_fwd, f_bwd, va, s0, batch, group, emit_state):
    n_chunks = qa.shape[0] // batch // CHUNK
    to3 = lambda a: a.reshape(batch, n_chunks * CHUNK, W_GROUP)
    fwd = pl.BlockSpec((group, CHUNK, W_GROUP), lambda b, n: (b, n, 0))
    bwd = pl.BlockSpec((group, CHUNK, W_GROUP), lambda b, n: (b, n_chunks - 1 - n, 0))
    state_shape = (group, 2, N_HEADS, D_HEAD, D_HEAD)
    state_spec = pl.BlockSpec(state_shape, lambda b, n: (b, 0, 0, 0, 0))
    in_specs = [fwd, fwd, fwd, bwd, bwd, bwd]
    args = [to3(qa), to3(f_fwd), to3(va), to3(qa), to3(f_bwd), to3(va)]
    if s0 is not None:
        in_specs.append(state_spec)
        args.append(s0)
    out_specs = [fwd, bwd]
    out_shape = [jax.ShapeDtypeStruct((batch, n_chunks * CHUNK, W_GROUP), F32)] * 2
    if emit_state:
        out_specs.append(state_spec)
        out_shape.append(jax.ShapeDtypeStruct((batch,) + state_shape[1:], F32))
    res = pl.pallas_call(
        functools.partial(_hgrn_kernel, n_chunks=n_chunks, group=group, has_s0=s0 is not None,
                          emit_state=emit_state),
        grid=(batch // group, n_chunks),
        in_specs=in_specs, out_specs=out_specs, out_shape=out_shape,
        scratch_shapes=[pltpu.VMEM(state_shape, F32)],
        compiler_params=pltpu.CompilerParams(dimension_semantics=("arbitrary", "arbitrary"),
                                             vmem_limit_bytes=VMEM_LIMIT),
        name="hgrn",
    )(*args)
    return [r.reshape(-1, W_GROUP) for r in res[:2]] + list(res[2:])


INVERSE_BASE_BLOCK = 8


def _same_block(block):
    r = lax.broadcasted_iota(jnp.int32, (CHUNK, CHUNK), 0) // block
    c = lax.broadcasted_iota(jnp.int32, (CHUNK, CHUNK), 1) // block
    return r == c


def _unit_triangular_inverses(lowers):
    r = lax.broadcasted_iota(jnp.int32, (CHUNK, CHUNK), 0)
    c = lax.broadcasted_iota(jnp.int32, (CHUNK, CHUNK), 1)
    eye = jnp.where(r == c, 1.0, 0.0)
    diag = _same_block(INVERSE_BASE_BLOCK)
    ds = [jnp.where(diag, lo, 0.0) for lo in lowers]
    ts = [eye - d for d in ds]
    ps = [_dot(d, d) for d in ds]
    power = 4
    while power < INVERSE_BASE_BLOCK:
        tps = [_dot(jnp.concatenate([t.astype(BF16), p.astype(BF16)], axis=0), p) for t, p in zip(ts, ps)]
        ts = [t + tp[:CHUNK] for t, tp in zip(ts, tps)]
        ps = [tp[CHUNK:] for tp in tps]
        power *= 2
    ts = [t + _dot(t, p) for t, p in zip(ts, ps)]
    block = INVERSE_BASE_BLOCK
    while block < CHUNK:
        off_mask = _same_block(2 * block) & jnp.logical_not(_same_block(block))
        ws = [_dot(t, jnp.where(off_mask, lo, 0.0)) for t, lo in zip(ts, lowers)]
        ts = [t - _dot(w, t) for t, w in zip(ts, ws)]
        block *= 2
    return ts


def _gdn_local(chains):
    for ch in chains:
        incl, strict = _tri_masks(ch["d"])
        diff = ch["g_cum"] - ch["g_cum_row"]
        ch["decay_mask"] = jnp.where(incl, jnp.exp(jnp.where(incl, diff, 0.0)), 0.0)
        ch["k_beta"] = ch["k"] * ch["beta"]
        ch["strict"] = strict
    kqs = [_dot_nt(jnp.concatenate([ch["k_beta"].astype(BF16), ch["q"].astype(BF16)], axis=0), ch["k"])
           for ch in chains]
    lowers = [jnp.where(ch["strict"], kq[:CHUNK] * ch["decay_mask"], 0.0) for ch, kq in zip(chains, kqs)]
    ts = _unit_triangular_inverses(lowers)
    for ch, kq, t in zip(chains, kqs, ts):
        e_g = jnp.exp(ch["g_cum"])
        ch["t"] = t
        ch["rhs"] = jnp.concatenate([ch["v"] * ch["beta"], ch["k_beta"] * e_g], axis=-1)
        ch["attn"] = (kq[CHUNK:] * ch["decay_mask"]).astype(BF16)
        g_last = ch["g_cum"][CHUNK - 1:CHUNK] if ch["d"] == 0 else ch["g_cum"][0:1]
        ch["q_dec"] = (ch["q"] * e_g).astype(BF16)
        ch["k_tail_t"] = (ch["k"] * jnp.exp(g_last - ch["g_cum"])).T.astype(BF16)
        ch["decay"] = jnp.exp(g_last)
    xs = [_dot(ch["t"], ch["rhs"]) for ch in chains]
    for ch, x in zip(chains, xs):
        ch["u"], ch["w"] = x[:, :D_HEAD], x[:, D_HEAD:].astype(BF16)


def _gdn_recurrence(chains, states):
    sbs = [s.astype(BF16) for s in states]
    wss = [_dot(ch["w"], sb) for ch, sb in zip(chains, sbs)]
    vbs = [(ch["u"] - ws).astype(BF16) for ch, ws in zip(chains, wss)]
    outs = [_dot(jnp.concatenate([ch["q_dec"], ch["attn"]], axis=1), jnp.concatenate([sb, vb], axis=0))
            for ch, sb, vb in zip(chains, sbs, vbs)]
    upd = [_dot(ch["k_tail_t"], vb) for ch, vb in zip(chains, vbs)]
    return outs, [s * ch["decay"] + u for ch, s, u in zip(chains, states, upd)]


def _gdn_kernel(*refs, n_steps, chunks_per_block, group, has_s0, emit_state):
    it = iter(refs)
    xf_ref, gf_ref, xb_ref, gb_ref = (next(it) for _ in range(4))
    s0_ref = next(it) if has_s0 else None
    of_ref, ob_ref = next(it), next(it)
    st_ref = next(it) if emit_state else None
    s_ref = next(it)
    n = pl.program_id(1)

    @pl.when(n == 0)
    def _():
        if has_s0:
            s_ref[...] = s0_ref[...]
        else:
            s_ref[...] = jnp.zeros_like(s_ref)

    chains = [[] for _ in range(chunks_per_block)]
    for b in range(group):
        for d, (x_ref, g_ref) in enumerate(((xf_ref, gf_ref), (xb_ref, gb_ref))):
            x = x_ref[b]
            heads = [tuple(x[:, (part * N_HEADS + h) * D_HEAD:(part * N_HEADS + h + 1) * D_HEAD] for part in range(3))
                     for h in range(N_HEADS)]
            tri3 = _tri3(d)
            for pos in range(chunks_per_block):
                ci = pos if d == 0 else chunks_per_block - 1 - pos
                rows = slice(ci * CHUNK, (ci + 1) * CHUNK)
                gates = g_ref[b, rows, :]
                cum = _cumsum_rows(tri3, gates)
                cum_t = cum.T
                for h in range(N_HEADS):
                    jb, jg = d * N_HEADS + h, 2 * N_HEADS + d * N_HEADS + h
                    q, k, v = (a[rows] for a in heads[h])
                    chains[pos].append(dict(q=q, k=k, v=v, beta=gates[:, jb:jb + 1], g_cum=cum[:, jg:jg + 1],
                                            g_cum_row=cum_t[jg:jg + 1, :], b=b, d=d, h=h, rows=rows))
    _gdn_local([ch for pos in chains for ch in pos])

    states = [s_ref[ch["b"], ch["d"], ch["h"]] for ch in chains[0]]
    o_refs = (of_ref, ob_ref)
    for pos in range(chunks_per_block):
        outs, states = _gdn_recurrence(chains[pos], states)
        for ch, o in zip(chains[pos], outs):
            o_refs[ch["d"]][ch["b"], ch["rows"], ch["h"] * D_HEAD:(ch["h"] + 1) * D_HEAD] = o
    for ch, s in zip(chains[0], states):
        s_ref[ch["b"], ch["d"], ch["h"]] = s

    if emit_state:
        @pl.when(n == n_steps - 1)
        def _():
            st_ref[...] = s_ref[...]


def _gdn_scan(qkv3, gates3, s0, block_rows, columns, group, emit_state):
    batch = qkv3.shape[0]
    n_row_blocks = qkv3.shape[1] // block_rows
    n_steps = n_row_blocks * columns
    assert n_row_blocks == 1 or columns == 1

    def fwd(b, n):
        return (b, n, 0) if columns == 1 else (b, 0, n)

    def bwd(b, n):
        return fwd(b, n_steps - 1 - n)

    spec = lambda width, imap: pl.BlockSpec((group, block_rows, width), imap)
    state_shape = (group, 2, N_HEADS, D_HEAD, D_HEAD)
    state_spec = pl.BlockSpec(state_shape, lambda b, n: (b, 0, 0, 0, 0))
    in_specs = [spec(3 * W_GROUP, fwd), spec(GATE_LANES, fwd), spec(3 * W_GROUP, bwd), spec(GATE_LANES, bwd)]
    args = [qkv3, gates3, qkv3, gates3]
    if s0 is not None:
        in_specs.append(state_spec)
        args.append(s0)
    o_shape = jax.ShapeDtypeStruct((batch, qkv3.shape[1], columns * W_GROUP), F32)
    out_specs, out_shape = [spec(W_GROUP, fwd), spec(W_GROUP, bwd)], [o_shape, o_shape]
    if emit_state:
        out_specs.append(state_spec)
        out_shape.append(jax.ShapeDtypeStruct((batch,) + state_shape[1:], F32))
    return pl.pallas_call(
        functools.partial(_gdn_kernel, n_steps=n_steps, chunks_per_block=block_rows // CHUNK, group=group,
                          has_s0=s0 is not None, emit_state=emit_state),
        grid=(batch // group, n_steps),
        in_specs=in_specs, out_specs=out_specs, out_shape=out_shape,
        scratch_shapes=[pltpu.VMEM(state_shape, F32)],
        compiler_params=pltpu.CompilerParams(dimension_semantics=("arbitrary", "arbitrary"),
                                             vmem_limit_bytes=VMEM_LIMIT),
        name="gdn",
    )(*args)


def _mixout_kernel(*refs, latent):
    it = iter(refs)
    x_ref, oaf_ref, oab_ref, obf_ref, obb_ref, ga_ref, zb_ref, mod_ref, na_ref, nb_ref, wo_ref, n2_ref = (
        next(it) for _ in range(12))
    x1_ref, h2_ref = next(it), next(it)
    ob_scr = next(it) if latent else None
    tm = x_ref.shape[0]

    def gated_norm(o, w_ref, gate_ref, h):
        sl = slice(h * D_HEAD, (h + 1) * D_HEAD)
        o = o * lax.rsqrt(jnp.mean(o * o, axis=-1, keepdims=True) + EPS)
        return o * w_ref[:, sl] * gate_ref[:, sl]

    if latent:
        n_rows = tm // GRID_W
        for c in range(GRID_W):
            for h in range(N_HEADS):
                sl = slice(c * W_GROUP + h * D_HEAD, c * W_GROUP + (h + 1) * D_HEAD)
                ob_scr[h, pl.ds(c, n_rows, stride=ROW_PITCH), :] = obf_ref[0, :, sl] + obb_ref[0, :, sl]
        o_b = [jnp.concatenate([ob_scr[h, r * ROW_PITCH:r * ROW_PITCH + GRID_W, :] for r in range(n_rows)], axis=0)
               for h in range(N_HEADS)]
    else:
        o_b = [obf_ref[:, h * D_HEAD:(h + 1) * D_HEAD] + obb_ref[:, h * D_HEAD:(h + 1) * D_HEAD]
               for h in range(N_HEADS)]
    o_a = [oaf_ref[:, h * D_HEAD:(h + 1) * D_HEAD] + oab_ref[:, h * D_HEAD:(h + 1) * D_HEAD] for h in range(N_HEADS)]
    mixed = jnp.concatenate([gated_norm(o, na_ref, ga_ref, h) for h, o in enumerate(o_a)]
                            + [gated_norm(o, nb_ref, zb_ref, h) for h, o in enumerate(o_b)], axis=-1)
    m = mod_ref[0]
    x1 = x_ref[...] + m[2:3] * _dot(mixed, wo_ref[...])
    x1_ref[...] = x1
    y = x1 * lax.rsqrt(jnp.mean(x1 * x1, axis=-1, keepdims=True) + EPS) * n2_ref[...]
    h2_ref[...] = (y * (1.0 + m[4:5]) + m[3:4]).astype(BF16)


def _mixout(x2d, oaf, oab, obf, obb, ga, zb, mod3, mod_row_of_tile, p, tm, latent, seq_len):
    n_tok = x2d.shape[0]
    tok = lambda width: pl.BlockSpec((tm, width), lambda i: (i, 0))
    ob_spec, scratch = tok(W_GROUP), []
    if latent:
        tiles_per_seq = seq_len // tm
        ob_spec = pl.BlockSpec((1, tm // GRID_W, GRID_W * W_GROUP), lambda i: (i // tiles_per_seq, i % tiles_per_seq, 0))
        scratch = [pltpu.VMEM((N_HEADS, tm // GRID_W * ROW_PITCH, D_HEAD), F32)]
    consts = [p["norm_a"], p["norm_b"], p["w_out"], p["norm2"]]
    return pl.pallas_call(
        functools.partial(_mixout_kernel, latent=latent),
        grid=(n_tok // tm,),
        in_specs=[tok(D_MODEL), tok(W_GROUP), tok(W_GROUP), ob_spec, ob_spec, tok(W_GROUP), tok(W_GROUP),
                  pl.BlockSpec((1, 6, D_MODEL), lambda i: (mod_row_of_tile(i), 0, 0))]
                 + [_resident(a.shape) for a in consts],
        out_specs=[tok(D_MODEL), tok(D_MODEL)],
        out_shape=[jax.ShapeDtypeStruct((n_tok, D_MODEL), F32), jax.ShapeDtypeStruct((n_tok, D_MODEL), BF16)],
        scratch_shapes=scratch,
        compiler_params=pltpu.CompilerParams(dimension_semantics=("arbitrary",),
                                             vmem_limit_bytes=VMEM_LIMIT),
        name="mixout",
    )(x2d, oaf, oab, obf, obb, ga, zb, mod3, *consts)


def _ffn_kernel(h_ref, x1_ref, mod_ref, wg_ref, wu_ref, wd_ref, nf_ref, y_ref):
    h = h_ref[...]
    gate = jnp.dot(h, wg_ref[...], preferred_element_type=F32)
    up = jnp.dot(h, wu_ref[...], preferred_element_type=F32)
    ff = _dot(_silu(gate) * up, wd_ref[...])
    x2 = x1_ref[...] + mod_ref[0][5:6] * ff
    y_ref[...] = x2 * lax.rsqrt(jnp.mean(x2 * x2, axis=-1, keepdims=True) + EPS) * nf_ref[...]


def _ffn(h2, x1, mod3, mod_row_of_tile, p, tm):
    n_tok = h2.shape[0]
    tok = pl.BlockSpec((tm, D_MODEL), lambda i: (i, 0))
    consts = [p["w_gate"], p["w_up"], p["w_down"], p["norm_f"]]
    return pl.pallas_call(
        _ffn_kernel,
        grid=(n_tok // tm,),
        in_specs=[tok, tok, pl.BlockSpec((1, 6, D_MODEL), lambda i: (mod_row_of_tile(i), 0, 0))]
                 + [_resident(a.shape) for a in consts],
        out_specs=tok,
        out_shape=jax.ShapeDtypeStruct((n_tok, D_MODEL), F32),
        compiler_params=pltpu.CompilerParams(dimension_semantics=("arbitrary",),
                                             vmem_limit_bytes=VMEM_LIMIT),
        name="ffn",
    )(h2, x1, mod3, *consts)


def _stream(x, mod3, mod_row_of_tile, s0_a, s0_b, p, latent):
    batch, seq, _ = x.shape
    x2d = x.reshape(batch * seq, D_MODEL)
    tm = 512
    mod_row = functools.partial(mod_row_of_tile, tm=tm)
    qa, f_fwd, f_bwd, va, ga, qkv, zb, gates = _inproj(x2d, mod3, mod_row, p, tm, latent, seq)

    hg = _hgrn_scan(qa, f_fwd, f_bwd, va, s0_a, batch, group=4, emit_state=not latent)
    if latent:
        assert seq // GRID_W == CHUNK
        gd = _gdn_scan(qkv, gates, s0_b, block_rows=CHUNK, columns=GRID_W, group=4, emit_state=False)
    else:
        gd = _gdn_scan(qkv.reshape(batch, seq, 3 * W_GROUP), gates.reshape(batch, seq, GATE_LANES),
                       s0_b, block_rows=seq, columns=1, group=2, emit_state=True)
    obf, obb = gd[:2] if latent else (o.reshape(batch * seq, W_GROUP) for o in gd[:2])

    x1, h2 = _mixout(x2d, hg[0], hg[1], obf, obb, ga, zb, mod3, mod_row, p, tm, latent, seq)
    y = _ffn(h2, x1, mod3, mod_row, p, tm)
    states = (None, None) if latent else (hg[2], gd[2])
    return y.reshape(batch, seq, D_MODEL), states


def kernel(x_prompt, x_sample, c, state_hgrn, state_gdn, c_ctx, w_ada, b_ada, norm1, norm2, w_in, conv_w,
           hgrn_lb, gdn_A_log, gdn_dt_bias, hgrn_out_norm, gdn_out_norm, w_out, w_gate, w_up, w_down, norm_f):
    depth = w_in.shape[0]
    assert depth == 1 and hgrn_lb.shape[0] == 2
    dec_batch, dec_seq, _ = x_sample.shape
    l = 0

    n_main = N_MAIN_GROUPS * W_GROUP
    pad8 = jnp.zeros((1, 2 * N_HEADS), F32)
    gparams = jnp.concatenate(
        [jnp.concatenate([pad8, a.reshape(1, 2 * N_HEADS).astype(F32),
                          jnp.zeros((1, GATE_LANES - 4 * N_HEADS), F32)], axis=1)
         for a in (gdn_A_log[l], gdn_dt_bias[l])], axis=0)
    p = {
        "norm1": norm1[l].reshape(1, D_MODEL), "norm2": norm2[l].reshape(1, D_MODEL),
        "lbp": hgrn_lb.reshape(2, 2 * W_GROUP), "gparams": gparams,
        "w_main": w_in[l][:, :n_main].astype(BF16),
        "w_gates": jnp.pad(w_in[l][:, n_main:], ((0, 0), (0, GATE_LANES - 4 * N_HEADS))).astype(BF16),
        "conv_w": conv_w[l],
        "norm_a": hgrn_out_norm[l].reshape(1, W_GROUP), "norm_b": gdn_out_norm[l].reshape(1, W_GROUP),
        "w_out": w_out[l].astype(BF16), "w_gate": w_gate[l].astype(BF16), "w_up": w_up[l].astype(BF16),
        "w_down": w_down[l].astype(BF16), "norm_f": norm_f.reshape(1, D_MODEL),
    }

    n_mod_rows = 8
    cvec = jnp.concatenate([c_ctx[None], c, jnp.zeros((n_mod_rows - 1 - dec_batch, D_MODEL), F32)], axis=0)
    mod3 = _modulation(cvec, w_ada[l], b_ada[l]).reshape(n_mod_rows, 6, D_MODEL)

    y_prompt, (new_a, new_b) = _stream(x_prompt, mod3, lambda i, tm: 0, None, None, p, latent=False)
    y_sample, _ = _stream(x_sample, mod3, lambda i, tm: 1 + i // (dec_seq // tm), state_hgrn[:, l],
                          state_gdn[:, l], p, latent=True)
    return y_prompt, y_sample, new_a[:, None], new_b[:, None]
```

```python
import functools

import jax
import jax.numpy as jnp
from jax import lax
from jax.experimental import pallas as pl
from jax.experimental.pallas import tpu as pltpu

F32 = jnp.float32
BF16 = jnp.bfloat16

D_MODEL = 1024
N_HEADS = 4
D_HEAD = 128
W_GROUP = N_HEADS * D_HEAD
CHUNK = 64
GRID_W = 64
CONV_W = 3
EPS = 1e-6
N_MAIN_GROUPS = 9
GATE_LANES = 128
VMEM_LIMIT = 56 * 1024 * 1024


def _sigmoid(x):
    return 1.0 / (1.0 + jnp.exp(-x))


def _silu(x):
    return x * _sigmoid(x)


def _dot(a, b):
    return jnp.dot(a.astype(BF16), b.astype(BF16), preferred_element_type=F32)


def _dot_nt(a, b):
    return lax.dot_general(a.astype(BF16), b.astype(BF16), (((1,), (1,)), ((), ())),
                           preferred_element_type=F32)


def _split3(x):
    x1 = x.astype(BF16)
    r = x - x1.astype(F32)
    x2 = r.astype(BF16)
    x3 = (r - x2.astype(F32)).astype(BF16)
    return x1, x2, x3


def _cumsum_rows(tri3, x):
    return jnp.dot(tri3, jnp.concatenate(_split3(x), axis=0), preferred_element_type=F32)


def _tri3(direction):
    tri = _tri_masks(direction)[0].astype(BF16)
    return jnp.concatenate([tri, tri, tri], axis=1)


def _tri_masks(direction):
    r = lax.broadcasted_iota(jnp.int32, (CHUNK, CHUNK), 0)
    c = lax.broadcasted_iota(jnp.int32, (CHUNK, CHUNK), 1)
    if direction == 0:
        return c <= r, c < r
    return c >= r, c > r


def _resident(shape):
    return pl.BlockSpec(shape, lambda i: (0,) * len(shape), pipeline_mode=pl.Buffered(1))


def _mod_kernel(c_ref, w_ref, b_ref, o_ref):
    s = _silu(c_ref[...])
    o_ref[...] = _dot(s, w_ref[...]) + b_ref[...]


def _modulation(cvec, w_ada, b_ada):
    n_rows, d = cvec.shape
    n_out = w_ada.shape[1]
    tn = 1536
    return pl.pallas_call(
        _mod_kernel,
        grid=(n_out // tn,),
        in_specs=[pl.BlockSpec((n_rows, d), lambda j: (0, 0)),
                  pl.BlockSpec((d, tn), lambda j: (0, j)),
                  pl.BlockSpec((1, tn), lambda j: (0, j))],
        out_specs=pl.BlockSpec((n_rows, tn), lambda j: (0, j)),
        out_shape=jax.ShapeDtypeStruct((n_rows, n_out), F32),
        compiler_params=pltpu.CompilerParams(dimension_semantics=("arbitrary",),
                                             vmem_limit_bytes=VMEM_LIMIT),
        name="mod",
    )(cvec, w_ada, b_ada.reshape(1, n_out))


HALO = GRID_W
LANE = 128
QKV_TILES = 3 * W_GROUP // LANE
ROW_PITCH = GRID_W + 8


def _inproj_kernel(*refs, latent, seq_len):
    it = iter(refs)
    x_ref = next(it)
    xp_ref, xn_ref = (next(it), next(it)) if latent else (None, None)
    mod_ref, n1_ref, lb_ref, gp_ref, cw_ref, w_ref, wg_ref = (next(it) for _ in range(7))
    qa_ref, ff_ref, fb_ref, va_ref, ga_ref, qkv_ref, zb_ref, gates_ref = (next(it) for _ in range(8))
    qkv_scr, gates_scr = (next(it), next(it)) if latent else (None, None)
    tm = x_ref.shape[0]
    m = mod_ref[0]

    def normed(x):
        y = x * lax.rsqrt(jnp.mean(x * x, axis=-1, keepdims=True) + EPS) * n1_ref[...]
        return (y * (1.0 + m[1:2]) + m[0:1]).astype(BF16)

    hb = normed(x_ref[...])

    w_qkv = w_ref[:, 5 * W_GROUP:8 * W_GROUP]
    if latent:
        i = pl.program_id(0)
        tiles_per_seq = seq_len // tm
        ext = jnp.dot(jnp.concatenate([normed(xp_ref[...]), hb, normed(xn_ref[...])], axis=0), w_qkv,
                      preferred_element_type=F32)
        above = jnp.where(i % tiles_per_seq == 0, 0.0, ext[:HALO])
        below = jnp.where(i % tiles_per_seq == tiles_per_seq - 1, 0.0, ext[HALO + tm:])
        cur = ext[HALO:HALO + tm]
        prev = jnp.concatenate([above, ext[HALO:tm]], axis=0)
        nxt = jnp.concatenate([ext[2 * HALO:HALO + tm], below], axis=0)
    else:
        cur = jnp.dot(hb, w_qkv, preferred_element_type=F32)
        pos = lax.broadcasted_iota(jnp.int32, cur.shape, 0) % seq_len
        prev = jnp.where(pos == 0, 0.0, pltpu.roll(cur, 1, axis=0))
        nxt = jnp.where(pos == seq_len - 1, 0.0, pltpu.roll(cur, tm - 1, axis=0))
    cw = cw_ref[...]
    y = _silu(prev * cw[0:1] + cur * cw[1:2] + nxt * cw[2:3])
    tiles = []
    for j in range(QKV_TILES):
        t = y[:, j * LANE:(j + 1) * LANE]
        if j < 2 * N_HEADS:
            t = t * lax.rsqrt(jnp.sum(t * t, axis=-1, keepdims=True) + EPS)
        if j < N_HEADS:
            t = t * (D_HEAD ** -0.5)
        tiles.append(t)

    raw = jnp.dot(hb, wg_ref[...], preferred_element_type=F32)
    gp = gp_ref[...]
    z = raw + gp[1:2]
    softplus = jnp.maximum(z, 0.0) + jnp.log(1.0 + jnp.exp(-jnp.abs(z)))
    lane = lax.broadcasted_iota(jnp.int32, raw.shape, 1)
    gates = jnp.where(lane < 2 * N_HEADS, _sigmoid(raw),
                      jnp.where(lane < 4 * N_HEADS, -jnp.exp(gp[0:1]) * softplus, 0.0))

    if not latent:
        for j, t in enumerate(tiles):
            qkv_ref[:, j * LANE:(j + 1) * LANE] = t
        gates_ref[...] = gates
    else:
        n_rows = tm // GRID_W
        for r in range(n_rows):
            rows, dst = slice(r * GRID_W, (r + 1) * GRID_W), slice(r * ROW_PITCH, r * ROW_PITCH + GRID_W)
            for j, t in enumerate(tiles):
                qkv_scr[j, dst, :] = t[rows]
            gates_scr[dst, :] = gates[rows]
        for c in range(GRID_W):
            for j in range(QKV_TILES):
                qkv_ref[0, :, (c * QKV_TILES + j) * LANE:(c * QKV_TILES + j + 1) * LANE] = (
                    qkv_scr[j, pl.ds(c, n_rows, stride=ROW_PITCH), :])
            gates_ref[0, :, c * GATE_LANES:(c + 1) * GATE_LANES] = gates_scr[pl.ds(c, n_rows, stride=ROW_PITCH), :]

    def proj(j):
        return jnp.dot(hb, w_ref[:, j * W_GROUP:(j + 1) * W_GROUP], preferred_element_type=F32)

    lbp = lb_ref[...]
    e = jnp.exp(lbp - jnp.max(lbp, axis=0, keepdims=True))
    lb = e[0:1] / jnp.sum(e, axis=0, keepdims=True)
    lb_f, lb_b = lb[:, :W_GROUP], lb[:, W_GROUP:]

    qa_ref[...] = _silu(proj(0)).astype(BF16)
    ff_ref[...] = lb_f + (1.0 - lb_f) * _sigmoid(proj(1))
    fb_ref[...] = lb_b + (1.0 - lb_b) * _sigmoid(proj(2))
    va_ref[...] = proj(3).astype(BF16)
    ga_ref[...] = _silu(proj(4)).astype(BF16)
    zb_ref[...] = _silu(proj(8)).astype(BF16)


def _inproj(x2d, mod3, mod_row_of_tile, p, tm, latent, seq_len):
    n_tok = x2d.shape[0]
    tok = lambda width: pl.BlockSpec((tm, width), lambda i: (i, 0))
    widths = [W_GROUP] * 5 + [3 * W_GROUP, W_GROUP, GATE_LANES]
    out_specs = [tok(w) for w in widths]
    dtypes = [BF16, F32, F32, BF16, BF16, F32, BF16, F32]
    out_shape = [jax.ShapeDtypeStruct((n_tok, w), dt) for w, dt in zip(widths, dtypes)]
    in_specs, args, scratch = [tok(D_MODEL)], [x2d], []
    if latent:
        n_halo_blocks, per_tile, rows = n_tok // HALO, tm // HALO, tm // GRID_W
        tiles_per_seq = seq_len // tm
        in_specs += [pl.BlockSpec((HALO, D_MODEL), lambda i: (jnp.maximum(i * per_tile - 1, 0), 0)),
                     pl.BlockSpec((HALO, D_MODEL), lambda i: (jnp.minimum((i + 1) * per_tile, n_halo_blocks - 1), 0))]
        args += [x2d, x2d]
        col = lambda width: pl.BlockSpec((1, rows, GRID_W * width),
                                         lambda i: (i // tiles_per_seq, i % tiles_per_seq, 0))
        col_shape = lambda width: jax.ShapeDtypeStruct((n_tok // seq_len, seq_len // GRID_W, GRID_W * width), F32)
        for k, width in ((5, 3 * W_GROUP), (7, GATE_LANES)):
            out_specs[k], out_shape[k] = col(width), col_shape(width)
        scratch = [pltpu.VMEM((QKV_TILES, rows * ROW_PITCH, LANE), F32), pltpu.VMEM((rows * ROW_PITCH, GATE_LANES), F32)]
    consts = [p["norm1"], p["lbp"], p["gparams"], p["conv_w"], p["w_main"], p["w_gates"]]
    in_specs += [pl.BlockSpec((1, 6, D_MODEL), lambda i: (mod_row_of_tile(i), 0, 0))] + [_resident(a.shape) for a in consts]
    return pl.pallas_call(
        functools.partial(_inproj_kernel, latent=latent, seq_len=seq_len),
        grid=(n_tok // tm,),
        in_specs=in_specs, out_specs=out_specs, out_shape=out_shape, scratch_shapes=scratch,
        compiler_params=pltpu.CompilerParams(dimension_semantics=("arbitrary",),
                                             vmem_limit_bytes=VMEM_LIMIT),
        name="inproj",
    )(*args, mod3, *consts)


def _hgrn_kernel(*refs, n_chunks, group, has_s0, emit_state):
    it = iter(refs)
    qf_ref, ff_ref, vf_ref, qb_ref, fb_ref, vb_ref = (next(it) for _ in range(6))
    s0_ref = next(it) if has_s0 else None
    of_ref, ob_ref = next(it), next(it)
    st_ref = next(it) if emit_state else None
    s_ref = next(it)
    n = pl.program_id(1)

    @pl.when(n == 0)
    def _():
        if has_s0:
            s_ref[...] = s0_ref[...]
        else:
            s_ref[...] = jnp.zeros_like(s_ref)

    chains = []
    for b in range(group):
        for d, (q_ref, f_ref, v_ref, o_ref) in enumerate(((qf_ref, ff_ref, vf_ref, of_ref),
                                                          (qb_ref, fb_ref, vb_ref, ob_ref))):
            incl, _ = _tri_masks(d)
            g_all = jnp.log(f_ref[b])
            cum_all = _cumsum_rows(_tri3(d), g_all)
            for h in range(N_HEADS):
                sl = slice(h * D_HEAD, (h + 1) * D_HEAD)
                k = 1.0 - f_ref[b, :, sl]
                g, G = g_all[:, sl], cum_all[:, sl]
                g_last_row = G[CHUNK - 1:CHUNK] if d == 0 else G[0:1]
                chains.append(dict(
                    b=b, d=d, h=h, sl=sl, o_ref=o_ref, incl=incl, vb=v_ref[b, :, sl].astype(BF16),
                    decay=jnp.exp(jnp.sum(g.T, axis=1, keepdims=True)),
                    q_dec=(q_ref[b, :, sl].astype(F32) * jnp.exp(G)).astype(BF16), k_dec=k * jnp.exp(-G),
                    k_tail_t=(k * jnp.exp(g_last_row - G)).T.astype(BF16)))
    attns = [jnp.where(ch["incl"], _dot_nt(ch["q_dec"], ch["k_dec"]), 0.0).astype(BF16) for ch in chains]
    states = [s_ref[ch["b"], ch["d"], ch["h"]] for ch in chains]
    outs = [_dot(jnp.concatenate([ch["q_dec"], attn], axis=1), jnp.concatenate([s.astype(BF16), ch["vb"]], axis=0))
            for ch, attn, s in zip(chains, attns, states)]
    upds = [_dot(ch["k_tail_t"], ch["vb"]) for ch in chains]
    for ch, o, s, u in zip(chains, outs, states, upds):
        ch["o_ref"][ch["b"], :, ch["sl"]] = o.astype(BF16)
        s_ref[ch["b"], ch["d"], ch["h"]] = ch["decay"] * s + u

    if emit_state:
        @pl.when(n == n_chunks - 1)
        def _():
            st_ref[...] = s_ref[...]


def _hgrn_scan(qa, f_fwd, f_bwd, va, s0, batch, group, emit_state):
    n_chunks = qa.shape[0] // batch // CHUNK
    to3 = lambda a: a.reshape(batch, n_chunks * CHUNK, W_GROUP)
    fwd = pl.BlockSpec((group, CHUNK, W_GROUP), lambda b, n: (b, n, 0))
    bwd = pl.BlockSpec((group, CHUNK, W_GROUP), lambda b, n: (b, n_chunks - 1 - n, 0))
    state_shape = (group, 2, N_HEADS, D_HEAD, D_HEAD)
    state_spec = pl.BlockSpec(state_shape, lambda b, n: (b, 0, 0, 0, 0))
    in_specs = [fwd, fwd, fwd, bwd, bwd, bwd]
    args = [to3(qa), to3(f_fwd), to3(va), to3(qa), to3(f_bwd), to3(va)]
    if s0 is not None:
        in_specs.append(state_spec)
        args.append(s0)
    out_specs = [fwd, bwd]
    out_shape = [jax.ShapeDtypeStruct((batch, n_chunks * CHUNK, W_GROUP), BF16)] * 2
    if emit_state:
        out_specs.append(state_spec)
        out_shape.append(jax.ShapeDtypeStruct((batch,) + state_shape[1:], F32))
    res = pl.pallas_call(
        functools.partial(_hgrn_kernel, n_chunks=n_chunks, group=group, has_s0=s0 is not None,
                          emit_state=emit_state),
        grid=(batch // group, n_chunks),
        in_specs=in_specs, out_specs=out_specs, out_shape=out_shape,
        scratch_shapes=[pltpu.VMEM(state_shape, F32)],
        compiler_params=pltpu.CompilerParams(dimension_semantics=("arbitrary", "arbitrary"),
                                             vmem_limit_bytes=VMEM_LIMIT),
        name="hgrn",
    )(*args)
    return [r.reshape(-1, W_GROUP) for r in res[:2]] + list(res[2:])


INVERSE_BASE_BLOCK = 8


def _same_block(block):
    r = lax.broadcasted_iota(jnp.int32, (CHUNK, CHUNK), 0) // block
    c = lax.broadcasted_iota(jnp.int32, (CHUNK, CHUNK), 1) // block
    return r == c


def _unit_triangular_inverses(lowers):
    r = lax.broadcasted_iota(jnp.int32, (CHUNK, CHUNK), 0)
    c = lax.broadcasted_iota(jnp.int32, (CHUNK, CHUNK), 1)
    eye = jnp.where(r == c, 1.0, 0.0)
    diag = _same_block(INVERSE_BASE_BLOCK)
    ds = [jnp.where(diag, lo, 0.0) for lo in lowers]
    ts = [eye - d for d in ds]
    ps = [_dot(d, d) for d in ds]
    power = 4
    while power < INVERSE_BASE_BLOCK:
        tps = [_dot(jnp.concatenate([t.astype(BF16), p.astype(BF16)], axis=0), p) for t, p in zip(ts, ps)]
        ts = [t + tp[:CHUNK] for t, tp in zip(ts, tps)]
        ps = [tp[CHUNK:] for tp in tps]
        power *= 2
    ts = [t + _dot(t, p) for t, p in zip(ts, ps)]
    block = INVERSE_BASE_BLOCK
    while block < CHUNK:
        off_mask = _same_block(2 * block) & jnp.logical_not(_same_block(block))
        ws = [_dot(t, jnp.where(off_mask, lo, 0.0)) for t, lo in zip(ts, lowers)]
        ts = [t - _dot(w, t) for t, w in zip(ts, ws)]
        block *= 2
    return ts


def _gdn_local(chains):
    for ch in chains:
        incl, strict = _tri_masks(ch["d"])
        diff = ch["g_cum"] - ch["g_cum_row"]
        ch["decay_mask"] = jnp.where(incl, jnp.exp(jnp.where(incl, diff, 0.0)), 0.0)
        ch["k_beta"] = ch["k"] * ch["beta"]
        ch["strict"] = strict
    kqs = [_dot_nt(jnp.concatenate([ch["k_beta"].astype(BF16), ch["q"].astype(BF16)], axis=0), ch["k"])
           for ch in chains]
    lowers = [jnp.where(ch["strict"], kq[:CHUNK] * ch["decay_mask"], 0.0) for ch, kq in zip(chains, kqs)]
    ts = _unit_triangular_inverses(lowers)
    for ch, kq, t in zip(chains, kqs, ts):
        e_g = jnp.exp(ch["g_cum"])
        ch["t"] = t
        ch["rhs"] = jnp.concatenate([ch["v"] * ch["beta"], ch["k_beta"] * e_g], axis=-1)
        ch["attn"] = (kq[CHUNK:] * ch["decay_mask"]).astype(BF16)
        g_last = ch["g_cum"][CHUNK - 1:CHUNK] if ch["d"] == 0 else ch["g_cum"][0:1]
        ch["q_dec"] = (ch["q"] * e_g).astype(BF16)
        ch["k_tail_t"] = (ch["k"] * jnp.exp(g_last - ch["g_cum"])).T.astype(BF16)
        ch["decay"] = jnp.exp(g_last)
    xs = [_dot(ch["t"], ch["rhs"]) for ch in chains]
    for ch, x in zip(chains, xs):
        ch["u"], ch["w"] = x[:, :D_HEAD], x[:, D_HEAD:].astype(BF16)


def _gdn_recurrence(chains, states):
    sbs = [s.astype(BF16) for s in states]
    wss = [_dot(ch["w"], sb) for ch, sb in zip(chains, sbs)]
    vbs = [(ch["u"] - ws).astype(BF16) for ch, ws in zip(chains, wss)]
    outs = [_dot(jnp.concatenate([ch["q_dec"], ch["attn"]], axis=1), jnp.concatenate([sb, vb], axis=0))
            for ch, sb, vb in zip(chains, sbs, vbs)]
    upd = [_dot(ch["k_tail_t"], vb) for ch, vb in zip(chains, vbs)]
    return outs, [s * ch["decay"] + u for ch, s, u in zip(chains, states, upd)]


def _gdn_kernel(*refs, n_steps, chunks_per_block, group, has_s0, emit_state):
    it = iter(refs)
    xf_ref, gf_ref, xb_ref, gb_ref = (next(it) for _ in range(4))
    s0_ref = next(it) if has_s0 else None
    of_ref, ob_ref = next(it), next(it)
    st_ref = next(it) if emit_state else None
    s_ref = next(it)
    n = pl.program_id(1)

    @pl.when(n == 0)
    def _():
        if has_s0:
            s_ref[...] = s0_ref[...]
        else:
            s_ref[...] = jnp.zeros_like(s_ref)

    chains = [[] for _ in range(chunks_per_block)]
    for b in range(group):
        for d, (x_ref, g_ref) in enumerate(((xf_ref, gf_ref), (xb_ref, gb_ref))):
            x = x_ref[b]
            heads = [tuple(x[:, (part * N_HEADS + h) * D_HEAD:(part * N_HEADS + h + 1) * D_HEAD] for part in range(3))
                     for h in range(N_HEADS)]
            tri3 = _tri3(d)
            for pos in range(chunks_per_block):
                ci = pos if d == 0 else chunks_per_block - 1 - pos
                rows = slice(ci * CHUNK, (ci + 1) * CHUNK)
                gates = g_ref[b, rows, :]
                cum = _cumsum_rows(tri3, gates)
                cum_t = cum.T
                for h in range(N_HEADS):
                    jb, jg = d * N_HEADS + h, 2 * N_HEADS + d * N_HEADS + h
                    q, k, v = (a[rows] for a in heads[h])
                    chains[pos].append(dict(q=q, k=k, v=v, beta=gates[:, jb:jb + 1], g_cum=cum[:, jg:jg + 1],
                                            g_cum_row=cum_t[jg:jg + 1, :], b=b, d=d, h=h, rows=rows))
    _gdn_local([ch for pos in chains for ch in pos])

    states = [s_ref[ch["b"], ch["d"], ch["h"]] for ch in chains[0]]
    o_refs = (of_ref, ob_ref)
    for pos in range(chunks_per_block):
        outs, states = _gdn_recurrence(chains[pos], states)
        for ch, o in zip(chains[pos], outs):
            o_refs[ch["d"]][ch["b"], ch["rows"], ch["h"] * D_HEAD:(ch["h"] + 1) * D_HEAD] = o.astype(of_ref.dtype)
    for ch, s in zip(chains[0], states):
        s_ref[ch["b"], ch["d"], ch["h"]] = s

    if emit_state:
        @pl.when(n == n_steps - 1)
        def _():
            st_ref[...] = s_ref[...]


def _gdn_scan(qkv3, gates3, s0, block_rows, columns, group, emit_state, out_dtype):
    batch = qkv3.shape[0]
    n_row_blocks = qkv3.shape[1] // block_rows
    n_steps = n_row_blocks * columns
    assert n_row_blocks == 1 or columns == 1

    def fwd(b, n):
        return (b, n, 0) if columns == 1 else (b, 0, n)

    def bwd(b, n):
        return fwd(b, n_steps - 1 - n)

    spec = lambda width, imap: pl.BlockSpec((group, block_rows, width), imap)
    state_shape = (group, 2, N_HEADS, D_HEAD, D_HEAD)
    state_spec = pl.BlockSpec(state_shape, lambda b, n: (b, 0, 0, 0, 0))
    in_specs = [spec(3 * W_GROUP, fwd), spec(GATE_LANES, fwd), spec(3 * W_GROUP, bwd), spec(GATE_LANES, bwd)]
    args = [qkv3, gates3, qkv3, gates3]
    if s0 is not None:
        in_specs.append(state_spec)
        args.append(s0)
    o_shape = jax.ShapeDtypeStruct((batch, qkv3.shape[1], columns * W_GROUP), out_dtype)
    out_specs, out_shape = [spec(W_GROUP, fwd), spec(W_GROUP, bwd)], [o_shape, o_shape]
    if emit_state:
        out_specs.append(state_spec)
        out_shape.append(jax.ShapeDtypeStruct((batch,) + state_shape[1:], F32))
    return pl.pallas_call(
        functools.partial(_gdn_kernel, n_steps=n_steps, chunks_per_block=block_rows // CHUNK, group=group,
                          has_s0=s0 is not None, emit_state=emit_state),
        grid=(batch // group, n_steps),
        in_specs=in_specs, out_specs=out_specs, out_shape=out_shape,
        scratch_shapes=[pltpu.VMEM(state_shape, F32)],
        compiler_params=pltpu.CompilerParams(dimension_semantics=("arbitrary", "arbitrary"),
                                             vmem_limit_bytes=VMEM_LIMIT),
        name="gdn",
    )(*args)


FFN_SPLIT = 2


def _tail_kernel(*refs, latent):
    it = iter(refs)
    (x_ref, oaf_ref, oab_ref, obf_ref, obb_ref, ga_ref, zb_ref, mod_ref, na_ref, nb_ref, wo_ref, n2_ref,
     wg_ref, wu_ref, wd_ref, nf_ref) = (next(it) for _ in range(16))
    y_ref = next(it)
    ob_scr = next(it) if latent else None
    tm = x_ref.shape[0]

    def gated_norm(o, w_ref, gate_ref, h):
        sl = slice(h * D_HEAD, (h + 1) * D_HEAD)
        o = o * lax.rsqrt(jnp.mean(o * o, axis=-1, keepdims=True) + EPS)
        return (o * w_ref[:, sl] * gate_ref[:, sl].astype(F32)).astype(BF16)

    def both(f_ref, b_ref, h):
        sl = slice(h * D_HEAD, (h + 1) * D_HEAD)
        return f_ref[:, sl].astype(F32) + b_ref[:, sl].astype(F32)

    if latent:
        n_rows = tm // GRID_W
        for c in range(GRID_W):
            for h in range(N_HEADS):
                sl = slice(c * W_GROUP + h * D_HEAD, c * W_GROUP + (h + 1) * D_HEAD)
                ob_scr[h, pl.ds(c, n_rows, stride=ROW_PITCH), :] = obf_ref[0, :, sl] + obb_ref[0, :, sl]
        o_b = [jnp.concatenate([ob_scr[h, r * ROW_PITCH:r * ROW_PITCH + GRID_W, :] for r in range(n_rows)], axis=0)
               for h in range(N_HEADS)]
    else:
        o_b = [both(obf_ref, obb_ref, h) for h in range(N_HEADS)]
    o_a = [both(oaf_ref, oab_ref, h) for h in range(N_HEADS)]
    mixed = jnp.concatenate([gated_norm(o, na_ref, ga_ref, h) for h, o in enumerate(o_a)]
                            + [gated_norm(o, nb_ref, zb_ref, h) for h, o in enumerate(o_b)], axis=-1)
    m = mod_ref[0]
    x1 = x_ref[...] + m[2:3] * jnp.dot(mixed, wo_ref[...], preferred_element_type=F32)
    y = x1 * lax.rsqrt(jnp.mean(x1 * x1, axis=-1, keepdims=True) + EPS) * n2_ref[...]
    h2 = (y * (1.0 + m[4:5]) + m[3:4]).astype(BF16)

    d_ff = wg_ref.shape[1]
    piece = d_ff // FFN_SPLIT
    ff = None
    for s in range(FFN_SPLIT):
        cols = slice(s * piece, (s + 1) * piece)
        gate = jnp.dot(h2, wg_ref[:, cols], preferred_element_type=F32)
        up = jnp.dot(h2, wu_ref[:, cols], preferred_element_type=F32)
        part = jnp.dot((_silu(gate) * up).astype(BF16), wd_ref[cols, :], preferred_element_type=F32)
        ff = part if ff is None else ff + part
    x2 = x1 + m[5:6] * ff
    y_ref[...] = x2 * lax.rsqrt(jnp.mean(x2 * x2, axis=-1, keepdims=True) + EPS) * nf_ref[...]


def _tail(x2d, oaf, oab, obf, obb, ga, zb, mod3, mod_row_of_tile, p, tm, latent, seq_len):
    n_tok = x2d.shape[0]
    assert p["w_gate"].shape[1] % (FFN_SPLIT * LANE) == 0
    tok = lambda width: pl.BlockSpec((tm, width), lambda i: (i, 0))
    ob_spec, scratch = tok(W_GROUP), []
    if latent:
        tiles_per_seq = seq_len // tm
        ob_spec = pl.BlockSpec((1, tm // GRID_W, GRID_W * W_GROUP), lambda i: (i // tiles_per_seq, i % tiles_per_seq, 0))
        scratch = [pltpu.VMEM((N_HEADS, tm // GRID_W * ROW_PITCH, D_HEAD), F32)]
    consts = [p["norm_a"], p["norm_b"], p["w_out"], p["norm2"], p["w_gate"], p["w_up"], p["w_down"], p["norm_f"]]
    return pl.pallas_call(
        functools.partial(_tail_kernel, latent=latent),
        grid=(n_tok // tm,),
        in_specs=[tok(D_MODEL), tok(W_GROUP), tok(W_GROUP), ob_spec, ob_spec, tok(W_GROUP), tok(W_GROUP),
                  pl.BlockSpec((1, 6, D_MODEL), lambda i: (mod_row_of_tile(i), 0, 0))]
                 + [_resident(a.shape) for a in consts],
        out_specs=tok(D_MODEL),
        out_shape=jax.ShapeDtypeStruct((n_tok, D_MODEL), F32),
        scratch_shapes=scratch,
        compiler_params=pltpu.CompilerParams(dimension_semantics=("arbitrary",),
                                             vmem_limit_bytes=VMEM_LIMIT),
        name="tail",
    )(x2d, oaf, oab, obf, obb, ga, zb, mod3, *consts)


def _stream(x, mod3, mod_row_of_tile, s0_a, s0_b, p, latent):
    batch, seq, _ = x.shape
    x2d = x.reshape(batch * seq, D_MODEL)
    tm = 512
    mod_row = functools.partial(mod_row_of_tile, tm=tm)
    qa, f_fwd, f_bwd, va, ga, qkv, zb, gates = _inproj(x2d, mod3, mod_row, p, tm, latent, seq)

    hg = _hgrn_scan(qa, f_fwd, f_bwd, va, s0_a, batch, group=4, emit_state=not latent)
    if latent:
        assert seq // GRID_W == CHUNK
        gd = _gdn_scan(qkv, gates, s0_b, block_rows=CHUNK, columns=GRID_W, group=4, emit_state=False,
                       out_dtype=F32)
    else:
        gd = _gdn_scan(qkv.reshape(batch, seq, 3 * W_GROUP), gates.reshape(batch, seq, GATE_LANES),
                       s0_b, block_rows=seq, columns=1, group=2, emit_state=True, out_dtype=BF16)
    obf, obb = gd[:2] if latent else (o.reshape(batch * seq, W_GROUP) for o in gd[:2])

    y = _tail(x2d, hg[0], hg[1], obf, obb, ga, zb, mod3, mod_row, p, tm, latent, seq)
    states = (None, None) if latent else (hg[2], gd[2])
    return y.reshape(batch, seq, D_MODEL), states


def kernel(x_prompt, x_sample, c, state_hgrn, state_gdn, c_ctx, w_ada, b_ada, norm1, norm2, w_in, conv_w,
           hgrn_lb, gdn_A_log, gdn_dt_bias, hgrn_out_norm, gdn_out_norm, w_out, w_gate, w_up, w_down, norm_f):
    depth = w_in.shape[0]
    assert depth == 1 and hgrn_lb.shape[0] == 2
    dec_batch, dec_seq, _ = x_sample.shape
    l = 0

    n_main = N_MAIN_GROUPS * W_GROUP
    pad8 = jnp.zeros((1, 2 * N_HEADS), F32)
    gparams = jnp.concatenate(
        [jnp.concatenate([pad8, a.reshape(1, 2 * N_HEADS).astype(F32),
                          jnp.zeros((1, GATE_LANES - 4 * N_HEADS), F32)], axis=1)
         for a in (gdn_A_log[l], gdn_dt_bias[l])], axis=0)
    p = {
        "norm1": norm1[l].reshape(1, D_MODEL), "norm2": norm2[l].reshape(1, D_MODEL),
        "lbp": hgrn_lb.reshape(2, 2 * W_GROUP), "gparams": gparams,
        "w_main": w_in[l][:, :n_main].astype(BF16),
        "w_gates": jnp.pad(w_in[l][:, n_main:], ((0, 0), (0, GATE_LANES - 4 * N_HEADS))).astype(BF16),
        "conv_w": conv_w[l],
        "norm_a": hgrn_out_norm[l].reshape(1, W_GROUP), "norm_b": gdn_out_norm[l].reshape(1, W_GROUP),
        "w_out": w_out[l].astype(BF16), "w_gate": w_gate[l].astype(BF16), "w_up": w_up[l].astype(BF16),
        "w_down": w_down[l].astype(BF16), "norm_f": norm_f.reshape(1, D_MODEL),
    }

    n_mod_rows = 8
    cvec = jnp.concatenate([c_ctx[None], c, jnp.zeros((n_mod_rows - 1 - dec_batch, D_MODEL), F32)], axis=0)
    mod3 = _modulation(cvec, w_ada[l], b_ada[l]).reshape(n_mod_rows, 6, D_MODEL)

    y_prompt, (new_a, new_b) = _stream(x_prompt, mod3, lambda i, tm: 0, None, None, p, latent=False)
    y_sample, _ = _stream(x_sample, mod3, lambda i, tm: 1 + i // (dec_seq // tm), state_hgrn[:, l],
                          state_gdn[:, l], p, latent=True)
    return y_prompt, y_sample, new_a[:, None], new_b[:, None]
```

```python
import functools

import jax
import jax.numpy as jnp
from jax import lax
from jax.experimental import pallas as pl
from jax.experimental.pallas import tpu as pltpu

F32 = jnp.float32
BF16 = jnp.bfloat16

D_MODEL = 1024
N_HEADS = 4
D_HEAD = 128
W_GROUP = N_HEADS * D_HEAD
CHUNK = 64
GRID_W = 64
CONV_W = 3
EPS = 1e-6
N_MAIN_GROUPS = 9
GATE_LANES = 128
VMEM_LIMIT = 56 * 1024 * 1024


def _sigmoid(x):
    return 1.0 / (1.0 + jnp.exp(-x))


def _silu(x):
    return x * _sigmoid(x)


def _dot(a, b):
    return jnp.dot(a.astype(BF16), b.astype(BF16), preferred_element_type=F32)


def _dot_nt(a, b):
    return lax.dot_general(a.astype(BF16), b.astype(BF16), (((1,), (1,)), ((), ())),
                           preferred_element_type=F32)


def _split3(x):
    x1 = x.astype(BF16)
    r = x - x1.astype(F32)
    x2 = r.astype(BF16)
    x3 = (r - x2.astype(F32)).astype(BF16)
    return x1, x2, x3


def _cumsum_rows(tri3, x):
    return jnp.dot(tri3, jnp.concatenate(_split3(x), axis=0), preferred_element_type=F32)


def _tri3(direction):
    tri = _tri_masks(direction)[0].astype(BF16)
    return jnp.concatenate([tri, tri, tri], axis=1)


def _tri_masks(direction):
    r = lax.broadcasted_iota(jnp.int32, (CHUNK, CHUNK), 0)
    c = lax.broadcasted_iota(jnp.int32, (CHUNK, CHUNK), 1)
    if direction == 0:
        return c <= r, c < r
    return c >= r, c > r


def _resident(shape):
    return pl.BlockSpec(shape, lambda i: (0,) * len(shape), pipeline_mode=pl.Buffered(1))


def _mod_kernel(c_ref, w_ref, b_ref, o_ref):
    s = _silu(c_ref[...])
    o_ref[...] = _dot(s, w_ref[...]) + b_ref[...]


def _modulation(cvec, w_ada, b_ada):
    n_rows, d = cvec.shape
    n_out = w_ada.shape[1]
    tn = 1536
    return pl.pallas_call(
        _mod_kernel,
        grid=(n_out // tn,),
        in_specs=[pl.BlockSpec((n_rows, d), lambda j: (0, 0)),
                  pl.BlockSpec((d, tn), lambda j: (0, j)),
                  pl.BlockSpec((1, tn), lambda j: (0, j))],
        out_specs=pl.BlockSpec((n_rows, tn), lambda j: (0, j)),
        out_shape=jax.ShapeDtypeStruct((n_rows, n_out), F32),
        compiler_params=pltpu.CompilerParams(dimension_semantics=("arbitrary",),
                                             vmem_limit_bytes=VMEM_LIMIT),
        name="mod",
    )(cvec, w_ada, b_ada.reshape(1, n_out))


HALO = GRID_W
LANE = 128
QKV_TILES = 3 * W_GROUP // LANE
ROW_PITCH = GRID_W + 8


def _inproj_kernel(*refs, latent, seq_len):
    it = iter(refs)
    x_ref = next(it)
    xp_ref, xn_ref = (next(it), next(it)) if latent else (None, None)
    mod_ref, n1_ref, lb_ref, gp_ref, cw_ref, w_ref, wg_ref = (next(it) for _ in range(7))
    qa_ref, ff_ref, fb_ref, va_ref, ga_ref, qkv_ref, zb_ref, gates_ref = (next(it) for _ in range(8))
    qkv_scr, gates_scr = (next(it), next(it)) if latent else (None, None)
    tm = x_ref.shape[0]
    m = mod_ref[0]

    def normed(x):
        y = x * lax.rsqrt(jnp.mean(x * x, axis=-1, keepdims=True) + EPS) * n1_ref[...]
        return (y * (1.0 + m[1:2]) + m[0:1]).astype(BF16)

    hb = normed(x_ref[...])

    w_qkv = w_ref[:, 5 * W_GROUP:8 * W_GROUP]
    if latent:
        i = pl.program_id(0)
        tiles_per_seq = seq_len // tm
        ext = jnp.dot(jnp.concatenate([normed(xp_ref[...]), hb, normed(xn_ref[...])], axis=0), w_qkv,
                      preferred_element_type=F32)
        above = jnp.where(i % tiles_per_seq == 0, 0.0, ext[:HALO])
        below = jnp.where(i % tiles_per_seq == tiles_per_seq - 1, 0.0, ext[HALO + tm:])
        cur = ext[HALO:HALO + tm]
        prev = jnp.concatenate([above, ext[HALO:tm]], axis=0)
        nxt = jnp.concatenate([ext[2 * HALO:HALO + tm], below], axis=0)
    else:
        cur = jnp.dot(hb, w_qkv, preferred_element_type=F32)
        pos = lax.broadcasted_iota(jnp.int32, cur.shape, 0) % seq_len
        prev = jnp.where(pos == 0, 0.0, pltpu.roll(cur, 1, axis=0))
        nxt = jnp.where(pos == seq_len - 1, 0.0, pltpu.roll(cur, tm - 1, axis=0))
    cw = cw_ref[...]
    y = _silu(prev * cw[0:1] + cur * cw[1:2] + nxt * cw[2:3])
    tiles = []
    for j in range(QKV_TILES):
        t = y[:, j * LANE:(j + 1) * LANE]
        if j < 2 * N_HEADS:
            t = t * lax.rsqrt(jnp.sum(t * t, axis=-1, keepdims=True) + EPS)
        if j < N_HEADS:
            t = t * (D_HEAD ** -0.5)
        tiles.append(t)

    raw = jnp.dot(hb, wg_ref[...], preferred_element_type=F32)
    gp = gp_ref[...]
    z = raw + gp[1:2]
    softplus = jnp.maximum(z, 0.0) + jnp.log(1.0 + jnp.exp(-jnp.abs(z)))
    lane = lax.broadcasted_iota(jnp.int32, raw.shape, 1)
    gates = jnp.where(lane < 2 * N_HEADS, _sigmoid(raw),
                      jnp.where(lane < 4 * N_HEADS, -jnp.exp(gp[0:1]) * softplus, 0.0))

    if not latent:
        for j, t in enumerate(tiles):
            qkv_ref[:, j * LANE:(j + 1) * LANE] = t
        gates_ref[...] = gates
    else:
        n_rows = tm // GRID_W
        for r in range(n_rows):
            rows, dst = slice(r * GRID_W, (r + 1) * GRID_W), slice(r * ROW_PITCH, r * ROW_PITCH + GRID_W)
            for j, t in enumerate(tiles):
                qkv_scr[j, dst, :] = t[rows]
            gates_scr[dst, :] = gates[rows]
        for c in range(GRID_W):
            for j in range(QKV_TILES):
                qkv_ref[0, :, (c * QKV_TILES + j) * LANE:(c * QKV_TILES + j + 1) * LANE] = (
                    qkv_scr[j, pl.ds(c, n_rows, stride=ROW_PITCH), :])
            gates_ref[0, :, c * GATE_LANES:(c + 1) * GATE_LANES] = gates_scr[pl.ds(c, n_rows, stride=ROW_PITCH), :]

    def proj(j):
        return jnp.dot(hb, w_ref[:, j * W_GROUP:(j + 1) * W_GROUP], preferred_element_type=F32)

    lbp = lb_ref[...]
    e = jnp.exp(lbp - jnp.max(lbp, axis=0, keepdims=True))
    lb = e[0:1] / jnp.sum(e, axis=0, keepdims=True)
    lb_f, lb_b = lb[:, :W_GROUP], lb[:, W_GROUP:]

    qa_ref[...] = _silu(proj(0)).astype(BF16)
    ff_ref[...] = lb_f + (1.0 - lb_f) * _sigmoid(proj(1))
    fb_ref[...] = lb_b + (1.0 - lb_b) * _sigmoid(proj(2))
    va_ref[...] = proj(3).astype(BF16)
    ga_ref[...] = _silu(proj(4)).astype(BF16)
    zb_ref[...] = _silu(proj(8)).astype(BF16)


def _inproj(x2d, mod3, mod_row_of_tile, p, tm, latent, seq_len):
    n_tok = x2d.shape[0]
    tok = lambda width: pl.BlockSpec((tm, width), lambda i: (i, 0))
    widths = [W_GROUP] * 5 + [3 * W_GROUP, W_GROUP, GATE_LANES]
    out_specs = [tok(w) for w in widths]
    dtypes = [BF16, F32, F32, BF16, BF16, F32, BF16, F32]
    out_shape = [jax.ShapeDtypeStruct((n_tok, w), dt) for w, dt in zip(widths, dtypes)]
    in_specs, args, scratch = [tok(D_MODEL)], [x2d], []
    if latent:
        n_halo_blocks, per_tile, rows = n_tok // HALO, tm // HALO, tm // GRID_W
        tiles_per_seq = seq_len // tm
        in_specs += [pl.BlockSpec((HALO, D_MODEL), lambda i: (jnp.maximum(i * per_tile - 1, 0), 0)),
                     pl.BlockSpec((HALO, D_MODEL), lambda i: (jnp.minimum((i + 1) * per_tile, n_halo_blocks - 1), 0))]
        args += [x2d, x2d]
        col = lambda width: pl.BlockSpec((1, rows, GRID_W * width),
                                         lambda i: (i // tiles_per_seq, i % tiles_per_seq, 0))
        col_shape = lambda width: jax.ShapeDtypeStruct((n_tok // seq_len, seq_len // GRID_W, GRID_W * width), F32)
        for k, width in ((5, 3 * W_GROUP), (7, GATE_LANES)):
            out_specs[k], out_shape[k] = col(width), col_shape(width)
        scratch = [pltpu.VMEM((QKV_TILES, rows * ROW_PITCH, LANE), F32), pltpu.VMEM((rows * ROW_PITCH, GATE_LANES), F32)]
    consts = [p["norm1"], p["lbp"], p["gparams"], p["conv_w"], p["w_main"], p["w_gates"]]
    in_specs += [pl.BlockSpec((1, 6, D_MODEL), lambda i: (mod_row_of_tile(i), 0, 0))] + [_resident(a.shape) for a in consts]
    return pl.pallas_call(
        functools.partial(_inproj_kernel, latent=latent, seq_len=seq_len),
        grid=(n_tok // tm,),
        in_specs=in_specs, out_specs=out_specs, out_shape=out_shape, scratch_shapes=scratch,
        compiler_params=pltpu.CompilerParams(dimension_semantics=("arbitrary",),
                                             vmem_limit_bytes=VMEM_LIMIT),
        name="inproj",
    )(*args, mod3, *consts)


INVERSE_BASE_BLOCK = 8


def _hgrn_stages(qf_ref, ff_ref, vf_ref, qb_ref, fb_ref, vb_ref, of_ref, ob_ref, s_ref, group):
    pre = []
    for b in range(group):
        for d, f_ref in enumerate((ff_ref, fb_ref)):
            g_all = jnp.log(f_ref[b])
            pre.append((b, d, g_all, _cumsum_rows(_tri3(d), g_all)))
    yield
    chains = []
    for b, d, g_all, cum_all in pre:
        q_ref, f_ref, v_ref, o_ref = (qf_ref, ff_ref, vf_ref, of_ref) if d == 0 else (qb_ref, fb_ref, vb_ref, ob_ref)
        incl, _ = _tri_masks(d)
        for h in range(N_HEADS):
            sl = slice(h * D_HEAD, (h + 1) * D_HEAD)
            k = 1.0 - f_ref[b, :, sl]
            g, G = g_all[:, sl], cum_all[:, sl]
            g_last_row = G[CHUNK - 1:CHUNK] if d == 0 else G[0:1]
            chains.append(dict(
                b=b, d=d, h=h, sl=sl, o_ref=o_ref, incl=incl, vb=v_ref[b, :, sl].astype(BF16),
                decay=jnp.exp(jnp.sum(g.T, axis=1, keepdims=True)),
                q_dec=(q_ref[b, :, sl].astype(F32) * jnp.exp(G)).astype(BF16), k_dec=k * jnp.exp(-G),
                k_tail_t=(k * jnp.exp(g_last_row - G)).T.astype(BF16)))
    attns = [jnp.where(ch["incl"], _dot_nt(ch["q_dec"], ch["k_dec"]), 0.0).astype(BF16) for ch in chains]
    yield
    states = [s_ref[ch["b"], ch["d"], ch["h"]] for ch in chains]
    outs = [_dot(jnp.concatenate([ch["q_dec"], attn], axis=1), jnp.concatenate([s.astype(BF16), ch["vb"]], axis=0))
            for ch, attn, s in zip(chains, attns, states)]
    yield
    upds = [_dot(ch["k_tail_t"], ch["vb"]) for ch in chains]
    for ch, o, s, u in zip(chains, outs, states, upds):
        ch["o_ref"][ch["b"], :, ch["sl"]] = o.astype(BF16)
        s_ref[ch["b"], ch["d"], ch["h"]] = ch["decay"] * s + u


def _same_block(block):
    r = lax.broadcasted_iota(jnp.int32, (CHUNK, CHUNK), 0) // block
    c = lax.broadcasted_iota(jnp.int32, (CHUNK, CHUNK), 1) // block
    return r == c


def _unit_triangular_inverses(lowers):
    r = lax.broadcasted_iota(jnp.int32, (CHUNK, CHUNK), 0)
    c = lax.broadcasted_iota(jnp.int32, (CHUNK, CHUNK), 1)
    eye = jnp.where(r == c, 1.0, 0.0)
    diag = _same_block(INVERSE_BASE_BLOCK)
    ds = [jnp.where(diag, lo, 0.0) for lo in lowers]
    ts = [eye - d for d in ds]
    ps = [_dot(d, d) for d in ds]
    yield
    power = 4
    while power < INVERSE_BASE_BLOCK:
        tps = [_dot(jnp.concatenate([t.astype(BF16), p.astype(BF16)], axis=0), p) for t, p in zip(ts, ps)]
        yield
        ts = [t + tp[:CHUNK] for t, tp in zip(ts, tps)]
        ps = [tp[CHUNK:] for tp in tps]
        power *= 2
    ts = [t + _dot(t, p) for t, p in zip(ts, ps)]
    yield
    block = INVERSE_BASE_BLOCK
    while block < CHUNK:
        off_mask = _same_block(2 * block) & jnp.logical_not(_same_block(block))
        ws = [_dot(t, jnp.where(off_mask, lo, 0.0)) for t, lo in zip(ts, lowers)]
        yield
        ts = [t - _dot(w, t) for t, w in zip(ts, ws)]
        yield
        block *= 2
    return ts


def _gdn_stages(xf_ref, gf_ref, xb_ref, gb_ref, of_ref, ob_ref, s_ref, group, chunks_per_block):
    pre = []
    for b in range(group):
        for d, g_ref in enumerate((gf_ref, gb_ref)):
            for pos in range(chunks_per_block):
                ci = pos if d == 0 else chunks_per_block - 1 - pos
                rows = slice(ci * CHUNK, (ci + 1) * CHUNK)
                gates = g_ref[b, rows, :]
                pre.append((b, d, pos, rows, gates, _cumsum_rows(_tri3(d), gates)))
    yield
    chains = [[] for _ in range(chunks_per_block)]
    for b, d, pos, rows, gates, cum in pre:
        x_ref = xf_ref if d == 0 else xb_ref
        incl, strict = _tri_masks(d)
        cum_t = cum.T
        for h in range(N_HEADS):
            jb, jg = d * N_HEADS + h, 2 * N_HEADS + d * N_HEADS + h
            q, k, v = (x_ref[b, rows, (part * N_HEADS + h) * D_HEAD:(part * N_HEADS + h + 1) * D_HEAD]
                       for part in range(3))
            beta, g_cum = gates[:, jb:jb + 1], cum[:, jg:jg + 1]
            diff = g_cum - cum_t[jg:jg + 1, :]
            chains[pos].append(dict(
                b=b, d=d, h=h, rows=rows, q=q, k=k, v=v, beta=beta, g_cum=g_cum, strict=strict, k_beta=k * beta,
                decay_mask=jnp.where(incl, jnp.exp(jnp.where(incl, diff, 0.0)), 0.0)))
    flat = [ch for pos in chains for ch in pos]

    kqs = [_dot_nt(jnp.concatenate([ch["k_beta"].astype(BF16), ch["q"].astype(BF16)], axis=0), ch["k"])
           for ch in flat]
    yield
    lowers = [jnp.where(ch["strict"], kq[:CHUNK] * ch["decay_mask"], 0.0) for ch, kq in zip(flat, kqs)]
    ts = yield from _unit_triangular_inverses(lowers)
    for ch, kq in zip(flat, kqs):
        e_g = jnp.exp(ch["g_cum"])
        ch["rhs"] = jnp.concatenate([ch["v"] * ch["beta"], ch["k_beta"] * e_g], axis=-1)
        ch["attn"] = (kq[CHUNK:] * ch["decay_mask"]).astype(BF16)
        g_last = ch["g_cum"][CHUNK - 1:CHUNK] if ch["d"] == 0 else ch["g_cum"][0:1]
        ch["q_dec"] = (ch["q"] * e_g).astype(BF16)
        ch["k_tail_t"] = (ch["k"] * jnp.exp(g_last - ch["g_cum"])).T.astype(BF16)
        ch["decay"] = jnp.exp(g_last)
    xs = [_dot(t, ch["rhs"]) for ch, t in zip(flat, ts)]
    yield
    for ch, x in zip(flat, xs):
        ch["u"], ch["w"] = x[:, :D_HEAD], x[:, D_HEAD:].astype(BF16)

    o_refs = (of_ref, ob_ref)
    states = [s_ref[ch["b"], ch["d"], ch["h"]] for ch in chains[0]]
    for pos in range(chunks_per_block):
        cur = chains[pos]
        sbs = [s.astype(BF16) for s in states]
        wss = [_dot(ch["w"], sb) for ch, sb in zip(cur, sbs)]
        yield
        vbs = [(ch["u"] - ws).astype(BF16) for ch, ws in zip(cur, wss)]
        outs = [_dot(jnp.concatenate([ch["q_dec"], ch["attn"]], axis=1), jnp.concatenate([sb, vb], axis=0))
                for ch, sb, vb in zip(cur, sbs, vbs)]
        yield
        upd = [_dot(ch["k_tail_t"], vb) for ch, vb in zip(cur, vbs)]
        yield
        states = [s * ch["decay"] + u for ch, s, u in zip(cur, states, upd)]
        for ch, o in zip(cur, outs):
            o_refs[ch["d"]][ch["b"], ch["rows"], ch["h"] * D_HEAD:(ch["h"] + 1) * D_HEAD] = o.astype(of_ref.dtype)
    for ch, s in zip(chains[0], states):
        s_ref[ch["b"], ch["d"], ch["h"]] = s


def _run_interleaved(primary, secondary, every):
    live = [primary, secondary]
    count = 0
    while live:
        gen = primary if primary in live and (secondary not in live or count < every) else secondary
        count = count + 1 if gen is primary else 0
        try:
            next(gen)
        except StopIteration:
            live.remove(gen)


def _scan_kernel(*refs, n_steps, chunks_per_block, group, hgrn, gdn, has_s0, emit_state):
    it = iter(refs)
    take = lambda k: [next(it) for _ in range(k)]
    h_in = take(6) if hgrn else None
    g_in = take(4) if gdn else None
    s0 = take(int(hgrn) + int(gdn)) if has_s0 else None
    h_out = take(2) if hgrn else None
    g_out = take(2) if gdn else None
    st = take(int(hgrn) + int(gdn)) if emit_state else None
    s_refs = take(int(hgrn) + int(gdn))
    n = pl.program_id(1)

    @pl.when(n == 0)
    def _():
        for k, s_ref in enumerate(s_refs):
            s_ref[...] = s0[k][...] if has_s0 else jnp.zeros_like(s_ref)

    gens = []
    if gdn:
        gens.append(_gdn_stages(*g_in, *g_out, s_refs[-1], group, chunks_per_block))
    if hgrn:
        gens.append(_hgrn_stages(*h_in, *h_out, s_refs[0], group))
    if len(gens) == 2:
        _run_interleaved(gens[0], gens[1], every=8)
    else:
        for _ in gens[0]:
            pass

    if emit_state:
        @pl.when(n == n_steps - 1)
        def _():
            for st_ref, s_ref in zip(st, s_refs):
                st_ref[...] = s_ref[...]


def _scan(hgrn_args, gdn_args, states0, batch, group, n_steps, gdn_block, emit_state, gdn_out_dtype):
    state_shape = (group, 2, N_HEADS, D_HEAD, D_HEAD)
    state_spec = pl.BlockSpec(state_shape, lambda b, n: (b, 0, 0, 0, 0))
    in_specs, args, out_specs, out_shape = [], [], [], []
    if hgrn_args is not None:
        qa, f_fwd, f_bwd, va = hgrn_args
        fwd = pl.BlockSpec((group, CHUNK, W_GROUP), lambda b, n: (b, n, 0))
        bwd = pl.BlockSpec((group, CHUNK, W_GROUP), lambda b, n: (b, n_steps - 1 - n, 0))
        in_specs += [fwd, fwd, fwd, bwd, bwd, bwd]
        args += [qa, f_fwd, va, qa, f_bwd, va]
        out_specs += [fwd, bwd]
        out_shape += [jax.ShapeDtypeStruct(qa.shape, BF16)] * 2
    chunks_per_block = 1
    if gdn_args is not None:
        qkv3, gates3 = gdn_args
        block_rows, columns = gdn_block
        chunks_per_block = block_rows // CHUNK
        fwd_map = (lambda b, n: (b, n, 0)) if columns == 1 else (lambda b, n: (b, 0, n))
        bwd_map = lambda b, n: fwd_map(b, n_steps - 1 - n)
        spec = lambda width, imap: pl.BlockSpec((group, block_rows, width), imap)
        in_specs += [spec(3 * W_GROUP, fwd_map), spec(GATE_LANES, fwd_map),
                     spec(3 * W_GROUP, bwd_map), spec(GATE_LANES, bwd_map)]
        args += [qkv3, gates3, qkv3, gates3]
        out_specs += [spec(W_GROUP, fwd_map), spec(W_GROUP, bwd_map)]
        out_shape += [jax.ShapeDtypeStruct((batch, qkv3.shape[1], columns * W_GROUP), gdn_out_dtype)] * 2
    n_mixers = int(hgrn_args is not None) + int(gdn_args is not None)
    if states0 is not None:
        in_specs, args = in_specs + [state_spec] * n_mixers, args + list(states0)
    if emit_state:
        out_specs = out_specs + [state_spec] * n_mixers
        out_shape = out_shape + [jax.ShapeDtypeStruct((batch,) + state_shape[1:], F32)] * n_mixers
    return pl.pallas_call(
        functools.partial(_scan_kernel, n_steps=n_steps, chunks_per_block=chunks_per_block, group=group,
                          hgrn=hgrn_args is not None, gdn=gdn_args is not None, has_s0=states0 is not None,
                          emit_state=emit_state),
        grid=(batch // group, n_steps),
        in_specs=in_specs, out_specs=out_specs, out_shape=out_shape,
        scratch_shapes=[pltpu.VMEM(state_shape, F32)] * n_mixers,
        compiler_params=pltpu.CompilerParams(dimension_semantics=("arbitrary", "arbitrary"),
                                             vmem_limit_bytes=VMEM_LIMIT),
        name="scan",
    )(*args)


FFN_SPLIT = 2
MXU_K_TILE = 256


def _ffn_pieces(d_ff):
    k_tiles = -(-d_ff // MXU_K_TILE)
    bounds = [min(d_ff, MXU_K_TILE * -(-k_tiles * s // FFN_SPLIT)) for s in range(FFN_SPLIT + 1)]
    return list(zip(bounds[:-1], bounds[1:]))


def _tail_kernel(*refs, latent):
    it = iter(refs)
    (x_ref, oaf_ref, oab_ref, obf_ref, obb_ref, ga_ref, zb_ref, mod_ref, na_ref, nb_ref, wo_ref, n2_ref,
     wg_ref, wu_ref, wd_ref, nf_ref) = (next(it) for _ in range(16))
    y_ref = next(it)
    ob_scr = next(it) if latent else None
    tm = x_ref.shape[0]

    def gated_norm(o, w_ref, gate_ref, h):
        sl = slice(h * D_HEAD, (h + 1) * D_HEAD)
        o = o * lax.rsqrt(jnp.mean(o * o, axis=-1, keepdims=True) + EPS)
        return (o * w_ref[:, sl] * gate_ref[:, sl].astype(F32)).astype(BF16)

    def both(f_ref, b_ref, h):
        sl = slice(h * D_HEAD, (h + 1) * D_HEAD)
        return f_ref[:, sl].astype(F32) + b_ref[:, sl].astype(F32)

    if latent:
        n_rows = tm // GRID_W
        for c in range(GRID_W):
            for h in range(N_HEADS):
                sl = slice(c * W_GROUP + h * D_HEAD, c * W_GROUP + (h + 1) * D_HEAD)
                ob_scr[h, pl.ds(c, n_rows, stride=ROW_PITCH), :] = obf_ref[0, :, sl] + obb_ref[0, :, sl]
        o_b = [jnp.concatenate([ob_scr[h, r * ROW_PITCH:r * ROW_PITCH + GRID_W, :] for r in range(n_rows)], axis=0)
               for h in range(N_HEADS)]
    else:
        o_b = [both(obf_ref, obb_ref, h) for h in range(N_HEADS)]
    o_a = [both(oaf_ref, oab_ref, h) for h in range(N_HEADS)]
    mixed = jnp.concatenate([gated_norm(o, na_ref, ga_ref, h) for h, o in enumerate(o_a)]
                            + [gated_norm(o, nb_ref, zb_ref, h) for h, o in enumerate(o_b)], axis=-1)
    m = mod_ref[0]
    x1 = x_ref[...] + m[2:3] * jnp.dot(mixed, wo_ref[...], preferred_element_type=F32)
    y = x1 * lax.rsqrt(jnp.mean(x1 * x1, axis=-1, keepdims=True) + EPS) * n2_ref[...]
    h2 = (y * (1.0 + m[4:5]) + m[3:4]).astype(BF16)

    ff = None
    for lo, hi in _ffn_pieces(wg_ref.shape[1]):
        cols = slice(lo, hi)
        gate = jnp.dot(h2, wg_ref[:, cols], preferred_element_type=F32)
        up = jnp.dot(h2, wu_ref[:, cols], preferred_element_type=F32)
        part = jnp.dot((_silu(gate) * up).astype(BF16), wd_ref[cols, :], preferred_element_type=F32)
        ff = part if ff is None else ff + part
    x2 = x1 + m[5:6] * ff
    y_ref[...] = x2 * lax.rsqrt(jnp.mean(x2 * x2, axis=-1, keepdims=True) + EPS) * nf_ref[...]


def _tail(x2d, oaf, oab, obf, obb, ga, zb, mod3, mod_row_of_tile, p, tm, latent, seq_len):
    n_tok = x2d.shape[0]
    assert p["w_gate"].shape[1] % LANE == 0
    tok = lambda width: pl.BlockSpec((tm, width), lambda i: (i, 0))
    ob_spec, scratch = tok(W_GROUP), []
    if latent:
        tiles_per_seq = seq_len // tm
        ob_spec = pl.BlockSpec((1, tm // GRID_W, GRID_W * W_GROUP), lambda i: (i // tiles_per_seq, i % tiles_per_seq, 0))
        scratch = [pltpu.VMEM((N_HEADS, tm // GRID_W * ROW_PITCH, D_HEAD), F32)]
    consts = [p["norm_a"], p["norm_b"], p["w_out"], p["norm2"], p["w_gate"], p["w_up"], p["w_down"], p["norm_f"]]
    return pl.pallas_call(
        functools.partial(_tail_kernel, latent=latent),
        grid=(n_tok // tm,),
        in_specs=[tok(D_MODEL), tok(W_GROUP), tok(W_GROUP), ob_spec, ob_spec, tok(W_GROUP), tok(W_GROUP),
                  pl.BlockSpec((1, 6, D_MODEL), lambda i: (mod_row_of_tile(i), 0, 0))]
                 + [_resident(a.shape) for a in consts],
        out_specs=tok(D_MODEL),
        out_shape=jax.ShapeDtypeStruct((n_tok, D_MODEL), F32),
        scratch_shapes=scratch,
        compiler_params=pltpu.CompilerParams(dimension_semantics=("arbitrary",),
                                             vmem_limit_bytes=VMEM_LIMIT),
        name="tail",
    )(x2d, oaf, oab, obf, obb, ga, zb, mod3, *consts)


def _stream(x, mod3, mod_row_of_tile, s0_a, s0_b, p, latent):
    batch, seq, _ = x.shape
    x2d = x.reshape(batch * seq, D_MODEL)
    tm = 512
    mod_row = functools.partial(mod_row_of_tile, tm=tm)
    qa, f_fwd, f_bwd, va, ga, qkv, zb, gates = _inproj(x2d, mod3, mod_row, p, tm, latent, seq)
    hgrn_args = [a.reshape(batch, seq, W_GROUP) for a in (qa, f_fwd, f_bwd, va)]
    n_chunks = seq // CHUNK
    if latent:
        assert seq // GRID_W == CHUNK and n_chunks == GRID_W
        oaf, oab, obf, obb = _scan(hgrn_args, (qkv, gates), (s0_a, s0_b), batch, group=4, n_steps=n_chunks,
                                   gdn_block=(CHUNK, GRID_W), emit_state=False, gdn_out_dtype=F32)
        states = (None, None)
    else:
        oaf, oab, new_a = _scan(hgrn_args, None, None, batch, group=4, n_steps=n_chunks, gdn_block=None,
                                emit_state=True, gdn_out_dtype=None)
        obf, obb, new_b = _scan(None, (qkv.reshape(batch, seq, 3 * W_GROUP), gates.reshape(batch, seq, GATE_LANES)),
                                None, batch, group=2, n_steps=1, gdn_block=(seq, 1), emit_state=True,
                                gdn_out_dtype=BF16)
        obf, obb = (o.reshape(batch * seq, W_GROUP) for o in (obf, obb))
        states = (new_a, new_b)
    oaf, oab = (o.reshape(batch * seq, W_GROUP) for o in (oaf, oab))
    y = _tail(x2d, oaf, oab, obf, obb, ga, zb, mod3, mod_row, p, tm, latent, seq)
    return y.reshape(batch, seq, D_MODEL), states


def kernel(x_prompt, x_sample, c, state_hgrn, state_gdn, c_ctx, w_ada, b_ada, norm1, norm2, w_in, conv_w,
           hgrn_lb, gdn_A_log, gdn_dt_bias, hgrn_out_norm, gdn_out_norm, w_out, w_gate, w_up, w_down, norm_f):
    depth = w_in.shape[0]
    assert depth == 1 and hgrn_lb.shape[0] == 2
    dec_batch, dec_seq, _ = x_sample.shape
    l = 0

    n_main = N_MAIN_GROUPS * W_GROUP
    pad8 = jnp.zeros((1, 2 * N_HEADS), F32)
    gparams = jnp.concatenate(
        [jnp.concatenate([pad8, a.reshape(1, 2 * N_HEADS).astype(F32),
                          jnp.zeros((1, GATE_LANES - 4 * N_HEADS), F32)], axis=1)
         for a in (gdn_A_log[l], gdn_dt_bias[l])], axis=0)
    p = {
        "norm1": norm1[l].reshape(1, D_MODEL), "norm2": norm2[l].reshape(1, D_MODEL),
        "lbp": hgrn_lb.reshape(2, 2 * W_GROUP), "gparams": gparams,
        "w_main": w_in[l].astype(BF16),
        "w_gates": jnp.pad(w_in[l][:, n_main:], ((0, 0), (0, GATE_LANES - 4 * N_HEADS))).astype(BF16),
        "conv_w": conv_w[l],
        "norm_a": hgrn_out_norm[l].reshape(1, W_GROUP), "norm_b": gdn_out_norm[l].reshape(1, W_GROUP),
        "w_out": w_out[l].astype(BF16), "w_gate": w_gate[l].astype(BF16), "w_up": w_up[l].astype(BF16),
        "w_down": w_down[l].astype(BF16), "norm_f": norm_f.reshape(1, D_MODEL),
    }

    n_mod_rows = 8
    cvec = jnp.concatenate([c_ctx[None], c, jnp.zeros((n_mod_rows - 1 - dec_batch, D_MODEL), F32)], axis=0)
    mod3 = _modulation(cvec, w_ada[l], b_ada[l]).reshape(n_mod_rows, 6, D_MODEL)

    y_prompt, (new_a, new_b) = _stream(x_prompt, mod3, lambda i, tm: 0, None, None, p, latent=False)
    y_sample, _ = _stream(x_sample, mod3, lambda i, tm: 1 + i // (dec_seq // tm), state_hgrn[:, l],
                          state_gdn[:, l], p, latent=True)
    return y_prompt, y_sample, new_a[:, None], new_b[:, None]
```

```python
import functools

import jax
import jax.numpy as jnp
from jax import lax
from jax.experimental import pallas as pl
from jax.experimental.pallas import tpu as pltpu

F32 = jnp.float32
BF16 = jnp.bfloat16

D_MODEL = 1024
N_HEADS = 4
D_HEAD = 128
W_GROUP = N_HEADS * D_HEAD
CHUNK = 64
GRID_W = 64
CONV_W = 3
EPS = 1e-6
N_MAIN_GROUPS = 9
GATE_LANES = 128
VMEM_LIMIT = 56 * 1024 * 1024


def _sigmoid(x):
    return 1.0 / (1.0 + jnp.exp(-x))


def _silu(x):
    return x * _sigmoid(x)


def _dot(a, b):
    return jnp.dot(a.astype(BF16), b.astype(BF16), preferred_element_type=F32)


def _dot_nt(a, b):
    return lax.dot_general(a.astype(BF16), b.astype(BF16), (((1,), (1,)), ((), ())),
                           preferred_element_type=F32)


def _split3(x):
    x1 = x.astype(BF16)
    r = x - x1.astype(F32)
    x2 = r.astype(BF16)
    x3 = (r - x2.astype(F32)).astype(BF16)
    return x1, x2, x3


def _cumsum_rows(tri3, x):
    return jnp.dot(tri3, jnp.concatenate(_split3(x), axis=0), preferred_element_type=F32)


def _tri3(direction):
    tri = _tri_masks(direction)[0].astype(BF16)
    return jnp.concatenate([tri, tri, tri], axis=1)


def _tri_masks(direction):
    r = lax.broadcasted_iota(jnp.int32, (CHUNK, CHUNK), 0)
    c = lax.broadcasted_iota(jnp.int32, (CHUNK, CHUNK), 1)
    if direction == 0:
        return c <= r, c < r
    return c >= r, c > r


def _resident(shape):
    return pl.BlockSpec(shape, lambda i: (0,) * len(shape), pipeline_mode=pl.Buffered(1))


def _mod_kernel(c_ref, w_ref, b_ref, o_ref):
    s = _silu(c_ref[...])
    o_ref[...] = _dot(s, w_ref[...]) + b_ref[...]


def _modulation(cvec, w_ada, b_ada):
    n_rows, d = cvec.shape
    n_out = w_ada.shape[1]
    tn = 1536
    return pl.pallas_call(
        _mod_kernel,
        grid=(n_out // tn,),
        in_specs=[pl.BlockSpec((n_rows, d), lambda j: (0, 0)),
                  pl.BlockSpec((d, tn), lambda j: (0, j)),
                  pl.BlockSpec((1, tn), lambda j: (0, j))],
        out_specs=pl.BlockSpec((n_rows, tn), lambda j: (0, j)),
        out_shape=jax.ShapeDtypeStruct((n_rows, n_out), F32),
        compiler_params=pltpu.CompilerParams(dimension_semantics=("arbitrary",),
                                             vmem_limit_bytes=VMEM_LIMIT),
        name="mod",
    )(cvec, w_ada, b_ada.reshape(1, n_out))


HALO = GRID_W
LANE = 128
QKV_TILES = 3 * W_GROUP // LANE
ROW_PITCH = GRID_W + 8


def _inproj_kernel(*refs, latent, seq_len):
    it = iter(refs)
    x_ref = next(it)
    xp_ref, xn_ref = (next(it), next(it)) if latent else (None, None)
    mod_ref, n1_ref, lb_ref, gp_ref, cw_ref, w_ref, wg_ref = (next(it) for _ in range(7))
    qa_ref, ff_ref, fb_ref, va_ref, ga_ref, qkv_ref, zb_ref, gates_ref = (next(it) for _ in range(8))
    qkv_scr, gates_scr = (next(it), next(it)) if latent else (None, None)
    tm = x_ref.shape[0]
    m = mod_ref[0]

    def normed(x):
        y = x * lax.rsqrt(jnp.mean(x * x, axis=-1, keepdims=True) + EPS) * n1_ref[...]
        return (y * (1.0 + m[1:2]) + m[0:1]).astype(BF16)

    hb = normed(x_ref[...])

    w_qkv = w_ref[:, 5 * W_GROUP:8 * W_GROUP]
    if latent:
        i = pl.program_id(0)
        tiles_per_seq = seq_len // tm
        ext = jnp.dot(jnp.concatenate([normed(xp_ref[...]), hb, normed(xn_ref[...])], axis=0), w_qkv,
                      preferred_element_type=F32)
        above = jnp.where(i % tiles_per_seq == 0, 0.0, ext[:HALO])
        below = jnp.where(i % tiles_per_seq == tiles_per_seq - 1, 0.0, ext[HALO + tm:])
        cur = ext[HALO:HALO + tm]
        prev = jnp.concatenate([above, ext[HALO:tm]], axis=0)
        nxt = jnp.concatenate([ext[2 * HALO:HALO + tm], below], axis=0)
    else:
        cur = jnp.dot(hb, w_qkv, preferred_element_type=F32)
        pos = lax.broadcasted_iota(jnp.int32, cur.shape, 0) % seq_len
        prev = jnp.where(pos == 0, 0.0, pltpu.roll(cur, 1, axis=0))
        nxt = jnp.where(pos == seq_len - 1, 0.0, pltpu.roll(cur, tm - 1, axis=0))
    cw = cw_ref[...]
    y = _silu(prev * cw[0:1] + cur * cw[1:2] + nxt * cw[2:3])
    tiles = []
    for j in range(QKV_TILES):
        t = y[:, j * LANE:(j + 1) * LANE]
        if j < 2 * N_HEADS:
            t = t * lax.rsqrt(jnp.sum(t * t, axis=-1, keepdims=True) + EPS)
        if j < N_HEADS:
            t = t * (D_HEAD ** -0.5)
        tiles.append(t)

    raw = jnp.dot(hb, wg_ref[...], preferred_element_type=F32)
    gp = gp_ref[...]
    z = raw + gp[1:2]
    softplus = jnp.maximum(z, 0.0) + jnp.log(1.0 + jnp.exp(-jnp.abs(z)))
    lane = lax.broadcasted_iota(jnp.int32, raw.shape, 1)
    gates = jnp.where(lane < 2 * N_HEADS, _sigmoid(raw),
                      jnp.where(lane < 4 * N_HEADS, -jnp.exp(gp[0:1]) * softplus, 0.0))

    if not latent:
        for j, t in enumerate(tiles):
            qkv_ref[:, j * LANE:(j + 1) * LANE] = t
        gates_ref[...] = gates
    else:
        n_rows = tm // GRID_W
        for r in range(n_rows):
            rows, dst = slice(r * GRID_W, (r + 1) * GRID_W), slice(r * ROW_PITCH, r * ROW_PITCH + GRID_W)
            for j, t in enumerate(tiles):
                qkv_scr[j, dst, :] = t[rows]
            gates_scr[dst, :] = gates[rows]
        for c in range(GRID_W):
            for j in range(QKV_TILES):
                qkv_ref[0, :, (c * QKV_TILES + j) * LANE:(c * QKV_TILES + j + 1) * LANE] = (
                    qkv_scr[j, pl.ds(c, n_rows, stride=ROW_PITCH), :])
            gates_ref[0, :, c * GATE_LANES:(c + 1) * GATE_LANES] = gates_scr[pl.ds(c, n_rows, stride=ROW_PITCH), :]

    def proj(j):
        return jnp.dot(hb, w_ref[:, j * W_GROUP:(j + 1) * W_GROUP], preferred_element_type=F32)

    lbp = lb_ref[...]
    e = jnp.exp(lbp - jnp.max(lbp, axis=0, keepdims=True))
    lb = e[0:1] / jnp.sum(e, axis=0, keepdims=True)
    lb_f, lb_b = lb[:, :W_GROUP], lb[:, W_GROUP:]

    qa_ref[...] = _silu(proj(0)).astype(BF16)
    ff_ref[...] = lb_f + (1.0 - lb_f) * _sigmoid(proj(1))
    fb_ref[...] = lb_b + (1.0 - lb_b) * _sigmoid(proj(2))
    va_ref[...] = proj(3).astype(BF16)
    ga_ref[...] = _silu(proj(4)).astype(BF16)
    zb_ref[...] = _silu(proj(8)).astype(BF16)


def _inproj(x2d, mod3, mod_row_of_tile, p, tm, latent, seq_len):
    n_tok = x2d.shape[0]
    tok = lambda width: pl.BlockSpec((tm, width), lambda i: (i, 0))
    widths = [W_GROUP] * 5 + [3 * W_GROUP, W_GROUP, GATE_LANES]
    out_specs = [tok(w) for w in widths]
    dtypes = [BF16, F32, F32, BF16, BF16, F32, BF16, F32]
    out_shape = [jax.ShapeDtypeStruct((n_tok, w), dt) for w, dt in zip(widths, dtypes)]
    in_specs, args, scratch = [tok(D_MODEL)], [x2d], []
    if latent:
        n_halo_blocks, per_tile, rows = n_tok // HALO, tm // HALO, tm // GRID_W
        tiles_per_seq = seq_len // tm
        in_specs += [pl.BlockSpec((HALO, D_MODEL), lambda i: (jnp.maximum(i * per_tile - 1, 0), 0)),
                     pl.BlockSpec((HALO, D_MODEL), lambda i: (jnp.minimum((i + 1) * per_tile, n_halo_blocks - 1), 0))]
        args += [x2d, x2d]
        col = lambda width: pl.BlockSpec((1, rows, GRID_W * width),
                                         lambda i: (i // tiles_per_seq, i % tiles_per_seq, 0))
        col_shape = lambda width: jax.ShapeDtypeStruct((n_tok // seq_len, seq_len // GRID_W, GRID_W * width), F32)
        for k, width in ((5, 3 * W_GROUP), (7, GATE_LANES)):
            out_specs[k], out_shape[k] = col(width), col_shape(width)
        scratch = [pltpu.VMEM((QKV_TILES, rows * ROW_PITCH, LANE), F32), pltpu.VMEM((rows * ROW_PITCH, GATE_LANES), F32)]
    consts = [p["norm1"], p["lbp"], p["gparams"], p["conv_w"], p["w_main"], p["w_gates"]]
    in_specs += [pl.BlockSpec((1, 6, D_MODEL), lambda i: (mod_row_of_tile(i), 0, 0))] + [_resident(a.shape) for a in consts]
    return pl.pallas_call(
        functools.partial(_inproj_kernel, latent=latent, seq_len=seq_len),
        grid=(n_tok // tm,),
        in_specs=in_specs, out_specs=out_specs, out_shape=out_shape, scratch_shapes=scratch,
        compiler_params=pltpu.CompilerParams(dimension_semantics=("arbitrary",),
                                             vmem_limit_bytes=VMEM_LIMIT),
        name="inproj",
    )(*args, mod3, *consts)


INVERSE_BASE_BLOCK = 8


def _hgrn_stages(qf_ref, ff_ref, vf_ref, qb_ref, fb_ref, vb_ref, of_ref, ob_ref, s_ref, group):
    pre = []
    for b in range(group):
        for d, f_ref in enumerate((ff_ref, fb_ref)):
            g_all = jnp.log(f_ref[b])
            pre.append((b, d, g_all, _cumsum_rows(_tri3(d), g_all)))
    yield
    chains = []
    for b, d, g_all, cum_all in pre:
        q_ref, f_ref, v_ref, o_ref = (qf_ref, ff_ref, vf_ref, of_ref) if d == 0 else (qb_ref, fb_ref, vb_ref, ob_ref)
        incl, _ = _tri_masks(d)
        for h in range(N_HEADS):
            sl = slice(h * D_HEAD, (h + 1) * D_HEAD)
            k = 1.0 - f_ref[b, :, sl]
            g, G = g_all[:, sl], cum_all[:, sl]
            g_last_row = G[CHUNK - 1:CHUNK] if d == 0 else G[0:1]
            chains.append(dict(
                b=b, d=d, h=h, sl=sl, o_ref=o_ref, incl=incl, vb=v_ref[b, :, sl].astype(BF16),
                decay=jnp.exp(jnp.sum(g.T, axis=1, keepdims=True)),
                q_dec=(q_ref[b, :, sl].astype(F32) * jnp.exp(G)).astype(BF16), k_dec=k * jnp.exp(-G),
                k_tail_t=(k * jnp.exp(g_last_row - G)).T.astype(BF16)))
    attns = [jnp.where(ch["incl"], _dot_nt(ch["q_dec"], ch["k_dec"]), 0.0).astype(BF16) for ch in chains]
    yield
    states = [s_ref[ch["b"], ch["d"], ch["h"]] for ch in chains]
    outs = [_dot(jnp.concatenate([ch["q_dec"], attn], axis=1), jnp.concatenate([s.astype(BF16), ch["vb"]], axis=0))
            for ch, attn, s in zip(chains, attns, states)]
    yield
    upds = [_dot(ch["k_tail_t"], ch["vb"]) for ch in chains]
    for ch, o, s, u in zip(chains, outs, states, upds):
        ch["o_ref"][ch["b"], :, ch["sl"]] = o.astype(BF16)
        s_ref[ch["b"], ch["d"], ch["h"]] = ch["decay"] * s + u


PAIR = 2


def _pair_index():
    r = lax.broadcasted_iota(jnp.int32, (CHUNK, PAIR * CHUNK), 0)
    lane = lax.broadcasted_iota(jnp.int32, (CHUNK, PAIR * CHUNK), 1)
    return r, lane % CHUNK, lane // CHUNK


def _block_diag(packed):
    head = lax.broadcasted_iota(jnp.int32, packed.shape, 1) // (packed.shape[1] // PAIR)
    return jnp.concatenate([jnp.where(head == h, packed, jnp.zeros_like(packed)) for h in range(PAIR)], axis=0)


def _unit_triangular_inverses(lowers):
    r, c, _ = _pair_index()
    same_block = lambda block: r // block == c // block
    pdot = lambda a, b: _dot(a, _block_diag(b.astype(BF16)))
    eye = jnp.where(r == c, 1.0, 0.0)
    diag = same_block(INVERSE_BASE_BLOCK)
    ds = [jnp.where(diag, lo, 0.0) for lo in lowers]
    ts = [eye - d for d in ds]
    ps = [pdot(d, d) for d in ds]
    yield
    power = 4
    while power < INVERSE_BASE_BLOCK:
        tps = [pdot(jnp.concatenate([t.astype(BF16), p.astype(BF16)], axis=0), p) for t, p in zip(ts, ps)]
        yield
        ts = [t + tp[:CHUNK] for t, tp in zip(ts, tps)]
        ps = [tp[CHUNK:] for tp in tps]
        power *= 2
    ts = [t + pdot(t, p) for t, p in zip(ts, ps)]
    yield
    block = INVERSE_BASE_BLOCK
    while block < CHUNK:
        off_mask = same_block(2 * block) & jnp.logical_not(same_block(block))
        ws = [pdot(t, jnp.where(off_mask, lo, 0.0)) for t, lo in zip(ts, lowers)]
        yield
        ts = [t - pdot(w, t) for t, w in zip(ts, ws)]
        yield
        block *= 2
    return ts


def _gdn_stages(xf_ref, gf_ref, xb_ref, gb_ref, of_ref, ob_ref, s_ref, group, chunks_per_block):
    pre = []
    for b in range(group):
        for d, g_ref in enumerate((gf_ref, gb_ref)):
            for pos in range(chunks_per_block):
                ci = pos if d == 0 else chunks_per_block - 1 - pos
                rows = slice(ci * CHUNK, (ci + 1) * CHUNK)
                gates = g_ref[b, rows, :]
                pre.append((b, d, pos, rows, gates, _cumsum_rows(_tri3(d), gates)))
    yield
    r, c, head = _pair_index()
    pairs = [[] for _ in range(chunks_per_block)]
    for b, d, pos, rows, gates, cum in pre:
        x_ref = xf_ref if d == 0 else xb_ref
        incl, strict = (c <= r, c < r) if d == 0 else (c >= r, c > r)
        cum_t = jnp.concatenate([cum] * PAIR, axis=0).T
        lane_head = lax.broadcasted_iota(jnp.int32, (CHUNK, PAIR * D_HEAD), 1) // D_HEAD
        for hp in range(N_HEADS // PAIR):
            hs = [PAIR * hp + i for i in range(PAIR)]
            jb, jg = [d * N_HEADS + h for h in hs], [2 * N_HEADS + d * N_HEADS + h for h in hs]
            part = lambda k: x_ref[b, rows, (k * N_HEADS + hs[0]) * D_HEAD:(k * N_HEADS + hs[0] + PAIR) * D_HEAD]
            q2, k2, v2 = part(0), part(1), part(2)
            beta2 = jnp.where(lane_head == 0, gates[:, jb[0]:jb[0] + 1], gates[:, jb[1]:jb[1] + 1])
            g_cum = [cum[:, j:j + 1] for j in jg]
            g_col = jnp.where(head == 0, g_cum[0], g_cum[1])
            g_row = jnp.where(head[0:1] == 0, cum_t[jg[0]:jg[0] + 1, :], cum_t[jg[1]:jg[1] + 1, :])
            diff = g_col - g_row
            pairs[pos].append(dict(
                b=b, d=d, hs=hs, rows=rows, q2=q2, k2=k2, v2=v2, beta2=beta2, g_cum=g_cum, strict=strict,
                k_beta2=k2 * beta2, lane_head=lane_head,
                decay_mask=jnp.where(incl, jnp.exp(jnp.where(incl, diff, 0.0)), 0.0)))
    flat = [pr for pos in pairs for pr in pos]

    kqs = [_dot_nt(jnp.concatenate([pr["k_beta2"].astype(BF16), pr["q2"].astype(BF16)], axis=0),
                   _block_diag(pr["k2"].astype(BF16))) for pr in flat]
    yield
    lowers = [jnp.where(pr["strict"], kq[:CHUNK] * pr["decay_mask"], 0.0) for pr, kq in zip(flat, kqs)]
    ts = yield from _unit_triangular_inverses(lowers)
    for pr, kq in zip(flat, kqs):
        e_g = [jnp.exp(g) for g in pr["g_cum"]]
        e_g2 = jnp.where(pr["lane_head"] == 0, e_g[0], e_g[1])
        pr["rhs"] = jnp.concatenate(
            [jnp.concatenate([(pr["v2"] * pr["beta2"])[:, i * D_HEAD:(i + 1) * D_HEAD],
                              (pr["k_beta2"] * e_g2)[:, i * D_HEAD:(i + 1) * D_HEAD]], axis=1) for i in range(PAIR)],
            axis=0).astype(BF16)
        attn2 = (kq[CHUNK:] * pr["decay_mask"]).astype(BF16)
        pr["attn"] = [attn2[:, i * CHUNK:(i + 1) * CHUNK] for i in range(PAIR)]
        g_last = [g[CHUNK - 1:CHUNK] if pr["d"] == 0 else g[0:1] for g in pr["g_cum"]]
        q_dec2 = (pr["q2"] * e_g2).astype(BF16)
        pr["q_dec"] = [q_dec2[:, i * D_HEAD:(i + 1) * D_HEAD] for i in range(PAIR)]
        pr["k_tail_t"] = [(pr["k2"][:, i * D_HEAD:(i + 1) * D_HEAD] * jnp.exp(g_last[i] - pr["g_cum"][i])).T.astype(BF16)
                          for i in range(PAIR)]
        pr["decay"] = [jnp.exp(g) for g in g_last]
    xs = [_dot(_block_diag(t.astype(BF16)), pr["rhs"]) for pr, t in zip(flat, ts)]
    yield
    chains = [[] for _ in range(chunks_per_block)]
    for pos in range(chunks_per_block):
        for pr, x in zip(pairs[pos], xs[pos * len(pairs[pos]):(pos + 1) * len(pairs[pos])]):
            for i, h in enumerate(pr["hs"]):
                xi = x[i * CHUNK:(i + 1) * CHUNK]
                chains[pos].append(dict(b=pr["b"], d=pr["d"], h=h, rows=pr["rows"], u=xi[:, :D_HEAD],
                                        w=xi[:, D_HEAD:].astype(BF16), attn=pr["attn"][i], q_dec=pr["q_dec"][i],
                                        k_tail_t=pr["k_tail_t"][i], decay=pr["decay"][i]))

    o_refs = (of_ref, ob_ref)
    states = [s_ref[ch["b"], ch["d"], ch["h"]] for ch in chains[0]]
    for pos in range(chunks_per_block):
        cur = chains[pos]
        sbs = [s.astype(BF16) for s in states]
        wss = [_dot(ch["w"], sb) for ch, sb in zip(cur, sbs)]
        yield
        vbs = [(ch["u"] - ws).astype(BF16) for ch, ws in zip(cur, wss)]
        outs = [_dot(jnp.concatenate([ch["q_dec"], ch["attn"]], axis=1), jnp.concatenate([sb, vb], axis=0))
                for ch, sb, vb in zip(cur, sbs, vbs)]
        yield
        upd = [_dot(ch["k_tail_t"], vb) for ch, vb in zip(cur, vbs)]
        yield
        states = [s * ch["decay"] + u for ch, s, u in zip(cur, states, upd)]
        for ch, o in zip(cur, outs):
            o_refs[ch["d"]][ch["b"], ch["rows"], ch["h"] * D_HEAD:(ch["h"] + 1) * D_HEAD] = o.astype(of_ref.dtype)
    for ch, s in zip(chains[0], states):
        s_ref[ch["b"], ch["d"], ch["h"]] = s


def _run_interleaved(primary, secondary, every):
    live = [primary, secondary]
    count = 0
    while live:
        gen = primary if primary in live and (secondary not in live or count < every) else secondary
        count = count + 1 if gen is primary else 0
        try:
            next(gen)
        except StopIteration:
            live.remove(gen)


def _scan_kernel(*refs, n_steps, chunks_per_block, group, hgrn, gdn, has_s0, emit_state):
    it = iter(refs)
    take = lambda k: [next(it) for _ in range(k)]
    h_in = take(6) if hgrn else None
    g_in = take(4) if gdn else None
    s0 = take(int(hgrn) + int(gdn)) if has_s0 else None
    h_out = take(2) if hgrn else None
    g_out = take(2) if gdn else None
    st = take(int(hgrn) + int(gdn)) if emit_state else None
    s_refs = take(int(hgrn) + int(gdn))
    n = pl.program_id(1)

    @pl.when(n == 0)
    def _():
        for k, s_ref in enumerate(s_refs):
            s_ref[...] = s0[k][...] if has_s0 else jnp.zeros_like(s_ref)

    gens = []
    if gdn:
        gens.append(_gdn_stages(*g_in, *g_out, s_refs[-1], group, chunks_per_block))
    if hgrn:
        gens.append(_hgrn_stages(*h_in, *h_out, s_refs[0], group))
    if len(gens) == 2:
        _run_interleaved(gens[0], gens[1], every=8)
    else:
        for _ in gens[0]:
            pass

    if emit_state:
        @pl.when(n == n_steps - 1)
        def _():
            for st_ref, s_ref in zip(st, s_refs):
                st_ref[...] = s_ref[...]


def _scan(hgrn_args, gdn_args, states0, batch, group, n_steps, gdn_block, emit_state, gdn_out_dtype):
    state_shape = (group, 2, N_HEADS, D_HEAD, D_HEAD)
    state_spec = pl.BlockSpec(state_shape, lambda b, n: (b, 0, 0, 0, 0))
    in_specs, args, out_specs, out_shape = [], [], [], []
    if hgrn_args is not None:
        qa, f_fwd, f_bwd, va = hgrn_args
        fwd = pl.BlockSpec((group, CHUNK, W_GROUP), lambda b, n: (b, n, 0))
        bwd = pl.BlockSpec((group, CHUNK, W_GROUP), lambda b, n: (b, n_steps - 1 - n, 0))
        in_specs += [fwd, fwd, fwd, bwd, bwd, bwd]
        args += [qa, f_fwd, va, qa, f_bwd, va]
        out_specs += [fwd, bwd]
        out_shape += [jax.ShapeDtypeStruct(qa.shape, BF16)] * 2
    chunks_per_block = 1
    if gdn_args is not None:
        qkv3, gates3 = gdn_args
        block_rows, columns = gdn_block
        chunks_per_block = block_rows // CHUNK
        fwd_map = (lambda b, n: (b, n, 0)) if columns == 1 else (lambda b, n: (b, 0, n))
        bwd_map = lambda b, n: fwd_map(b, n_steps - 1 - n)
        spec = lambda width, imap: pl.BlockSpec((group, block_rows, width), imap)
        in_specs += [spec(3 * W_GROUP, fwd_map), spec(GATE_LANES, fwd_map),
                     spec(3 * W_GROUP, bwd_map), spec(GATE_LANES, bwd_map)]
        args += [qkv3, gates3, qkv3, gates3]
        out_specs += [spec(W_GROUP, fwd_map), spec(W_GROUP, bwd_map)]
        out_shape += [jax.ShapeDtypeStruct((batch, qkv3.shape[1], columns * W_GROUP), gdn_out_dtype)] * 2
    n_mixers = int(hgrn_args is not None) + int(gdn_args is not None)
    if states0 is not None:
        in_specs, args = in_specs + [state_spec] * n_mixers, args + list(states0)
    if emit_state:
        out_specs = out_specs + [state_spec] * n_mixers
        out_shape = out_shape + [jax.ShapeDtypeStruct((batch,) + state_shape[1:], F32)] * n_mixers
    return pl.pallas_call(
        functools.partial(_scan_kernel, n_steps=n_steps, chunks_per_block=chunks_per_block, group=group,
                          hgrn=hgrn_args is not None, gdn=gdn_args is not None, has_s0=states0 is not None,
                          emit_state=emit_state),
        grid=(batch // group, n_steps),
        in_specs=in_specs, out_specs=out_specs, out_shape=out_shape,
        scratch_shapes=[pltpu.VMEM(state_shape, F32)] * n_mixers,
        compiler_params=pltpu.CompilerParams(dimension_semantics=("arbitrary", "arbitrary"),
                                             vmem_limit_bytes=VMEM_LIMIT),
        name="scan",
    )(*args)


FFN_SPLIT = 2
MXU_K_TILE = 256


def _ffn_pieces(d_ff):
    k_tiles = -(-d_ff // MXU_K_TILE)
    bounds = [min(d_ff, MXU_K_TILE * -(-k_tiles * s // FFN_SPLIT)) for s in range(FFN_SPLIT + 1)]
    return list(zip(bounds[:-1], bounds[1:]))


def _tail_kernel(*refs, latent):
    it = iter(refs)
    (x_ref, oaf_ref, oab_ref, obf_ref, obb_ref, ga_ref, zb_ref, mod_ref, na_ref, nb_ref, wo_ref, n2_ref,
     wg_ref, wu_ref, wd_ref, nf_ref) = (next(it) for _ in range(16))
    y_ref = next(it)
    ob_scr = next(it) if latent else None
    tm = x_ref.shape[0]

    def gated_norm(o, w_ref, gate_ref, h):
        sl = slice(h * D_HEAD, (h + 1) * D_HEAD)
        o = o * lax.rsqrt(jnp.mean(o * o, axis=-1, keepdims=True) + EPS)
        return (o * w_ref[:, sl] * gate_ref[:, sl].astype(F32)).astype(BF16)

    def both(f_ref, b_ref, h):
        sl = slice(h * D_HEAD, (h + 1) * D_HEAD)
        return f_ref[:, sl].astype(F32) + b_ref[:, sl].astype(F32)

    if latent:
        n_rows = tm // GRID_W
        for c in range(GRID_W):
            for h in range(N_HEADS):
                sl = slice(c * W_GROUP + h * D_HEAD, c * W_GROUP + (h + 1) * D_HEAD)
                ob_scr[h, pl.ds(c, n_rows, stride=ROW_PITCH), :] = obf_ref[0, :, sl] + obb_ref[0, :, sl]
        o_b = [jnp.concatenate([ob_scr[h, r * ROW_PITCH:r * ROW_PITCH + GRID_W, :] for r in range(n_rows)], axis=0)
               for h in range(N_HEADS)]
    else:
        o_b = [both(obf_ref, obb_ref, h) for h in range(N_HEADS)]
    o_a = [both(oaf_ref, oab_ref, h) for h in range(N_HEADS)]
    mixed = jnp.concatenate([gated_norm(o, na_ref, ga_ref, h) for h, o in enumerate(o_a)]
                            + [gated_norm(o, nb_ref, zb_ref, h) for h, o in enumerate(o_b)], axis=-1)
    m = mod_ref[0]
    x1 = x_ref[...] + m[2:3] * jnp.dot(mixed, wo_ref[...], preferred_element_type=F32)
    y = x1 * lax.rsqrt(jnp.mean(x1 * x1, axis=-1, keepdims=True) + EPS) * n2_ref[...]
    h2 = (y * (1.0 + m[4:5]) + m[3:4]).astype(BF16)

    ff = None
    for lo, hi in _ffn_pieces(wg_ref.shape[1]):
        cols = slice(lo, hi)
        gate = jnp.dot(h2, wg_ref[:, cols], preferred_element_type=F32)
        up = jnp.dot(h2, wu_ref[:, cols], preferred_element_type=F32)
        part = jnp.dot((_silu(gate) * up).astype(BF16), wd_ref[cols, :], preferred_element_type=F32)
        ff = part if ff is None else ff + part
    x2 = x1 + m[5:6] * ff
    y_ref[...] = x2 * lax.rsqrt(jnp.mean(x2 * x2, axis=-1, keepdims=True) + EPS) * nf_ref[...]


def _tail(x2d, oaf, oab, obf, obb, ga, zb, mod3, mod_row_of_tile, p, tm, latent, seq_len):
    n_tok = x2d.shape[0]
    assert p["w_gate"].shape[1] % LANE == 0
    tok = lambda width: pl.BlockSpec((tm, width), lambda i: (i, 0))
    ob_spec, scratch = tok(W_GROUP), []
    if latent:
        tiles_per_seq = seq_len // tm
        ob_spec = pl.BlockSpec((1, tm // GRID_W, GRID_W * W_GROUP), lambda i: (i // tiles_per_seq, i % tiles_per_seq, 0))
        scratch = [pltpu.VMEM((N_HEADS, tm // GRID_W * ROW_PITCH, D_HEAD), F32)]
    consts = [p["norm_a"], p["norm_b"], p["w_out"], p["norm2"], p["w_gate"], p["w_up"], p["w_down"], p["norm_f"]]
    return pl.pallas_call(
        functools.partial(_tail_kernel, latent=latent),
        grid=(n_tok // tm,),
        in_specs=[tok(D_MODEL), tok(W_GROUP), tok(W_GROUP), ob_spec, ob_spec, tok(W_GROUP), tok(W_GROUP),
                  pl.BlockSpec((1, 6, D_MODEL), lambda i: (mod_row_of_tile(i), 0, 0))]
                 + [_resident(a.shape) for a in consts],
        out_specs=tok(D_MODEL),
        out_shape=jax.ShapeDtypeStruct((n_tok, D_MODEL), F32),
        scratch_shapes=scratch,
        compiler_params=pltpu.CompilerParams(dimension_semantics=("arbitrary",),
                                             vmem_limit_bytes=VMEM_LIMIT),
        name="tail",
    )(x2d, oaf, oab, obf, obb, ga, zb, mod3, *consts)


def _stream(x, mod3, mod_row_of_tile, s0_a, s0_b, p, latent):
    batch, seq, _ = x.shape
    x2d = x.reshape(batch * seq, D_MODEL)
    tm = 512
    mod_row = functools.partial(mod_row_of_tile, tm=tm)
    qa, f_fwd, f_bwd, va, ga, qkv, zb, gates = _inproj(x2d, mod3, mod_row, p, tm, latent, seq)
    hgrn_args = [a.reshape(batch, seq, W_GROUP) for a in (qa, f_fwd, f_bwd, va)]
    n_chunks = seq // CHUNK
    if latent:
        assert seq // GRID_W == CHUNK and n_chunks == GRID_W
        oaf, oab, obf, obb = _scan(hgrn_args, (qkv, gates), (s0_a, s0_b), batch, group=4, n_steps=n_chunks,
                                   gdn_block=(CHUNK, GRID_W), emit_state=False, gdn_out_dtype=F32)
        states = (None, None)
    else:
        oaf, oab, new_a = _scan(hgrn_args, None, None, batch, group=4, n_steps=n_chunks, gdn_block=None,
                                emit_state=True, gdn_out_dtype=None)
        obf, obb, new_b = _scan(None, (qkv.reshape(batch, seq, 3 * W_GROUP), gates.reshape(batch, seq, GATE_LANES)),
                                None, batch, group=2, n_steps=1, gdn_block=(seq, 1), emit_state=True,
                                gdn_out_dtype=BF16)
        obf, obb = (o.reshape(batch * seq, W_GROUP) for o in (obf, obb))
        states = (new_a, new_b)
    oaf, oab = (o.reshape(batch * seq, W_GROUP) for o in (oaf, oab))
    y = _tail(x2d, oaf, oab, obf, obb, ga, zb, mod3, mod_row, p, tm, latent, seq)
    return y.reshape(batch, seq, D_MODEL), states


def kernel(x_prompt, x_sample, c, state_hgrn, state_gdn, c_ctx, w_ada, b_ada, norm1, norm2, w_in, conv_w,
           hgrn_lb, gdn_A_log, gdn_dt_bias, hgrn_out_norm, gdn_out_norm, w_out, w_gate, w_up, w_down, norm_f):
    depth = w_in.shape[0]
    assert depth == 1 and hgrn_lb.shape[0] == 2
    dec_batch, dec_seq, _ = x_sample.shape
    l = 0

    n_main = N_MAIN_GROUPS * W_GROUP
    pad8 = jnp.zeros((1, 2 * N_HEADS), F32)
    gparams = jnp.concatenate(
        [jnp.concatenate([pad8, a.reshape(1, 2 * N_HEADS).astype(F32),
                          jnp.zeros((1, GATE_LANES - 4 * N_HEADS), F32)], axis=1)
         for a in (gdn_A_log[l], gdn_dt_bias[l])], axis=0)
    p = {
        "norm1": norm1[l].reshape(1, D_MODEL), "norm2": norm2[l].reshape(1, D_MODEL),
        "lbp": hgrn_lb.reshape(2, 2 * W_GROUP), "gparams": gparams,
        "w_main": w_in[l].astype(BF16),
        "w_gates": jnp.pad(w_in[l][:, n_main:], ((0, 0), (0, GATE_LANES - 4 * N_HEADS))).astype(BF16),
        "conv_w": conv_w[l],
        "norm_a": hgrn_out_norm[l].reshape(1, W_GROUP), "norm_b": gdn_out_norm[l].reshape(1, W_GROUP),
        "w_out": w_out[l].astype(BF16), "w_gate": w_gate[l].astype(BF16), "w_up": w_up[l].astype(BF16),
        "w_down": w_down[l].astype(BF16), "norm_f": norm_f.reshape(1, D_MODEL),
    }

    n_mod_rows = 8
    cvec = jnp.concatenate([c_ctx[None], c, jnp.zeros((n_mod_rows - 1 - dec_batch, D_MODEL), F32)], axis=0)
    mod3 = _modulation(cvec, w_ada[l], b_ada[l]).reshape(n_mod_rows, 6, D_MODEL)

    y_prompt, (new_a, new_b) = _stream(x_prompt, mod3, lambda i, tm: 0, None, None, p, latent=False)
    y_sample, _ = _stream(x_sample, mod3, lambda i, tm: 1 + i // (dec_seq // tm), state_hgrn[:, l],
                          state_gdn[:, l], p, latent=True)
    return y_prompt, y_sample, new_a[:, None], new_b[:, None]
```

```python
import functools

import jax
import jax.numpy as jnp
from jax import lax
from jax.experimental import pallas as pl
from jax.experimental.pallas import tpu as pltpu

F32 = jnp.float32
BF16 = jnp.bfloat16

D_MODEL = 1024
N_HEADS = 4
D_HEAD = 128
W_GROUP = N_HEADS * D_HEAD
CHUNK = 64
GRID_W = 64
CONV_W = 3
EPS = 1e-6
N_MAIN_GROUPS = 9
GATE_LANES = 128
VMEM_LIMIT = 56 * 1024 * 1024


def _sigmoid(x):
    return 1.0 / (1.0 + jnp.exp(-x))


def _silu(x):
    return x * _sigmoid(x)


def _dot(a, b):
    return jnp.dot(a.astype(BF16), b.astype(BF16), preferred_element_type=F32)


def _dot_nt(a, b):
    return lax.dot_general(a.astype(BF16), b.astype(BF16), (((1,), (1,)), ((), ())),
                           preferred_element_type=F32)


def _split3(x):
    x1 = x.astype(BF16)
    r = x - x1.astype(F32)
    x2 = r.astype(BF16)
    x3 = (r - x2.astype(F32)).astype(BF16)
    return x1, x2, x3


def _cumsum_rows(tri3, x):
    return jnp.dot(tri3, jnp.concatenate(_split3(x), axis=0), preferred_element_type=F32)


def _tri3(direction):
    tri = _tri_masks(direction)[0].astype(BF16)
    return jnp.concatenate([tri, tri, tri], axis=1)


def _tri_masks(direction):
    r = lax.broadcasted_iota(jnp.int32, (CHUNK, CHUNK), 0)
    c = lax.broadcasted_iota(jnp.int32, (CHUNK, CHUNK), 1)
    if direction == 0:
        return c <= r, c < r
    return c >= r, c > r


def _resident(shape):
    return pl.BlockSpec(shape, lambda i: (0,) * len(shape), pipeline_mode=pl.Buffered(1))


def _mod_kernel(c_ref, w_ref, b_ref, o_ref):
    s = _silu(c_ref[...])
    o_ref[...] = _dot(s, w_ref[...]) + b_ref[...]


def _modulation(cvec, w_ada, b_ada):
    n_rows, d = cvec.shape
    n_out = w_ada.shape[1]
    tn = 1536
    return pl.pallas_call(
        _mod_kernel,
        grid=(n_out // tn,),
        in_specs=[pl.BlockSpec((n_rows, d), lambda j: (0, 0)),
                  pl.BlockSpec((d, tn), lambda j: (0, j)),
                  pl.BlockSpec((1, tn), lambda j: (0, j))],
        out_specs=pl.BlockSpec((n_rows, tn), lambda j: (0, j)),
        out_shape=jax.ShapeDtypeStruct((n_rows, n_out), F32),
        compiler_params=pltpu.CompilerParams(dimension_semantics=("arbitrary",),
                                             vmem_limit_bytes=VMEM_LIMIT),
        name="mod",
    )(cvec, w_ada, b_ada.reshape(1, n_out))


HALO = GRID_W
LANE = 128
QKV_TILES = 3 * W_GROUP // LANE
ROW_PITCH = GRID_W + 8


def _inproj_kernel(*refs, latent, seq_len):
    it = iter(refs)
    x_ref = next(it)
    xp_ref, xn_ref = (next(it), next(it)) if latent else (None, None)
    mod_ref, n1_ref, lb_ref, gp_ref, cw_ref, w_ref, wg_ref = (next(it) for _ in range(7))
    qa_ref, ff_ref, fb_ref, va_ref, ga_ref, qkv_ref, zb_ref, gates_ref = (next(it) for _ in range(8))
    qkv_scr, gates_scr = (next(it), next(it)) if latent else (None, None)
    tm = x_ref.shape[0]
    m = mod_ref[0]

    def normed(x):
        y = x * lax.rsqrt(jnp.mean(x * x, axis=-1, keepdims=True) + EPS) * n1_ref[...]
        return (y * (1.0 + m[1:2]) + m[0:1]).astype(BF16)

    hb = normed(x_ref[...])

    w_qkv = w_ref[:, 5 * W_GROUP:8 * W_GROUP]
    if latent:
        i = pl.program_id(0)
        tiles_per_seq = seq_len // tm
        ext = jnp.dot(jnp.concatenate([normed(xp_ref[...]), hb, normed(xn_ref[...])], axis=0), w_qkv,
                      preferred_element_type=F32)
        above = jnp.where(i % tiles_per_seq == 0, 0.0, ext[:HALO])
        below = jnp.where(i % tiles_per_seq == tiles_per_seq - 1, 0.0, ext[HALO + tm:])
        cur = ext[HALO:HALO + tm]
        prev = jnp.concatenate([above, ext[HALO:tm]], axis=0)
        nxt = jnp.concatenate([ext[2 * HALO:HALO + tm], below], axis=0)
    else:
        cur = jnp.dot(hb, w_qkv, preferred_element_type=F32)
        pos = lax.broadcasted_iota(jnp.int32, cur.shape, 0) % seq_len
        prev = jnp.where(pos == 0, 0.0, pltpu.roll(cur, 1, axis=0))
        nxt = jnp.where(pos == seq_len - 1, 0.0, pltpu.roll(cur, tm - 1, axis=0))
    cw = cw_ref[...]
    y = _silu(prev * cw[0:1] + cur * cw[1:2] + nxt * cw[2:3])
    tiles = []
    for j in range(QKV_TILES):
        t = y[:, j * LANE:(j + 1) * LANE]
        if j < 2 * N_HEADS:
            t = t * lax.rsqrt(jnp.sum(t * t, axis=-1, keepdims=True) + EPS)
        if j < N_HEADS:
            t = t * (D_HEAD ** -0.5)
        tiles.append(t)

    raw = jnp.dot(hb, wg_ref[...], preferred_element_type=F32)
    gp = gp_ref[...]
    z = raw + gp[1:2]
    softplus = jnp.maximum(z, 0.0) + jnp.log(1.0 + jnp.exp(-jnp.abs(z)))
    lane = lax.broadcasted_iota(jnp.int32, raw.shape, 1)
    gates = jnp.where(lane < 2 * N_HEADS, _sigmoid(raw),
                      jnp.where(lane < 4 * N_HEADS, -jnp.exp(gp[0:1]) * softplus, 0.0))

    if not latent:
        for j, t in enumerate(tiles):
            qkv_ref[:, j * LANE:(j + 1) * LANE] = t
        gates_ref[...] = gates
    else:
        n_rows = tm // GRID_W
        for r in range(n_rows):
            rows, dst = slice(r * GRID_W, (r + 1) * GRID_W), slice(r * ROW_PITCH, r * ROW_PITCH + GRID_W)
            for j, t in enumerate(tiles):
                qkv_scr[j, dst, :] = t[rows]
            gates_scr[dst, :] = gates[rows]
        for c in range(GRID_W):
            for j in range(QKV_TILES):
                qkv_ref[0, :, (c * QKV_TILES + j) * LANE:(c * QKV_TILES + j + 1) * LANE] = (
                    qkv_scr[j, pl.ds(c, n_rows, stride=ROW_PITCH), :])
            gates_ref[0, :, c * GATE_LANES:(c + 1) * GATE_LANES] = gates_scr[pl.ds(c, n_rows, stride=ROW_PITCH), :]

    def proj(j):
        return jnp.dot(hb, w_ref[:, j * W_GROUP:(j + 1) * W_GROUP], preferred_element_type=F32)

    lbp = lb_ref[...]
    e = jnp.exp(lbp - jnp.max(lbp, axis=0, keepdims=True))
    lb = e[0:1] / jnp.sum(e, axis=0, keepdims=True)
    lb_f, lb_b = lb[:, :W_GROUP], lb[:, W_GROUP:]

    qa_ref[...] = _silu(proj(0)).astype(BF16)
    ff_ref[...] = lb_f + (1.0 - lb_f) * _sigmoid(proj(1))
    fb_ref[...] = lb_b + (1.0 - lb_b) * _sigmoid(proj(2))
    va_ref[...] = proj(3).astype(BF16)
    ga_ref[...] = _silu(proj(4)).astype(BF16)
    zb_ref[...] = _silu(proj(8)).astype(BF16)


def _inproj(x2d, mod3, mod_row_of_tile, p, tm, latent, seq_len):
    n_tok = x2d.shape[0]
    tok = lambda width: pl.BlockSpec((tm, width), lambda i: (i, 0))
    widths = [W_GROUP] * 5 + [3 * W_GROUP, W_GROUP, GATE_LANES]
    out_specs = [tok(w) for w in widths]
    dtypes = [BF16, F32, F32, BF16, BF16, F32, BF16, F32]
    out_shape = [jax.ShapeDtypeStruct((n_tok, w), dt) for w, dt in zip(widths, dtypes)]
    in_specs, args, scratch = [tok(D_MODEL)], [x2d], []
    if latent:
        n_halo_blocks, per_tile, rows = n_tok // HALO, tm // HALO, tm // GRID_W
        tiles_per_seq = seq_len // tm
        in_specs += [pl.BlockSpec((HALO, D_MODEL), lambda i: (jnp.maximum(i * per_tile - 1, 0), 0)),
                     pl.BlockSpec((HALO, D_MODEL), lambda i: (jnp.minimum((i + 1) * per_tile, n_halo_blocks - 1), 0))]
        args += [x2d, x2d]
        col = lambda width: pl.BlockSpec((1, rows, GRID_W * width),
                                         lambda i: (i // tiles_per_seq, i % tiles_per_seq, 0))
        col_shape = lambda width: jax.ShapeDtypeStruct((n_tok // seq_len, seq_len // GRID_W, GRID_W * width), F32)
        for k, width in ((5, 3 * W_GROUP), (7, GATE_LANES)):
            out_specs[k], out_shape[k] = col(width), col_shape(width)
        scratch = [pltpu.VMEM((QKV_TILES, rows * ROW_PITCH, LANE), F32), pltpu.VMEM((rows * ROW_PITCH, GATE_LANES), F32)]
    consts = [p["norm1"], p["lbp"], p["gparams"], p["conv_w"], p["w_main"], p["w_gates"]]
    in_specs += [pl.BlockSpec((1, 6, D_MODEL), lambda i: (mod_row_of_tile(i), 0, 0))] + [_resident(a.shape) for a in consts]
    return pl.pallas_call(
        functools.partial(_inproj_kernel, latent=latent, seq_len=seq_len),
        grid=(n_tok // tm,),
        in_specs=in_specs, out_specs=out_specs, out_shape=out_shape, scratch_shapes=scratch,
        compiler_params=pltpu.CompilerParams(dimension_semantics=("arbitrary",),
                                             vmem_limit_bytes=VMEM_LIMIT),
        name="inproj",
    )(*args, mod3, *consts)


INVERSE_BASE_BLOCK = 8


def _hgrn_stages(qf_ref, ff_ref, vf_ref, qb_ref, fb_ref, vb_ref, of_ref, ob_ref, s_ref, group):
    pre = []
    for b in range(group):
        for d, f_ref in enumerate((ff_ref, fb_ref)):
            g_all = jnp.log(f_ref[b])
            pre.append((b, d, g_all, _cumsum_rows(_tri3(d), g_all)))
    yield
    chains = []
    for b, d, g_all, cum_all in pre:
        q_ref, f_ref, v_ref, o_ref = (qf_ref, ff_ref, vf_ref, of_ref) if d == 0 else (qb_ref, fb_ref, vb_ref, ob_ref)
        incl, _ = _tri_masks(d)
        for h in range(N_HEADS):
            sl = slice(h * D_HEAD, (h + 1) * D_HEAD)
            k = 1.0 - f_ref[b, :, sl]
            g, G = g_all[:, sl], cum_all[:, sl]
            g_last_row = G[CHUNK - 1:CHUNK] if d == 0 else G[0:1]
            chains.append(dict(
                b=b, d=d, h=h, sl=sl, o_ref=o_ref, incl=incl, vb=v_ref[b, :, sl].astype(BF16),
                decay=jnp.exp(jnp.sum(g.T, axis=1, keepdims=True)),
                q_dec=(q_ref[b, :, sl].astype(F32) * jnp.exp(G)).astype(BF16), k_dec=k * jnp.exp(-G),
                k_tail_t=(k * jnp.exp(g_last_row - G)).T.astype(BF16)))
    attns = [jnp.where(ch["incl"], _dot_nt(ch["q_dec"], ch["k_dec"]), 0.0).astype(BF16) for ch in chains]
    yield
    states = [s_ref[ch["b"], ch["d"], ch["h"]] for ch in chains]
    outs = [_dot(jnp.concatenate([ch["q_dec"], attn], axis=1), jnp.concatenate([s.astype(BF16), ch["vb"]], axis=0))
            for ch, attn, s in zip(chains, attns, states)]
    yield
    upds = [_dot(ch["k_tail_t"], ch["vb"]) for ch in chains]
    for ch, o, s, u in zip(chains, outs, states, upds):
        ch["o_ref"][ch["b"], :, ch["sl"]] = o.astype(BF16)
        s_ref[ch["b"], ch["d"], ch["h"]] = ch["decay"] * s + u


PAIR = 2


def _pair_index():
    r = lax.broadcasted_iota(jnp.int32, (CHUNK, PAIR * CHUNK), 0)
    lane = lax.broadcasted_iota(jnp.int32, (CHUNK, PAIR * CHUNK), 1)
    return r, lane % CHUNK, lane // CHUNK


def _block_diag(packed):
    head = lax.broadcasted_iota(jnp.int32, packed.shape, 1) // (packed.shape[1] // PAIR)
    return jnp.concatenate([jnp.where(head == h, packed, jnp.zeros_like(packed)) for h in range(PAIR)], axis=0)


def _unit_triangular_inverses(lowers):
    r, c, _ = _pair_index()
    same_block = lambda block: r // block == c // block
    pdot = lambda a, b: _dot(a, _block_diag(b.astype(BF16)))
    eye = jnp.where(r == c, 1.0, 0.0)
    diag = same_block(INVERSE_BASE_BLOCK)
    ds = [jnp.where(diag, lo, 0.0) for lo in lowers]
    ts = [eye - d for d in ds]
    ps = [pdot(d, d) for d in ds]
    yield
    power = 4
    while power < INVERSE_BASE_BLOCK:
        tps = [pdot(jnp.concatenate([t.astype(BF16), p.astype(BF16)], axis=0), p) for t, p in zip(ts, ps)]
        yield
        ts = [t + tp[:CHUNK] for t, tp in zip(ts, tps)]
        ps = [tp[CHUNK:] for tp in tps]
        power *= 2
    ts = [t + pdot(t, p) for t, p in zip(ts, ps)]
    yield
    block = INVERSE_BASE_BLOCK
    while block < CHUNK:
        off_mask = same_block(2 * block) & jnp.logical_not(same_block(block))
        ws = [pdot(t, jnp.where(off_mask, lo, 0.0)) for t, lo in zip(ts, lowers)]
        yield
        ts = [t - pdot(w, t) for t, w in zip(ts, ws)]
        yield
        block *= 2
    return ts


def _gdn_stages(xf_ref, gf_ref, xb_ref, gb_ref, of_ref, ob_ref, s_ref, group):
    pre = []
    for b in range(group):
        for d, g_ref in enumerate((gf_ref, gb_ref)):
            gates = g_ref[b]
            pre.append((b, d, gates, _cumsum_rows(_tri3(d), gates)))
    yield
    r, c, head = _pair_index()
    lane_head = lax.broadcasted_iota(jnp.int32, (CHUNK, PAIR * D_HEAD), 1) // D_HEAD
    pairs = []
    for b, d, gates, cum in pre:
        x_ref = xf_ref if d == 0 else xb_ref
        incl, strict = (c <= r, c < r) if d == 0 else (c >= r, c > r)
        cum_t = jnp.concatenate([cum] * PAIR, axis=0).T
        for hp in range(N_HEADS // PAIR):
            hs = [PAIR * hp + i for i in range(PAIR)]
            jb, jg = [d * N_HEADS + h for h in hs], [2 * N_HEADS + d * N_HEADS + h for h in hs]
            part = lambda k: x_ref[b, :, (k * N_HEADS + hs[0]) * D_HEAD:(k * N_HEADS + hs[0] + PAIR) * D_HEAD]
            q2, k2, v2 = part(0), part(1), part(2)
            beta2 = jnp.where(lane_head == 0, gates[:, jb[0]:jb[0] + 1], gates[:, jb[1]:jb[1] + 1])
            g_cum = [cum[:, j:j + 1] for j in jg]
            g_col = jnp.where(head == 0, g_cum[0], g_cum[1])
            g_row = jnp.where(head[0:1] == 0, cum_t[jg[0]:jg[0] + 1, :], cum_t[jg[1]:jg[1] + 1, :])
            diff = g_col - g_row
            pairs.append(dict(
                b=b, d=d, hs=hs, q2=q2, k2=k2, v2=v2, beta2=beta2, g_cum=g_cum, strict=strict, k_beta2=k2 * beta2,
                decay_mask=jnp.where(incl, jnp.exp(jnp.where(incl, diff, 0.0)), 0.0)))

    kqs = [_dot_nt(jnp.concatenate([pr["k_beta2"].astype(BF16), pr["q2"].astype(BF16)], axis=0),
                   _block_diag(pr["k2"].astype(BF16))) for pr in pairs]
    yield
    lowers = [jnp.where(pr["strict"], kq[:CHUNK] * pr["decay_mask"], 0.0) for pr, kq in zip(pairs, kqs)]
    ts = yield from _unit_triangular_inverses(lowers)
    for pr, kq in zip(pairs, kqs):
        e_g = [jnp.exp(g) for g in pr["g_cum"]]
        e_g2 = jnp.where(lane_head == 0, e_g[0], e_g[1])
        pr["rhs"] = jnp.concatenate(
            [jnp.concatenate([(pr["v2"] * pr["beta2"])[:, i * D_HEAD:(i + 1) * D_HEAD],
                              (pr["k_beta2"] * e_g2)[:, i * D_HEAD:(i + 1) * D_HEAD]], axis=1) for i in range(PAIR)],
            axis=0).astype(BF16)
        attn2 = (kq[CHUNK:] * pr["decay_mask"]).astype(BF16)
        pr["attn"] = [attn2[:, i * CHUNK:(i + 1) * CHUNK] for i in range(PAIR)]
        g_last = [g[CHUNK - 1:CHUNK] if pr["d"] == 0 else g[0:1] for g in pr["g_cum"]]
        q_dec2 = (pr["q2"] * e_g2).astype(BF16)
        pr["q_dec"] = [q_dec2[:, i * D_HEAD:(i + 1) * D_HEAD] for i in range(PAIR)]
        pr["k_tail_t"] = [(pr["k2"][:, i * D_HEAD:(i + 1) * D_HEAD] * jnp.exp(g_last[i] - pr["g_cum"][i])).T.astype(BF16)
                          for i in range(PAIR)]
        pr["decay"] = [jnp.exp(g) for g in g_last]
    xs = [_dot(_block_diag(t.astype(BF16)), pr["rhs"]) for pr, t in zip(pairs, ts)]
    yield
    chains = []
    for pr, x in zip(pairs, xs):
        for i, h in enumerate(pr["hs"]):
            xi = x[i * CHUNK:(i + 1) * CHUNK]
            chains.append(dict(b=pr["b"], d=pr["d"], h=h, u=xi[:, :D_HEAD], w=xi[:, D_HEAD:].astype(BF16),
                               attn=pr["attn"][i], q_dec=pr["q_dec"][i], k_tail_t=pr["k_tail_t"][i],
                               decay=pr["decay"][i]))

    o_refs = (of_ref, ob_ref)
    states = [s_ref[ch["b"], ch["d"], ch["h"]] for ch in chains]
    sbs = [s.astype(BF16) for s in states]
    wss = [_dot(ch["w"], sb) for ch, sb in zip(chains, sbs)]
    yield
    vbs = [(ch["u"] - ws).astype(BF16) for ch, ws in zip(chains, wss)]
    outs = [_dot(jnp.concatenate([ch["q_dec"], ch["attn"]], axis=1), jnp.concatenate([sb, vb], axis=0))
            for ch, sb, vb in zip(chains, sbs, vbs)]
    yield
    upd = [_dot(ch["k_tail_t"], vb) for ch, vb in zip(chains, vbs)]
    for ch, s, u, o in zip(chains, states, upd, outs):
        o_refs[ch["d"]][ch["b"], :, ch["h"] * D_HEAD:(ch["h"] + 1) * D_HEAD] = o.astype(of_ref.dtype)
        s_ref[ch["b"], ch["d"], ch["h"]] = s * ch["decay"] + u


def _run_interleaved(primary, secondary, every):
    live = [primary, secondary]
    count = 0
    while live:
        gen = primary if primary in live and (secondary not in live or count < every) else secondary
        count = count + 1 if gen is primary else 0
        try:
            next(gen)
        except StopIteration:
            live.remove(gen)


def _scan_kernel(*refs, n_steps, group, hgrn, gdn, has_s0, emit_state):
    it = iter(refs)
    take = lambda k: [next(it) for _ in range(k)]
    h_in = take(6) if hgrn else None
    g_in = take(4) if gdn else None
    s0 = take(int(hgrn) + int(gdn)) if has_s0 else None
    h_out = take(2) if hgrn else None
    g_out = take(2) if gdn else None
    st = take(int(hgrn) + int(gdn)) if emit_state else None
    s_refs = take(int(hgrn) + int(gdn))
    n = pl.program_id(1)

    @pl.when(n == 0)
    def _():
        for k, s_ref in enumerate(s_refs):
            s_ref[...] = s0[k][...] if has_s0 else jnp.zeros_like(s_ref)

    gens = []
    if gdn:
        gens.append(_gdn_stages(*g_in, *g_out, s_refs[-1], group))
    if hgrn:
        gens.append(_hgrn_stages(*h_in, *h_out, s_refs[0], group))
    if len(gens) == 2:
        _run_interleaved(gens[0], gens[1], every=8)
    else:
        for _ in gens[0]:
            pass

    if emit_state:
        @pl.when(n == n_steps - 1)
        def _():
            for st_ref, s_ref in zip(st, s_refs):
                st_ref[...] = s_ref[...]


def _scan(hgrn_args, gdn_args, states0, batch, group, n_steps, gdn_columns, emit_state, gdn_out_dtype):
    state_shape = (group, 2, N_HEADS, D_HEAD, D_HEAD)
    state_spec = pl.BlockSpec(state_shape, lambda b, n: (b, 0, 0, 0, 0))
    in_specs, args, out_specs, out_shape = [], [], [], []
    if hgrn_args is not None:
        qa, f_fwd, f_bwd, va = hgrn_args
        fwd = pl.BlockSpec((group, CHUNK, W_GROUP), lambda b, n: (b, n, 0))
        bwd = pl.BlockSpec((group, CHUNK, W_GROUP), lambda b, n: (b, n_steps - 1 - n, 0))
        in_specs += [fwd, fwd, fwd, bwd, bwd, bwd]
        args += [qa, f_fwd, va, qa, f_bwd, va]
        out_specs += [fwd, bwd]
        out_shape += [jax.ShapeDtypeStruct(qa.shape, BF16)] * 2
    if gdn_args is not None:
        qkv3, gates3 = gdn_args
        columns = gdn_columns
        fwd_map = (lambda b, n: (b, n, 0)) if columns == 1 else (lambda b, n: (b, 0, n))
        bwd_map = lambda b, n: fwd_map(b, n_steps - 1 - n)
        spec = lambda width, imap: pl.BlockSpec((group, CHUNK, width), imap)
        in_specs += [spec(3 * W_GROUP, fwd_map), spec(GATE_LANES, fwd_map),
                     spec(3 * W_GROUP, bwd_map), spec(GATE_LANES, bwd_map)]
        args += [qkv3, gates3, qkv3, gates3]
        out_specs += [spec(W_GROUP, fwd_map), spec(W_GROUP, bwd_map)]
        out_shape += [jax.ShapeDtypeStruct((batch, qkv3.shape[1], columns * W_GROUP), gdn_out_dtype)] * 2
    n_mixers = int(hgrn_args is not None) + int(gdn_args is not None)
    if states0 is not None:
        in_specs, args = in_specs + [state_spec] * n_mixers, args + list(states0)
    if emit_state:
        out_specs = out_specs + [state_spec] * n_mixers
        out_shape = out_shape + [jax.ShapeDtypeStruct((batch,) + state_shape[1:], F32)] * n_mixers
    return pl.pallas_call(
        functools.partial(_scan_kernel, n_steps=n_steps, group=group, hgrn=hgrn_args is not None,
                          gdn=gdn_args is not None, has_s0=states0 is not None, emit_state=emit_state),
        grid=(batch // group, n_steps),
        in_specs=in_specs, out_specs=out_specs, out_shape=out_shape,
        scratch_shapes=[pltpu.VMEM(state_shape, F32)] * n_mixers,
        compiler_params=pltpu.CompilerParams(dimension_semantics=("arbitrary", "arbitrary"),
                                             vmem_limit_bytes=VMEM_LIMIT),
        name="scan",
    )(*args)


FFN_SPLIT = 2
MXU_K_TILE = 256


def _ffn_pieces(d_ff):
    k_tiles = -(-d_ff // MXU_K_TILE)
    bounds = [min(d_ff, MXU_K_TILE * -(-k_tiles * s // FFN_SPLIT)) for s in range(FFN_SPLIT + 1)]
    return list(zip(bounds[:-1], bounds[1:]))


def _tail_kernel(*refs, latent):
    it = iter(refs)
    (x_ref, oaf_ref, oab_ref, obf_ref, obb_ref, ga_ref, zb_ref, mod_ref, na_ref, nb_ref, wo_ref, n2_ref,
     wg_ref, wu_ref, wd_ref, nf_ref) = (next(it) for _ in range(16))
    y_ref = next(it)
    ob_scr = next(it) if latent else None
    tm = x_ref.shape[0]

    def gated_norm(o, w_ref, gate_ref, h):
        sl = slice(h * D_HEAD, (h + 1) * D_HEAD)
        o = o * lax.rsqrt(jnp.mean(o * o, axis=-1, keepdims=True) + EPS)
        return (o * w_ref[:, sl] * gate_ref[:, sl].astype(F32)).astype(BF16)

    def both(f_ref, b_ref, h):
        sl = slice(h * D_HEAD, (h + 1) * D_HEAD)
        return f_ref[:, sl].astype(F32) + b_ref[:, sl].astype(F32)

    if latent:
        n_rows = tm // GRID_W
        for c in range(GRID_W):
            for h in range(N_HEADS):
                sl = slice(c * W_GROUP + h * D_HEAD, c * W_GROUP + (h + 1) * D_HEAD)
                ob_scr[h, pl.ds(c, n_rows, stride=ROW_PITCH), :] = obf_ref[0, :, sl] + obb_ref[0, :, sl]
        o_b = [jnp.concatenate([ob_scr[h, r * ROW_PITCH:r * ROW_PITCH + GRID_W, :] for r in range(n_rows)], axis=0)
               for h in range(N_HEADS)]
    else:
        o_b = [both(obf_ref, obb_ref, h) for h in range(N_HEADS)]
    o_a = [both(oaf_ref, oab_ref, h) for h in range(N_HEADS)]
    mixed = jnp.concatenate([gated_norm(o, na_ref, ga_ref, h) for h, o in enumerate(o_a)]
                            + [gated_norm(o, nb_ref, zb_ref, h) for h, o in enumerate(o_b)], axis=-1)
    m = mod_ref[0]
    x1 = x_ref[...] + m[2:3] * jnp.dot(mixed, wo_ref[...], preferred_element_type=F32)
    y = x1 * lax.rsqrt(jnp.mean(x1 * x1, axis=-1, keepdims=True) + EPS) * n2_ref[...]
    h2 = (y * (1.0 + m[4:5]) + m[3:4]).astype(BF16)

    ff = None
    for lo, hi in _ffn_pieces(wg_ref.shape[1]):
        cols = slice(lo, hi)
        gate = jnp.dot(h2, wg_ref[:, cols], preferred_element_type=F32)
        up = jnp.dot(h2, wu_ref[:, cols], preferred_element_type=F32)
        part = jnp.dot((_silu(gate) * up).astype(BF16), wd_ref[cols, :], preferred_element_type=F32)
        ff = part if ff is None else ff + part
    x2 = x1 + m[5:6] * ff
    y_ref[...] = x2 * lax.rsqrt(jnp.mean(x2 * x2, axis=-1, keepdims=True) + EPS) * nf_ref[...]


def _tail(x2d, oaf, oab, obf, obb, ga, zb, mod3, mod_row_of_tile, p, tm, latent, seq_len):
    n_tok = x2d.shape[0]
    assert p["w_gate"].shape[1] % LANE == 0
    tok = lambda width: pl.BlockSpec((tm, width), lambda i: (i, 0))
    ob_spec, scratch = tok(W_GROUP), []
    if latent:
        tiles_per_seq = seq_len // tm
        ob_spec = pl.BlockSpec((1, tm // GRID_W, GRID_W * W_GROUP), lambda i: (i // tiles_per_seq, i % tiles_per_seq, 0))
        scratch = [pltpu.VMEM((N_HEADS, tm // GRID_W * ROW_PITCH, D_HEAD), F32)]
    consts = [p["norm_a"], p["norm_b"], p["w_out"], p["norm2"], p["w_gate"], p["w_up"], p["w_down"], p["norm_f"]]
    return pl.pallas_call(
        functools.partial(_tail_kernel, latent=latent),
        grid=(n_tok // tm,),
        in_specs=[tok(D_MODEL), tok(W_GROUP), tok(W_GROUP), ob_spec, ob_spec, tok(W_GROUP), tok(W_GROUP),
                  pl.BlockSpec((1, 6, D_MODEL), lambda i: (mod_row_of_tile(i), 0, 0))]
                 + [_resident(a.shape) for a in consts],
        out_specs=tok(D_MODEL),
        out_shape=jax.ShapeDtypeStruct((n_tok, D_MODEL), F32),
        scratch_shapes=scratch,
        compiler_params=pltpu.CompilerParams(dimension_semantics=("arbitrary",),
                                             vmem_limit_bytes=VMEM_LIMIT),
        name="tail",
    )(x2d, oaf, oab, obf, obb, ga, zb, mod3, *consts)


def _stream(x, mod3, mod_row_of_tile, s0_a, s0_b, p, latent):
    batch, seq, _ = x.shape
    x2d = x.reshape(batch * seq, D_MODEL)
    tm = 512
    mod_row = functools.partial(mod_row_of_tile, tm=tm)
    qa, f_fwd, f_bwd, va, ga, qkv, zb, gates = _inproj(x2d, mod3, mod_row, p, tm, latent, seq)
    hgrn_args = [a.reshape(batch, seq, W_GROUP) for a in (qa, f_fwd, f_bwd, va)]
    n_chunks = seq // CHUNK
    if latent:
        assert seq // GRID_W == CHUNK and n_chunks == GRID_W
        oaf, oab, obf, obb = _scan(hgrn_args, (qkv, gates), (s0_a, s0_b), batch, group=4, n_steps=n_chunks,
                                   gdn_columns=GRID_W, emit_state=False, gdn_out_dtype=F32)
        states = (None, None)
    else:
        gdn_args = (qkv.reshape(batch, seq, 3 * W_GROUP), gates.reshape(batch, seq, GATE_LANES))
        oaf, oab, obf, obb, new_a, new_b = _scan(hgrn_args, gdn_args, None, batch, group=4, n_steps=n_chunks,
                                                 gdn_columns=1, emit_state=True, gdn_out_dtype=BF16)
        obf, obb = (o.reshape(batch * seq, W_GROUP) for o in (obf, obb))
        states = (new_a, new_b)
    oaf, oab = (o.reshape(batch * seq, W_GROUP) for o in (oaf, oab))
    y = _tail(x2d, oaf, oab, obf, obb, ga, zb, mod3, mod_row, p, tm, latent, seq)
    return y.reshape(batch, seq, D_MODEL), states


def kernel(x_prompt, x_sample, c, state_hgrn, state_gdn, c_ctx, w_ada, b_ada, norm1, norm2, w_in, conv_w,
           hgrn_lb, gdn_A_log, gdn_dt_bias, hgrn_out_norm, gdn_out_norm, w_out, w_gate, w_up, w_down, norm_f):
    depth = w_in.shape[0]
    assert depth == 1 and hgrn_lb.shape[0] == 2
    dec_batch, dec_seq, _ = x_sample.shape
    l = 0

    n_main = N_MAIN_GROUPS * W_GROUP
    pad8 = jnp.zeros((1, 2 * N_HEADS), F32)
    gparams = jnp.concatenate(
        [jnp.concatenate([pad8, a.reshape(1, 2 * N_HEADS).astype(F32),
                          jnp.zeros((1, GATE_LANES - 4 * N_HEADS), F32)], axis=1)
         for a in (gdn_A_log[l], gdn_dt_bias[l])], axis=0)
    p = {
        "norm1": norm1[l].reshape(1, D_MODEL), "norm2": norm2[l].reshape(1, D_MODEL),
        "lbp": hgrn_lb.reshape(2, 2 * W_GROUP), "gparams": gparams,
        "w_main": w_in[l].astype(BF16),
        "w_gates": jnp.pad(w_in[l][:, n_main:], ((0, 0), (0, GATE_LANES - 4 * N_HEADS))).astype(BF16),
        "conv_w": conv_w[l],
        "norm_a": hgrn_out_norm[l].reshape(1, W_GROUP), "norm_b": gdn_out_norm[l].reshape(1, W_GROUP),
        "w_out": w_out[l].astype(BF16), "w_gate": w_gate[l].astype(BF16), "w_up": w_up[l].astype(BF16),
        "w_down": w_down[l].astype(BF16), "norm_f": norm_f.reshape(1, D_MODEL),
    }

    n_mod_rows = 8
    cvec = jnp.concatenate([c_ctx[None], c, jnp.zeros((n_mod_rows - 1 - dec_batch, D_MODEL), F32)], axis=0)
    mod3 = _modulation(cvec, w_ada[l], b_ada[l]).reshape(n_mod_rows, 6, D_MODEL)

    y_prompt, (new_a, new_b) = _stream(x_prompt, mod3, lambda i, tm: 0, None, None, p, latent=False)
    y_sample, _ = _stream(x_sample, mod3, lambda i, tm: 1 + i // (dec_seq // tm), state_hgrn[:, l],
                          state_gdn[:, l], p, latent=True)
    return y_prompt, y_sample, new_a[:, None], new_b[:, None]
```

```python
import functools

import jax
import jax.numpy as jnp
from jax import lax
from jax.experimental import pallas as pl
from jax.experimental.pallas import tpu as pltpu

F32 = jnp.float32
BF16 = jnp.bfloat16

D_MODEL = 1024
N_HEADS = 4
D_HEAD = 128
W_GROUP = N_HEADS * D_HEAD
CHUNK = 64
GRID_W = 64
CONV_W = 3
EPS = 1e-6
N_MAIN_GROUPS = 9
GATE_LANES = 128
VMEM_LIMIT = 56 * 1024 * 1024


NEG_LOG2_E = -1.4426950408889634


def _sigmoid(x):
    return 1.0 / (1.0 + jnp.exp2(x * NEG_LOG2_E))


def _silu(x):
    return x * _sigmoid(x)


def _dot(a, b):
    return jnp.dot(a.astype(BF16), b.astype(BF16), preferred_element_type=F32)


def _dot_nt(a, b):
    return lax.dot_general(a.astype(BF16), b.astype(BF16), (((1,), (1,)), ((), ())),
                           preferred_element_type=F32)


def _split3(x):
    x1 = x.astype(BF16)
    r = x - x1.astype(F32)
    x2 = r.astype(BF16)
    x3 = (r - x2.astype(F32)).astype(BF16)
    return x1, x2, x3


def _cumsum_rows(tri3, x):
    return jnp.dot(tri3, jnp.concatenate(_split3(x), axis=0), preferred_element_type=F32)


def _tri3(direction):
    tri = _tri_masks(direction)[0].astype(BF16)
    return jnp.concatenate([tri, tri, tri], axis=1)


def _tri_masks(direction):
    r = lax.broadcasted_iota(jnp.int32, (CHUNK, CHUNK), 0)
    c = lax.broadcasted_iota(jnp.int32, (CHUNK, CHUNK), 1)
    if direction == 0:
        return c <= r, c < r
    return c >= r, c > r


def _resident(shape):
    return pl.BlockSpec(shape, lambda i: (0,) * len(shape), pipeline_mode=pl.Buffered(1))


def _mod_kernel(c_ref, w_ref, b_ref, o_ref):
    s = _silu(c_ref[...])
    o_ref[...] = _dot(s, w_ref[...]) + b_ref[...]


def _modulation(cvec, w_ada, b_ada):
    n_rows, d = cvec.shape
    n_out = w_ada.shape[1]
    tn = 1536
    return pl.pallas_call(
        _mod_kernel,
        grid=(n_out // tn,),
        in_specs=[pl.BlockSpec((n_rows, d), lambda j: (0, 0)),
                  pl.BlockSpec((d, tn), lambda j: (0, j)),
                  pl.BlockSpec((1, tn), lambda j: (0, j))],
        out_specs=pl.BlockSpec((n_rows, tn), lambda j: (0, j)),
        out_shape=jax.ShapeDtypeStruct((n_rows, n_out), F32),
        compiler_params=pltpu.CompilerParams(dimension_semantics=("arbitrary",),
                                             vmem_limit_bytes=VMEM_LIMIT),
        name="mod",
    )(cvec, w_ada, b_ada.reshape(1, n_out))


HALO = GRID_W
LANE = 128
QKV_TILES = 3 * W_GROUP // LANE
ROW_PITCH = GRID_W + 8


def _inproj_kernel(*refs, latent, seq_len):
    it = iter(refs)
    x_ref = next(it)
    xp_ref, xn_ref = (next(it), next(it)) if latent else (None, None)
    mod_ref, n1_ref, lb_ref, gp_ref, cw_ref, w_ref, wg_ref = (next(it) for _ in range(7))
    qa_ref, ff_ref, fb_ref, va_ref, ga_ref, qkv_ref, zb_ref, gates_ref = (next(it) for _ in range(8))
    qkv_scr, gates_scr = (next(it), next(it)) if latent else (None, None)
    tm = x_ref.shape[0]
    m = mod_ref[0]

    norm_scale = n1_ref[...] * (1.0 + m[1:2])

    def normed(x):
        return (x * lax.rsqrt(jnp.mean(x * x, axis=-1, keepdims=True) + EPS) * norm_scale + m[0:1]).astype(BF16)

    hb = normed(x_ref[...])

    w_qkv = w_ref[:, 5 * W_GROUP:8 * W_GROUP]
    if latent:
        i = pl.program_id(0)
        tiles_per_seq = seq_len // tm
        ext = jnp.dot(jnp.concatenate([normed(xp_ref[...]), hb, normed(xn_ref[...])], axis=0), w_qkv,
                      preferred_element_type=F32)
        above = jnp.where(i % tiles_per_seq == 0, 0.0, ext[:HALO])
        below = jnp.where(i % tiles_per_seq == tiles_per_seq - 1, 0.0, ext[HALO + tm:])
        cur = ext[HALO:HALO + tm]
        prev = jnp.concatenate([above, ext[HALO:tm]], axis=0)
        nxt = jnp.concatenate([ext[2 * HALO:HALO + tm], below], axis=0)
    else:
        cur = jnp.dot(hb, w_qkv, preferred_element_type=F32)
        pos = lax.broadcasted_iota(jnp.int32, cur.shape, 0) % seq_len
        prev = jnp.where(pos == 0, 0.0, pltpu.roll(cur, 1, axis=0))
        nxt = jnp.where(pos == seq_len - 1, 0.0, pltpu.roll(cur, tm - 1, axis=0))
    cw = cw_ref[...]
    y = _silu(prev * cw[0:1] + cur * cw[1:2] + nxt * cw[2:3])
    tiles = []
    for j in range(QKV_TILES):
        t = y[:, j * LANE:(j + 1) * LANE]
        if j < 2 * N_HEADS:
            inv = lax.rsqrt(jnp.sum(t * t, axis=-1, keepdims=True) + EPS)
            t = t * (inv * (D_HEAD ** -0.5) if j < N_HEADS else inv)
        tiles.append(t)

    raw = jnp.dot(hb, wg_ref[...], preferred_element_type=F32)
    gp = gp_ref[...]
    z = raw + gp[1:2]
    softplus = jnp.maximum(z, 0.0) + jnp.log(1.0 + jnp.exp(-jnp.abs(z)))
    lane = lax.broadcasted_iota(jnp.int32, raw.shape, 1)
    gates = jnp.where(lane < 2 * N_HEADS, _sigmoid(raw),
                      jnp.where(lane < 4 * N_HEADS, -jnp.exp(gp[0:1]) * softplus, 0.0))

    if not latent:
        for j, t in enumerate(tiles):
            qkv_ref[:, j * LANE:(j + 1) * LANE] = t
        gates_ref[...] = gates
    else:
        n_rows = tm // GRID_W
        for r in range(n_rows):
            rows, dst = slice(r * GRID_W, (r + 1) * GRID_W), slice(r * ROW_PITCH, r * ROW_PITCH + GRID_W)
            for j, t in enumerate(tiles):
                qkv_scr[j, dst, :] = t[rows]
            gates_scr[dst, :] = gates[rows]
        for c in range(GRID_W):
            for j in range(QKV_TILES):
                qkv_ref[0, :, (c * QKV_TILES + j) * LANE:(c * QKV_TILES + j + 1) * LANE] = (
                    qkv_scr[j, pl.ds(c, n_rows, stride=ROW_PITCH), :])
            gates_ref[0, :, c * GATE_LANES:(c + 1) * GATE_LANES] = gates_scr[pl.ds(c, n_rows, stride=ROW_PITCH), :]

    def proj(j):
        return jnp.dot(hb, w_ref[:, j * W_GROUP:(j + 1) * W_GROUP], preferred_element_type=F32)

    lbp = lb_ref[...]
    e = jnp.exp(lbp - jnp.max(lbp, axis=0, keepdims=True))
    lb = e[0:1] / jnp.sum(e, axis=0, keepdims=True)
    lb_f, lb_b = lb[:, :W_GROUP], lb[:, W_GROUP:]

    qa_ref[...] = _silu(proj(0)).astype(BF16)
    ff_ref[...] = lb_f + (1.0 - lb_f) * _sigmoid(proj(1))
    fb_ref[...] = lb_b + (1.0 - lb_b) * _sigmoid(proj(2))
    va_ref[...] = proj(3).astype(BF16)
    ga_ref[...] = _silu(proj(4)).astype(BF16)
    zb_ref[...] = _silu(proj(8)).astype(BF16)


def _inproj(x2d, mod3, mod_row_of_tile, p, tm, latent, seq_len):
    n_tok = x2d.shape[0]
    tok = lambda width: pl.BlockSpec((tm, width), lambda i: (i, 0))
    widths = [W_GROUP] * 5 + [3 * W_GROUP, W_GROUP, GATE_LANES]
    out_specs = [tok(w) for w in widths]
    dtypes = [BF16, F32, F32, BF16, BF16, F32, BF16, F32]
    out_shape = [jax.ShapeDtypeStruct((n_tok, w), dt) for w, dt in zip(widths, dtypes)]
    in_specs, args, scratch = [tok(D_MODEL)], [x2d], []
    if latent:
        n_halo_blocks, per_tile, rows = n_tok // HALO, tm // HALO, tm // GRID_W
        tiles_per_seq = seq_len // tm
        in_specs += [pl.BlockSpec((HALO, D_MODEL), lambda i: (jnp.maximum(i * per_tile - 1, 0), 0)),
                     pl.BlockSpec((HALO, D_MODEL), lambda i: (jnp.minimum((i + 1) * per_tile, n_halo_blocks - 1), 0))]
        args += [x2d, x2d]
        col = lambda width: pl.BlockSpec((1, rows, GRID_W * width),
                                         lambda i: (i // tiles_per_seq, i % tiles_per_seq, 0))
        col_shape = lambda width: jax.ShapeDtypeStruct((n_tok // seq_len, seq_len // GRID_W, GRID_W * width), F32)
        for k, width in ((5, 3 * W_GROUP), (7, GATE_LANES)):
            out_specs[k], out_shape[k] = col(width), col_shape(width)
        scratch = [pltpu.VMEM((QKV_TILES, rows * ROW_PITCH, LANE), F32), pltpu.VMEM((rows * ROW_PITCH, GATE_LANES), F32)]
    consts = [p["norm1"], p["lbp"], p["gparams"], p["conv_w"], p["w_main"], p["w_gates"]]
    in_specs += [pl.BlockSpec((1, 6, D_MODEL), lambda i: (mod_row_of_tile(i), 0, 0))] + [_resident(a.shape) for a in consts]
    return pl.pallas_call(
        functools.partial(_inproj_kernel, latent=latent, seq_len=seq_len),
        grid=(n_tok // tm,),
        in_specs=in_specs, out_specs=out_specs, out_shape=out_shape, scratch_shapes=scratch,
        compiler_params=pltpu.CompilerParams(dimension_semantics=("arbitrary",),
                                             vmem_limit_bytes=VMEM_LIMIT),
        name="inproj",
    )(*args, mod3, *consts)


INVERSE_BASE_BLOCK = 8


def _hgrn_stages(qf_ref, ff_ref, vf_ref, qb_ref, fb_ref, vb_ref, of_ref, ob_ref, s_ref, group):
    pre = []
    for b in range(group):
        for d, f_ref in enumerate((ff_ref, fb_ref)):
            pre.append((b, d, _cumsum_rows(_tri3(d), jnp.log(f_ref[b]))))
    yield
    chains = []
    for b, d, cum_all in pre:
        q_ref, f_ref, v_ref, o_ref = (qf_ref, ff_ref, vf_ref, of_ref) if d == 0 else (qb_ref, fb_ref, vb_ref, ob_ref)
        incl, _ = _tri_masks(d)
        for h in range(N_HEADS):
            sl = slice(h * D_HEAD, (h + 1) * D_HEAD)
            k = 1.0 - f_ref[b, :, sl]
            G = cum_all[:, sl]
            g_last_row = G[CHUNK - 1:CHUNK] if d == 0 else G[0:1]
            chains.append(dict(
                b=b, d=d, h=h, sl=sl, o_ref=o_ref, incl=incl, vb=v_ref[b, :, sl].astype(BF16),
                decay=jnp.exp(jnp.broadcast_to(g_last_row, (8, D_HEAD)).T[:, 0:1]),
                q_dec=(q_ref[b, :, sl].astype(F32) * jnp.exp(G)).astype(BF16), k_dec=k * jnp.exp2(G * NEG_LOG2_E),
                k_tail_t=(k * jnp.exp(g_last_row - G)).T.astype(BF16)))
    attns = [jnp.where(ch["incl"], _dot_nt(ch["q_dec"], ch["k_dec"]), 0.0).astype(BF16) for ch in chains]
    yield
    states = [s_ref[ch["b"], ch["d"], ch["h"]] for ch in chains]
    outs = [_dot(jnp.concatenate([ch["q_dec"], attn], axis=1), jnp.concatenate([s.astype(BF16), ch["vb"]], axis=0))
            for ch, attn, s in zip(chains, attns, states)]
    yield
    upds = [_dot(ch["k_tail_t"], ch["vb"]) for ch in chains]
    for ch, o, s, u in zip(chains, outs, states, upds):
        ch["o_ref"][ch["b"], :, ch["sl"]] = o.astype(BF16)
        s_ref[ch["b"], ch["d"], ch["h"]] = ch["decay"] * s + u


PAIR = 2


def _pair_index():
    r = lax.broadcasted_iota(jnp.int32, (CHUNK, PAIR * CHUNK), 0)
    lane = lax.broadcasted_iota(jnp.int32, (CHUNK, PAIR * CHUNK), 1)
    return r, lane % CHUNK, lane // CHUNK


def _block_diag(packed):
    head = lax.broadcasted_iota(jnp.int32, packed.shape, 1) // (packed.shape[1] // PAIR)
    return jnp.concatenate([jnp.where(head == h, packed, jnp.zeros_like(packed)) for h in range(PAIR)], axis=0)


def _unit_triangular_inverses(lowers):
    r, c, _ = _pair_index()
    same_block = lambda block: r // block == c // block
    pdot = lambda a, b: _dot(a, _block_diag(b.astype(BF16)))
    eye = jnp.where(r == c, 1.0, 0.0)
    diag = same_block(INVERSE_BASE_BLOCK)
    ds = [jnp.where(diag, lo, 0.0) for lo in lowers]
    ts = [eye - d for d in ds]
    ps = [pdot(d, d) for d in ds]
    yield
    power = 4
    while power < INVERSE_BASE_BLOCK:
        tps = [pdot(jnp.concatenate([t.astype(BF16), p.astype(BF16)], axis=0), p) for t, p in zip(ts, ps)]
        yield
        ts = [t + tp[:CHUNK] for t, tp in zip(ts, tps)]
        ps = [tp[CHUNK:] for tp in tps]
        power *= 2
    ts = [t + pdot(t, p) for t, p in zip(ts, ps)]
    yield
    block = INVERSE_BASE_BLOCK
    while block < CHUNK:
        off_mask = same_block(2 * block) & jnp.logical_not(same_block(block))
        ws = [pdot(t, jnp.where(off_mask, lo, 0.0)) for t, lo in zip(ts, lowers)]
        yield
        ts = [t - pdot(w, t) for t, w in zip(ts, ws)]
        yield
        block *= 2
    return ts


def _gdn_stages(xf_ref, gf_ref, xb_ref, gb_ref, of_ref, ob_ref, s_ref, group):
    pre = []
    for b in range(group):
        for d, g_ref in enumerate((gf_ref, gb_ref)):
            gates = g_ref[b]
            pre.append((b, d, gates, _cumsum_rows(_tri3(d), gates)))
    yield
    r, c, head = _pair_index()
    lane_head = lax.broadcasted_iota(jnp.int32, (CHUNK, PAIR * D_HEAD), 1) // D_HEAD
    pairs = []
    for b, d, gates, cum in pre:
        x_ref = xf_ref if d == 0 else xb_ref
        incl, strict = (c <= r, c < r) if d == 0 else (c >= r, c > r)
        cum_t = jnp.concatenate([cum] * PAIR, axis=0).T
        for hp in range(N_HEADS // PAIR):
            hs = [PAIR * hp + i for i in range(PAIR)]
            jb, jg = [d * N_HEADS + h for h in hs], [2 * N_HEADS + d * N_HEADS + h for h in hs]
            part = lambda k: x_ref[b, :, (k * N_HEADS + hs[0]) * D_HEAD:(k * N_HEADS + hs[0] + PAIR) * D_HEAD]
            q2, k2, v2 = part(0), part(1), part(2)
            beta2 = jnp.where(lane_head == 0, gates[:, jb[0]:jb[0] + 1], gates[:, jb[1]:jb[1] + 1])
            g_cum = [cum[:, j:j + 1] for j in jg]
            g_col = jnp.where(head == 0, g_cum[0], g_cum[1])
            g_row = jnp.where(head[0:1] == 0, cum_t[jg[0]:jg[0] + 1, :], cum_t[jg[1]:jg[1] + 1, :])
            diff = g_col - g_row
            pairs.append(dict(
                b=b, d=d, hs=hs, q2=q2, k2=k2, v2=v2, beta2=beta2, g_cum=g_cum, strict=strict, k_beta2=k2 * beta2,
                decay_mask=jnp.where(incl, jnp.exp(jnp.where(incl, diff, 0.0)), 0.0)))

    kqs = [_dot_nt(jnp.concatenate([pr["k_beta2"].astype(BF16), pr["q2"].astype(BF16)], axis=0),
                   _block_diag(pr["k2"].astype(BF16))) for pr in pairs]
    yield
    lowers = [jnp.where(pr["strict"], kq[:CHUNK] * pr["decay_mask"], 0.0) for pr, kq in zip(pairs, kqs)]
    ts = yield from _unit_triangular_inverses(lowers)
    for pr, kq in zip(pairs, kqs):
        e_g = [jnp.exp(g) for g in pr["g_cum"]]
        e_g2 = jnp.where(lane_head == 0, e_g[0], e_g[1])
        pr["rhs"] = jnp.concatenate(
            [jnp.concatenate([(pr["v2"] * pr["beta2"])[:, i * D_HEAD:(i + 1) * D_HEAD],
                              (pr["k_beta2"] * e_g2)[:, i * D_HEAD:(i + 1) * D_HEAD]], axis=1) for i in range(PAIR)],
            axis=0).astype(BF16)
        attn2 = (kq[CHUNK:] * pr["decay_mask"]).astype(BF16)
        pr["attn"] = [attn2[:, i * CHUNK:(i + 1) * CHUNK] for i in range(PAIR)]
        g_last = [g[CHUNK - 1:CHUNK] if pr["d"] == 0 else g[0:1] for g in pr["g_cum"]]
        q_dec2 = (pr["q2"] * e_g2).astype(BF16)
        pr["q_dec"] = [q_dec2[:, i * D_HEAD:(i + 1) * D_HEAD] for i in range(PAIR)]
        pr["k_tail_t"] = [(pr["k2"][:, i * D_HEAD:(i + 1) * D_HEAD] * jnp.exp(g_last[i] - pr["g_cum"][i])).T.astype(BF16)
                          for i in range(PAIR)]
        pr["decay"] = [jnp.exp(g) for g in g_last]
    xs = [_dot(_block_diag(t.astype(BF16)), pr["rhs"]) for pr, t in zip(pairs, ts)]
    yield
    chains = []
    for pr, x in zip(pairs, xs):
        for i, h in enumerate(pr["hs"]):
            xi = x[i * CHUNK:(i + 1) * CHUNK]
            chains.append(dict(b=pr["b"], d=pr["d"], h=h, u=xi[:, :D_HEAD], w=xi[:, D_HEAD:].astype(BF16),
                               attn=pr["attn"][i], q_dec=pr["q_dec"][i], k_tail_t=pr["k_tail_t"][i],
                               decay=pr["decay"][i]))

    o_refs = (of_ref, ob_ref)
    states = [s_ref[ch["b"], ch["d"], ch["h"]] for ch in chains]
    sbs = [s.astype(BF16) for s in states]
    wss = [_dot(ch["w"], sb) for ch, sb in zip(chains, sbs)]
    yield
    vbs = [(ch["u"] - ws).astype(BF16) for ch, ws in zip(chains, wss)]
    outs = [_dot(jnp.concatenate([ch["q_dec"], ch["attn"]], axis=1), jnp.concatenate([sb, vb], axis=0))
            for ch, sb, vb in zip(chains, sbs, vbs)]
    yield
    upd = [_dot(ch["k_tail_t"], vb) for ch, vb in zip(chains, vbs)]
    for ch, s, u, o in zip(chains, states, upd, outs):
        o_refs[ch["d"]][ch["b"], :, ch["h"] * D_HEAD:(ch["h"] + 1) * D_HEAD] = o.astype(of_ref.dtype)
        s_ref[ch["b"], ch["d"], ch["h"]] = s * ch["decay"] + u


def _run_interleaved(primary, secondary, every):
    live = [primary, secondary]
    count = 0
    while live:
        gen = primary if primary in live and (secondary not in live or count < every) else secondary
        count = count + 1 if gen is primary else 0
        try:
            next(gen)
        except StopIteration:
            live.remove(gen)


def _scan_kernel(*refs, n_steps, group, hgrn, gdn, has_s0, emit_state):
    it = iter(refs)
    take = lambda k: [next(it) for _ in range(k)]
    h_in = take(6) if hgrn else None
    g_in = take(4) if gdn else None
    s0 = take(int(hgrn) + int(gdn)) if has_s0 else None
    h_out = take(2) if hgrn else None
    g_out = take(2) if gdn else None
    st = take(int(hgrn) + int(gdn)) if emit_state else None
    s_refs = take(int(hgrn) + int(gdn))
    n = pl.program_id(1)

    @pl.when(n == 0)
    def _():
        for k, s_ref in enumerate(s_refs):
            s_ref[...] = s0[k][...] if has_s0 else jnp.zeros_like(s_ref)

    gens = []
    if gdn:
        gens.append(_gdn_stages(*g_in, *g_out, s_refs[-1], group))
    if hgrn:
        gens.append(_hgrn_stages(*h_in, *h_out, s_refs[0], group))
    if len(gens) == 2:
        _run_interleaved(gens[0], gens[1], every=8)
    else:
        for _ in gens[0]:
            pass

    if emit_state:
        @pl.when(n == n_steps - 1)
        def _():
            for st_ref, s_ref in zip(st, s_refs):
                st_ref[...] = s_ref[...]


def _scan(hgrn_args, gdn_args, states0, batch, group, n_steps, gdn_columns, emit_state, gdn_out_dtype):
    state_shape = (group, 2, N_HEADS, D_HEAD, D_HEAD)
    state_spec = pl.BlockSpec(state_shape, lambda b, n: (b, 0, 0, 0, 0))
    in_specs, args, out_specs, out_shape = [], [], [], []
    if hgrn_args is not None:
        qa, f_fwd, f_bwd, va = hgrn_args
        fwd = pl.BlockSpec((group, CHUNK, W_GROUP), lambda b, n: (b, n, 0))
        bwd = pl.BlockSpec((group, CHUNK, W_GROUP), lambda b, n: (b, n_steps - 1 - n, 0))
        in_specs += [fwd, fwd, fwd, bwd, bwd, bwd]
        args += [qa, f_fwd, va, qa, f_bwd, va]
        out_specs += [fwd, bwd]
        out_shape += [jax.ShapeDtypeStruct(qa.shape, BF16)] * 2
    if gdn_args is not None:
        qkv3, gates3 = gdn_args
        columns = gdn_columns
        fwd_map = (lambda b, n: (b, n, 0)) if columns == 1 else (lambda b, n: (b, 0, n))
        bwd_map = lambda b, n: fwd_map(b, n_steps - 1 - n)
        spec = lambda width, imap: pl.BlockSpec((group, CHUNK, width), imap)
        in_specs += [spec(3 * W_GROUP, fwd_map), spec(GATE_LANES, fwd_map),
                     spec(3 * W_GROUP, bwd_map), spec(GATE_LANES, bwd_map)]
        args += [qkv3, gates3, qkv3, gates3]
        out_specs += [spec(W_GROUP, fwd_map), spec(W_GROUP, bwd_map)]
        out_shape += [jax.ShapeDtypeStruct((batch, qkv3.shape[1], columns * W_GROUP), gdn_out_dtype)] * 2
    n_mixers = int(hgrn_args is not None) + int(gdn_args is not None)
    if states0 is not None:
        in_specs, args = in_specs + [state_spec] * n_mixers, args + list(states0)
    if emit_state:
        out_specs = out_specs + [state_spec] * n_mixers
        out_shape = out_shape + [jax.ShapeDtypeStruct((batch,) + state_shape[1:], F32)] * n_mixers
    return pl.pallas_call(
        functools.partial(_scan_kernel, n_steps=n_steps, group=group, hgrn=hgrn_args is not None,
                          gdn=gdn_args is not None, has_s0=states0 is not None, emit_state=emit_state),
        grid=(batch // group, n_steps),
        in_specs=in_specs, out_specs=out_specs, out_shape=out_shape,
        scratch_shapes=[pltpu.VMEM(state_shape, F32)] * n_mixers,
        compiler_params=pltpu.CompilerParams(dimension_semantics=("arbitrary", "arbitrary"),
                                             vmem_limit_bytes=VMEM_LIMIT),
        name="scan",
    )(*args)


FFN_SPLIT = 2
MXU_K_TILE = 256


def _ffn_pieces(d_ff):
    k_tiles = -(-d_ff // MXU_K_TILE)
    bounds = [min(d_ff, MXU_K_TILE * -(-k_tiles * s // FFN_SPLIT)) for s in range(FFN_SPLIT + 1)]
    return list(zip(bounds[:-1], bounds[1:]))


def _tail_kernel(*refs, latent):
    it = iter(refs)
    (x_ref, oaf_ref, oab_ref, obf_ref, obb_ref, ga_ref, zb_ref, mod_ref, na_ref, nb_ref, wo_ref, n2_ref,
     wg_ref, wu_ref, wd_ref, nf_ref) = (next(it) for _ in range(16))
    y_ref = next(it)
    ob_scr = next(it) if latent else None
    tm = x_ref.shape[0]

    def gated_norm(o, w_ref, gate_ref, h):
        sl = slice(h * D_HEAD, (h + 1) * D_HEAD)
        o = o * lax.rsqrt(jnp.mean(o * o, axis=-1, keepdims=True) + EPS)
        return (o * w_ref[:, sl] * gate_ref[:, sl].astype(F32)).astype(BF16)

    def both(f_ref, b_ref, h):
        sl = slice(h * D_HEAD, (h + 1) * D_HEAD)
        return f_ref[:, sl].astype(F32) + b_ref[:, sl].astype(F32)

    if latent:
        n_rows = tm // GRID_W
        for c in range(GRID_W):
            for h in range(N_HEADS):
                sl = slice(c * W_GROUP + h * D_HEAD, c * W_GROUP + (h + 1) * D_HEAD)
                ob_scr[h, pl.ds(c, n_rows, stride=ROW_PITCH), :] = obf_ref[0, :, sl] + obb_ref[0, :, sl]
        o_b = [jnp.concatenate([ob_scr[h, r * ROW_PITCH:r * ROW_PITCH + GRID_W, :] for r in range(n_rows)], axis=0)
               for h in range(N_HEADS)]
    else:
        o_b = [both(obf_ref, obb_ref, h) for h in range(N_HEADS)]
    o_a = [both(oaf_ref, oab_ref, h) for h in range(N_HEADS)]
    mixed = jnp.concatenate([gated_norm(o, na_ref, ga_ref, h) for h, o in enumerate(o_a)]
                            + [gated_norm(o, nb_ref, zb_ref, h) for h, o in enumerate(o_b)], axis=-1)
    m = mod_ref[0]
    x1 = x_ref[...] + m[2:3] * jnp.dot(mixed, wo_ref[...], preferred_element_type=F32)
    y = x1 * lax.rsqrt(jnp.mean(x1 * x1, axis=-1, keepdims=True) + EPS) * n2_ref[...]
    h2 = (y * (1.0 + m[4:5]) + m[3:4]).astype(BF16)

    ff = None
    for lo, hi in _ffn_pieces(wg_ref.shape[1]):
        cols = slice(lo, hi)
        gate = jnp.dot(h2, wg_ref[:, cols], preferred_element_type=F32)
        up = jnp.dot(h2, wu_ref[:, cols], preferred_element_type=F32)
        part = jnp.dot((_silu(gate) * up).astype(BF16), wd_ref[cols, :], preferred_element_type=F32)
        ff = part if ff is None else ff + part
    x2 = x1 + m[5:6] * ff
    y_ref[...] = x2 * lax.rsqrt(jnp.mean(x2 * x2, axis=-1, keepdims=True) + EPS) * nf_ref[...]


def _tail(x2d, oaf, oab, obf, obb, ga, zb, mod3, mod_row_of_tile, p, tm, latent, seq_len):
    n_tok = x2d.shape[0]
    assert p["w_gate"].shape[1] % LANE == 0
    tok = lambda width: pl.BlockSpec((tm, width), lambda i: (i, 0))
    ob_spec, scratch = tok(W_GROUP), []
    if latent:
        tiles_per_seq = seq_len // tm
        ob_spec = pl.BlockSpec((1, tm // GRID_W, GRID_W * W_GROUP), lambda i: (i // tiles_per_seq, i % tiles_per_seq, 0))
        scratch = [pltpu.VMEM((N_HEADS, tm // GRID_W * ROW_PITCH, D_HEAD), F32)]
    consts = [p["norm_a"], p["norm_b"], p["w_out"], p["norm2"], p["w_gate"], p["w_up"], p["w_down"], p["norm_f"]]
    return pl.pallas_call(
        functools.partial(_tail_kernel, latent=latent),
        grid=(n_tok // tm,),
        in_specs=[tok(D_MODEL), tok(W_GROUP), tok(W_GROUP), ob_spec, ob_spec, tok(W_GROUP), tok(W_GROUP),
                  pl.BlockSpec((1, 6, D_MODEL), lambda i: (mod_row_of_tile(i), 0, 0))]
                 + [_resident(a.shape) for a in consts],
        out_specs=tok(D_MODEL),
        out_shape=jax.ShapeDtypeStruct((n_tok, D_MODEL), F32),
        scratch_shapes=scratch,
        compiler_params=pltpu.CompilerParams(dimension_semantics=("arbitrary",),
                                             vmem_limit_bytes=VMEM_LIMIT),
        name="tail",
    )(x2d, oaf, oab, obf, obb, ga, zb, mod3, *consts)


def _stream(x, mod3, mod_row_of_tile, s0_a, s0_b, p, latent):
    batch, seq, _ = x.shape
    x2d = x.reshape(batch * seq, D_MODEL)
    tm = 512
    mod_row = functools.partial(mod_row_of_tile, tm=tm)
    qa, f_fwd, f_bwd, va, ga, qkv, zb, gates = _inproj(x2d, mod3, mod_row, p, tm, latent, seq)
    hgrn_args = [a.reshape(batch, seq, W_GROUP) for a in (qa, f_fwd, f_bwd, va)]
    n_chunks = seq // CHUNK
    if latent:
        assert seq // GRID_W == CHUNK and n_chunks == GRID_W
        oaf, oab, obf, obb = _scan(hgrn_args, (qkv, gates), (s0_a, s0_b), batch, group=4, n_steps=n_chunks,
                                   gdn_columns=GRID_W, emit_state=False, gdn_out_dtype=F32)
        states = (None, None)
    else:
        gdn_args = (qkv.reshape(batch, seq, 3 * W_GROUP), gates.reshape(batch, seq, GATE_LANES))
        oaf, oab, obf, obb, new_a, new_b = _scan(hgrn_args, gdn_args, None, batch, group=4, n_steps=n_chunks,
                                                 gdn_columns=1, emit_state=True, gdn_out_dtype=BF16)
        obf, obb = (o.reshape(batch * seq, W_GROUP) for o in (obf, obb))
        states = (new_a, new_b)
    oaf, oab = (o.reshape(batch * seq, W_GROUP) for o in (oaf, oab))
    y = _tail(x2d, oaf, oab, obf, obb, ga, zb, mod3, mod_row, p, tm, latent, seq)
    return y.reshape(batch, seq, D_MODEL), states


def kernel(x_prompt, x_sample, c, state_hgrn, state_gdn, c_ctx, w_ada, b_ada, norm1, norm2, w_in, conv_w,
           hgrn_lb, gdn_A_log, gdn_dt_bias, hgrn_out_norm, gdn_out_norm, w_out, w_gate, w_up, w_down, norm_f):
    depth = w_in.shape[0]
    assert depth == 1 and hgrn_lb.shape[0] == 2
    dec_batch, dec_seq, _ = x_sample.shape
    l = 0

    n_main = N_MAIN_GROUPS * W_GROUP
    pad8 = jnp.zeros((1, 2 * N_HEADS), F32)
    gparams = jnp.concatenate(
        [jnp.concatenate([pad8, a.reshape(1, 2 * N_HEADS).astype(F32),
                          jnp.zeros((1, GATE_LANES - 4 * N_HEADS), F32)], axis=1)
         for a in (gdn_A_log[l], gdn_dt_bias[l])], axis=0)
    p = {
        "norm1": norm1[l].reshape(1, D_MODEL), "norm2": norm2[l].reshape(1, D_MODEL),
        "lbp": hgrn_lb.reshape(2, 2 * W_GROUP), "gparams": gparams,
        "w_main": w_in[l].astype(BF16),
        "w_gates": jnp.pad(w_in[l][:, n_main:], ((0, 0), (0, GATE_LANES - 4 * N_HEADS))).astype(BF16),
        "conv_w": conv_w[l],
        "norm_a": hgrn_out_norm[l].reshape(1, W_GROUP), "norm_b": gdn_out_norm[l].reshape(1, W_GROUP),
        "w_out": w_out[l].astype(BF16), "w_gate": w_gate[l].astype(BF16), "w_up": w_up[l].astype(BF16),
        "w_down": w_down[l].astype(BF16), "norm_f": norm_f.reshape(1, D_MODEL),
    }

    n_mod_rows = 8
    cvec = jnp.concatenate([c_ctx[None], c, jnp.zeros((n_mod_rows - 1 - dec_batch, D_MODEL), F32)], axis=0)
    mod3 = _modulation(cvec, w_ada[l], b_ada[l]).reshape(n_mod_rows, 6, D_MODEL)

    y_prompt, (new_a, new_b) = _stream(x_prompt, mod3, lambda i, tm: 0, None, None, p, latent=False)
    y_sample, _ = _stream(x_sample, mod3, lambda i, tm: 1 + i // (dec_seq // tm), state_hgrn[:, l],
                          state_gdn[:, l], p, latent=True)
    return y_prompt, y_sample, new_a[:, None], new_b[:, None]
```

```python
import functools

import jax
import jax.numpy as jnp
from jax import lax
from jax.experimental import pallas as pl
from jax.experimental.pallas import tpu as pltpu

F32 = jnp.float32
BF16 = jnp.bfloat16

D_MODEL = 1024
N_HEADS = 4
D_HEAD = 128
W_GROUP = N_HEADS * D_HEAD
CHUNK = 64
GRID_W = 64
CONV_W = 3
EPS = 1e-6
N_MAIN_GROUPS = 9
GATE_LANES = 128
VMEM_LIMIT = 56 * 1024 * 1024


NEG_LOG2_E = -1.4426950408889634


def _sigmoid(x):
    return 1.0 / (1.0 + jnp.exp2(x * NEG_LOG2_E))


def _silu(x):
    return x * _sigmoid(x)


def _dot(a, b):
    return jnp.dot(a.astype(BF16), b.astype(BF16), preferred_element_type=F32)


def _dot_nt(a, b):
    return lax.dot_general(a.astype(BF16), b.astype(BF16), (((1,), (1,)), ((), ())),
                           preferred_element_type=F32)


def _split3(x):
    x1 = x.astype(BF16)
    r = x - x1.astype(F32)
    x2 = r.astype(BF16)
    x3 = (r - x2.astype(F32)).astype(BF16)
    return x1, x2, x3


def _cumsum_rows(tri3, x):
    return jnp.dot(tri3, jnp.concatenate(_split3(x), axis=0), preferred_element_type=F32)


def _tri3(direction):
    tri = _tri_masks(direction)[0].astype(BF16)
    return jnp.concatenate([tri, tri, tri], axis=1)


def _tri_masks(direction):
    r = lax.broadcasted_iota(jnp.int32, (CHUNK, CHUNK), 0)
    c = lax.broadcasted_iota(jnp.int32, (CHUNK, CHUNK), 1)
    if direction == 0:
        return c <= r, c < r
    return c >= r, c > r


def _resident(shape):
    return pl.BlockSpec(shape, lambda i: (0,) * len(shape), pipeline_mode=pl.Buffered(1))


def _mod_kernel(c_ref, w_ref, b_ref, o_ref):
    s = _silu(c_ref[...])
    o_ref[...] = _dot(s, w_ref[...]) + b_ref[...]


def _modulation(cvec, w_ada, b_ada):
    n_rows, d = cvec.shape
    n_out = w_ada.shape[1]
    tn = 1536
    return pl.pallas_call(
        _mod_kernel,
        grid=(n_out // tn,),
        in_specs=[pl.BlockSpec((n_rows, d), lambda j: (0, 0)),
                  pl.BlockSpec((d, tn), lambda j: (0, j)),
                  pl.BlockSpec((1, tn), lambda j: (0, j))],
        out_specs=pl.BlockSpec((n_rows, tn), lambda j: (0, j)),
        out_shape=jax.ShapeDtypeStruct((n_rows, n_out), F32),
        compiler_params=pltpu.CompilerParams(dimension_semantics=("arbitrary",),
                                             vmem_limit_bytes=VMEM_LIMIT),
        name="mod",
    )(cvec, w_ada, b_ada.reshape(1, n_out))


HALO = GRID_W
LANE = 128
QKV_TILES = 3 * W_GROUP // LANE
ROW_PITCH = GRID_W + 8


def _inproj_kernel(*refs, latent, seq_len):
    it = iter(refs)
    x_ref = next(it)
    xp_ref, xn_ref = (next(it), next(it)) if latent else (None, None)
    mod_ref, n1_ref, lb_ref, gp_ref, cw_ref, w_ref, wg_ref = (next(it) for _ in range(7))
    qa_ref, ff_ref, fb_ref, va_ref, ga_ref, qkv_ref, zb_ref, gates_ref = (next(it) for _ in range(8))
    qkv_scr, gates_scr = (next(it), next(it)) if latent else (None, None)
    tm = x_ref.shape[0]
    m = mod_ref[0]

    norm_scale = n1_ref[...] * (1.0 + m[1:2])

    def normed(x):
        return (x * lax.rsqrt(jnp.mean(x * x, axis=-1, keepdims=True) + EPS) * norm_scale + m[0:1]).astype(BF16)

    hb = normed(x_ref[...])

    w_qkv = w_ref[:, 5 * W_GROUP:8 * W_GROUP]
    if latent:
        i = pl.program_id(0)
        tiles_per_seq = seq_len // tm
        ext = jnp.dot(jnp.concatenate([normed(xp_ref[...]), hb, normed(xn_ref[...])], axis=0), w_qkv,
                      preferred_element_type=F32)
        above = jnp.where(i % tiles_per_seq == 0, 0.0, ext[:HALO])
        below = jnp.where(i % tiles_per_seq == tiles_per_seq - 1, 0.0, ext[HALO + tm:])
        cur = ext[HALO:HALO + tm]
        prev = jnp.concatenate([above, ext[HALO:tm]], axis=0)
        nxt = jnp.concatenate([ext[2 * HALO:HALO + tm], below], axis=0)
    else:
        cur = jnp.dot(hb, w_qkv, preferred_element_type=F32)
        pos = lax.broadcasted_iota(jnp.int32, cur.shape, 0) % seq_len
        prev = jnp.where(pos == 0, 0.0, pltpu.roll(cur, 1, axis=0))
        nxt = jnp.where(pos == seq_len - 1, 0.0, pltpu.roll(cur, tm - 1, axis=0))
    cw = cw_ref[...]
    y = _silu(prev * cw[0:1] + cur * cw[1:2] + nxt * cw[2:3])
    tiles = []
    for j in range(QKV_TILES):
        t = y[:, j * LANE:(j + 1) * LANE]
        if j < 2 * N_HEADS:
            inv = lax.rsqrt(jnp.sum(t * t, axis=-1, keepdims=True) + EPS)
            t = t * (inv * (D_HEAD ** -0.5) if j < N_HEADS else inv)
        tiles.append(t)

    raw = jnp.dot(hb, wg_ref[...], preferred_element_type=F32)
    gp = gp_ref[...]
    z = raw + gp[1:2]
    softplus = jnp.maximum(z, 0.0) + jnp.log(1.0 + jnp.exp(-jnp.abs(z)))
    lane = lax.broadcasted_iota(jnp.int32, raw.shape, 1)
    gates = jnp.where(lane < 2 * N_HEADS, _sigmoid(raw),
                      jnp.where(lane < 4 * N_HEADS, -jnp.exp(gp[0:1]) * softplus, 0.0))

    if not latent:
        for j, t in enumerate(tiles):
            qkv_ref[:, j * LANE:(j + 1) * LANE] = t
        gates_ref[...] = gates
    else:
        n_rows = tm // GRID_W
        for r in range(n_rows):
            rows, dst = slice(r * GRID_W, (r + 1) * GRID_W), slice(r * ROW_PITCH, r * ROW_PITCH + GRID_W)
            for j, t in enumerate(tiles):
                qkv_scr[j, dst, :] = t[rows]
            gates_scr[dst, :] = gates[rows]
        for c in range(GRID_W):
            for j in range(QKV_TILES):
                qkv_ref[0, :, (c * QKV_TILES + j) * LANE:(c * QKV_TILES + j + 1) * LANE] = (
                    qkv_scr[j, pl.ds(c, n_rows, stride=ROW_PITCH), :])
            gates_ref[0, :, c * GATE_LANES:(c + 1) * GATE_LANES] = gates_scr[pl.ds(c, n_rows, stride=ROW_PITCH), :]

    def proj(j):
        return jnp.dot(hb, w_ref[:, j * W_GROUP:(j + 1) * W_GROUP], preferred_element_type=F32)

    lbp = lb_ref[...]
    e = jnp.exp(lbp - jnp.max(lbp, axis=0, keepdims=True))
    lb = e[0:1] / jnp.sum(e, axis=0, keepdims=True)
    lb_f, lb_b = lb[:, :W_GROUP], lb[:, W_GROUP:]

    qa_ref[...] = _silu(proj(0)).astype(BF16)
    ff_ref[...] = lb_f + (1.0 - lb_f) * _sigmoid(proj(1))
    fb_ref[...] = lb_b + (1.0 - lb_b) * _sigmoid(proj(2))
    va_ref[...] = proj(3).astype(BF16)
    ga_ref[...] = _silu(proj(4)).astype(BF16)
    zb_ref[...] = _silu(proj(8)).astype(BF16)


def _inproj(x2d, mod3, mod_row_of_tile, p, tm, latent, seq_len):
    n_tok = x2d.shape[0]
    tok = lambda width: pl.BlockSpec((tm, width), lambda i: (i, 0))
    widths = [W_GROUP] * 5 + [3 * W_GROUP, W_GROUP, GATE_LANES]
    out_specs = [tok(w) for w in widths]
    dtypes = [BF16, F32, F32, BF16, BF16, F32, BF16, F32]
    out_shape = [jax.ShapeDtypeStruct((n_tok, w), dt) for w, dt in zip(widths, dtypes)]
    in_specs, args, scratch = [tok(D_MODEL)], [x2d], []
    if latent:
        n_halo_blocks, per_tile, rows = n_tok // HALO, tm // HALO, tm // GRID_W
        tiles_per_seq = seq_len // tm
        in_specs += [pl.BlockSpec((HALO, D_MODEL), lambda i: (jnp.maximum(i * per_tile - 1, 0), 0)),
                     pl.BlockSpec((HALO, D_MODEL), lambda i: (jnp.minimum((i + 1) * per_tile, n_halo_blocks - 1), 0))]
        args += [x2d, x2d]
        col = lambda width: pl.BlockSpec((1, rows, GRID_W * width),
                                         lambda i: (i // tiles_per_seq, i % tiles_per_seq, 0))
        col_shape = lambda width: jax.ShapeDtypeStruct((n_tok // seq_len, seq_len // GRID_W, GRID_W * width), F32)
        for k, width in ((5, 3 * W_GROUP), (7, GATE_LANES)):
            out_specs[k], out_shape[k] = col(width), col_shape(width)
        scratch = [pltpu.VMEM((QKV_TILES, rows * ROW_PITCH, LANE), F32), pltpu.VMEM((rows * ROW_PITCH, GATE_LANES), F32)]
    consts = [p["norm1"], p["lbp"], p["gparams"], p["conv_w"], p["w_main"], p["w_gates"]]
    in_specs += [pl.BlockSpec((1, 6, D_MODEL), lambda i: (mod_row_of_tile(i), 0, 0))] + [_resident(a.shape) for a in consts]
    return pl.pallas_call(
        functools.partial(_inproj_kernel, latent=latent, seq_len=seq_len),
        grid=(n_tok // tm,),
        in_specs=in_specs, out_specs=out_specs, out_shape=out_shape, scratch_shapes=scratch,
        compiler_params=pltpu.CompilerParams(dimension_semantics=("arbitrary",),
                                             vmem_limit_bytes=VMEM_LIMIT),
        name="inproj",
    )(*args, mod3, *consts)


INVERSE_BASE_BLOCK = 8


def _hgrn_stages(qf_ref, ff_ref, vf_ref, qb_ref, fb_ref, vb_ref, of_ref, ob_ref, s_ref, group):
    pre = []
    for b in range(group):
        for d, f_ref in enumerate((ff_ref, fb_ref)):
            pre.append((b, d, _cumsum_rows(_tri3(d), jnp.log(f_ref[b]))))
    yield
    chains = []
    for b, d, cum_all in pre:
        q_ref, f_ref, v_ref, o_ref = (qf_ref, ff_ref, vf_ref, of_ref) if d == 0 else (qb_ref, fb_ref, vb_ref, ob_ref)
        incl, _ = _tri_masks(d)
        for h in range(N_HEADS):
            sl = slice(h * D_HEAD, (h + 1) * D_HEAD)
            k = 1.0 - f_ref[b, :, sl]
            G = cum_all[:, sl]
            g_last_row = G[CHUNK - 1:CHUNK] if d == 0 else G[0:1]
            chains.append(dict(
                b=b, d=d, h=h, sl=sl, o_ref=o_ref, incl=incl, vb=v_ref[b, :, sl].astype(BF16),
                decay=jnp.exp(jnp.broadcast_to(g_last_row, (8, D_HEAD)).T[:, 0:1]),
                q_dec=(q_ref[b, :, sl].astype(F32) * jnp.exp(G)).astype(BF16), k_dec=k * jnp.exp2(G * NEG_LOG2_E),
                k_tail_t=(k * jnp.exp(g_last_row - G)).T.astype(BF16)))
    attns = [jnp.where(ch["incl"], _dot_nt(ch["q_dec"], ch["k_dec"]), 0.0).astype(BF16) for ch in chains]
    yield
    states = [s_ref[ch["b"], ch["d"], ch["h"]] for ch in chains]
    outs = [_dot(jnp.concatenate([ch["q_dec"], attn], axis=1), jnp.concatenate([s.astype(BF16), ch["vb"]], axis=0))
            for ch, attn, s in zip(chains, attns, states)]
    yield
    upds = [_dot(ch["k_tail_t"], ch["vb"]) for ch in chains]
    for ch, o, s, u in zip(chains, outs, states, upds):
        ch["o_ref"][ch["b"], :, ch["sl"]] = o.astype(BF16)
        s_ref[ch["b"], ch["d"], ch["h"]] = ch["decay"] * s + u


PAIR = 2


def _pair_index():
    r = lax.broadcasted_iota(jnp.int32, (CHUNK, PAIR * CHUNK), 0)
    lane = lax.broadcasted_iota(jnp.int32, (CHUNK, PAIR * CHUNK), 1)
    return r, lane % CHUNK, lane // CHUNK


def _block_diag(packed):
    head = lax.broadcasted_iota(jnp.int32, packed.shape, 1) // (packed.shape[1] // PAIR)
    return jnp.concatenate([jnp.where(head == h, packed, jnp.zeros_like(packed)) for h in range(PAIR)], axis=0)


def _unit_triangular_inverses(lowers):
    r, c, _ = _pair_index()
    same_block = lambda block: r // block == c // block
    pdot = lambda a, b: _dot(a, _block_diag(b.astype(BF16)))
    eye = jnp.where(r == c, 1.0, 0.0)
    diag = same_block(INVERSE_BASE_BLOCK)
    ds = [jnp.where(diag, lo, 0.0) for lo in lowers]
    ts = [eye - d for d in ds]
    ps = [pdot(d, d) for d in ds]
    yield
    power = 4
    while power < INVERSE_BASE_BLOCK:
        tps = [pdot(jnp.concatenate([t.astype(BF16), p.astype(BF16)], axis=0), p) for t, p in zip(ts, ps)]
        yield
        ts = [t + tp[:CHUNK] for t, tp in zip(ts, tps)]
        ps = [tp[CHUNK:] for tp in tps]
        power *= 2
    ts = [t + pdot(t, p) for t, p in zip(ts, ps)]
    yield
    block = INVERSE_BASE_BLOCK
    while block < CHUNK:
        off_mask = same_block(2 * block) & jnp.logical_not(same_block(block))
        ws = [pdot(t, jnp.where(off_mask, lo, 0.0)) for t, lo in zip(ts, lowers)]
        yield
        ts = [t - pdot(w, t) for t, w in zip(ts, ws)]
        yield
        block *= 2
    return ts


def _gdn_stages(xf_ref, gf_ref, xb_ref, gb_ref, of_ref, ob_ref, s_ref, group):
    pre = []
    for b in range(group):
        for d, g_ref in enumerate((gf_ref, gb_ref)):
            gates = g_ref[b]
            pre.append((b, d, gates, _cumsum_rows(_tri3(d), gates)))
    yield
    r, c, head = _pair_index()
    lane_head = lax.broadcasted_iota(jnp.int32, (CHUNK, PAIR * D_HEAD), 1) // D_HEAD
    pairs = []
    for b, d, gates, cum in pre:
        x_ref = xf_ref if d == 0 else xb_ref
        incl, strict = (c <= r, c < r) if d == 0 else (c >= r, c > r)
        cum_t = jnp.concatenate([cum] * PAIR, axis=0).T
        for hp in range(N_HEADS // PAIR):
            hs = [PAIR * hp + i for i in range(PAIR)]
            jb, jg = [d * N_HEADS + h for h in hs], [2 * N_HEADS + d * N_HEADS + h for h in hs]
            part = lambda k: x_ref[b, :, (k * N_HEADS + hs[0]) * D_HEAD:(k * N_HEADS + hs[0] + PAIR) * D_HEAD]
            q2, k2, v2 = part(0), part(1), part(2)
            beta2 = jnp.where(lane_head == 0, gates[:, jb[0]:jb[0] + 1], gates[:, jb[1]:jb[1] + 1])
            g_cum = [cum[:, j:j + 1] for j in jg]
            g_col = jnp.where(head == 0, g_cum[0], g_cum[1])
            g_row = jnp.where(head[0:1] == 0, cum_t[jg[0]:jg[0] + 1, :], cum_t[jg[1]:jg[1] + 1, :])
            diff = g_col - g_row
            pairs.append(dict(
                b=b, d=d, hs=hs, q2=q2, k2=k2, v2=v2, beta2=beta2, g_cum=g_cum, strict=strict, k_beta2=k2 * beta2,
                decay_mask=jnp.where(incl, jnp.exp(jnp.where(incl, diff, 0.0)), 0.0)))

    kqs = [_dot_nt(jnp.concatenate([pr["k_beta2"].astype(BF16), pr["q2"].astype(BF16)], axis=0),
                   _block_diag(pr["k2"].astype(BF16))) for pr in pairs]
    yield
    lowers = [jnp.where(pr["strict"], kq[:CHUNK] * pr["decay_mask"], 0.0) for pr, kq in zip(pairs, kqs)]
    ts = yield from _unit_triangular_inverses(lowers)
    for pr, kq in zip(pairs, kqs):
        e_g = [jnp.exp(g) for g in pr["g_cum"]]
        e_g2 = jnp.where(lane_head == 0, e_g[0], e_g[1])
        pr["rhs"] = jnp.concatenate(
            [jnp.concatenate([(pr["v2"] * pr["beta2"])[:, i * D_HEAD:(i + 1) * D_HEAD],
                              (pr["k_beta2"] * e_g2)[:, i * D_HEAD:(i + 1) * D_HEAD]], axis=1) for i in range(PAIR)],
            axis=0).astype(BF16)
        attn2 = (kq[CHUNK:] * pr["decay_mask"]).astype(BF16)
        pr["attn"] = [attn2[:, i * CHUNK:(i + 1) * CHUNK] for i in range(PAIR)]
        g_last = [g[CHUNK - 1:CHUNK] if pr["d"] == 0 else g[0:1] for g in pr["g_cum"]]
        q_dec2 = (pr["q2"] * e_g2).astype(BF16)
        pr["q_dec"] = [q_dec2[:, i * D_HEAD:(i + 1) * D_HEAD] for i in range(PAIR)]
        pr["k_tail_t"] = [(pr["k2"][:, i * D_HEAD:(i + 1) * D_HEAD] * jnp.exp(g_last[i] - pr["g_cum"][i])).T.astype(BF16)
                          for i in range(PAIR)]
        pr["decay"] = [jnp.exp(g) for g in g_last]
    xs = [_dot(_block_diag(t.astype(BF16)), pr["rhs"]) for pr, t in zip(pairs, ts)]
    yield
    chains = []
    for pr, x in zip(pairs, xs):
        for i, h in enumerate(pr["hs"]):
            xi = x[i * CHUNK:(i + 1) * CHUNK]
            chains.append(dict(b=pr["b"], d=pr["d"], h=h, u=xi[:, :D_HEAD], w=xi[:, D_HEAD:].astype(BF16),
                               attn=pr["attn"][i], q_dec=pr["q_dec"][i], k_tail_t=pr["k_tail_t"][i],
                               decay=pr["decay"][i]))

    o_refs = (of_ref, ob_ref)
    states = [s_ref[ch["b"], ch["d"], ch["h"]] for ch in chains]
    sbs = [s.astype(BF16) for s in states]
    wqs = [_dot(jnp.concatenate([ch["w"], ch["q_dec"]], axis=0), sb) for ch, sb in zip(chains, sbs)]
    yield
    vbs = [(ch["u"] - wq[:CHUNK]).astype(BF16) for ch, wq in zip(chains, wqs)]
    outs = [wq[CHUNK:] + _dot(ch["attn"], vb) for ch, wq, vb in zip(chains, wqs, vbs)]
    yield
    upd = [_dot(ch["k_tail_t"], vb) for ch, vb in zip(chains, vbs)]
    for ch, s, u, o in zip(chains, states, upd, outs):
        o_refs[ch["d"]][ch["b"], :, ch["h"] * D_HEAD:(ch["h"] + 1) * D_HEAD] = o.astype(of_ref.dtype)
        s_ref[ch["b"], ch["d"], ch["h"]] = s * ch["decay"] + u


def _run_interleaved(primary, secondary, every):
    live = [primary, secondary]
    count = 0
    while live:
        gen = primary if primary in live and (secondary not in live or count < every) else secondary
        count = count + 1 if gen is primary else 0
        try:
            next(gen)
        except StopIteration:
            live.remove(gen)


def _scan_kernel(*refs, n_steps, group, hgrn, gdn, has_s0, emit_state):
    it = iter(refs)
    take = lambda k: [next(it) for _ in range(k)]
    h_in = take(6) if hgrn else None
    g_in = take(4) if gdn else None
    s0 = take(int(hgrn) + int(gdn)) if has_s0 else None
    h_out = take(2) if hgrn else None
    g_out = take(2) if gdn else None
    st = take(int(hgrn) + int(gdn)) if emit_state else None
    s_refs = take(int(hgrn) + int(gdn))
    n = pl.program_id(1)

    @pl.when(n == 0)
    def _():
        for k, s_ref in enumerate(s_refs):
            s_ref[...] = s0[k][...] if has_s0 else jnp.zeros_like(s_ref)

    gens = []
    if gdn:
        gens.append(_gdn_stages(*g_in, *g_out, s_refs[-1], group))
    if hgrn:
        gens.append(_hgrn_stages(*h_in, *h_out, s_refs[0], group))
    if len(gens) == 2:
        _run_interleaved(gens[0], gens[1], every=8)
    else:
        for _ in gens[0]:
            pass

    if emit_state:
        @pl.when(n == n_steps - 1)
        def _():
            for st_ref, s_ref in zip(st, s_refs):
                st_ref[...] = s_ref[...]


def _scan(hgrn_args, gdn_args, states0, batch, group, n_steps, gdn_columns, emit_state, gdn_out_dtype):
    state_shape = (group, 2, N_HEADS, D_HEAD, D_HEAD)
    state_spec = pl.BlockSpec(state_shape, lambda b, n: (b, 0, 0, 0, 0))
    in_specs, args, out_specs, out_shape = [], [], [], []
    if hgrn_args is not None:
        qa, f_fwd, f_bwd, va = hgrn_args
        fwd = pl.BlockSpec((group, CHUNK, W_GROUP), lambda b, n: (b, n, 0))
        bwd = pl.BlockSpec((group, CHUNK, W_GROUP), lambda b, n: (b, n_steps - 1 - n, 0))
        in_specs += [fwd, fwd, fwd, bwd, bwd, bwd]
        args += [qa, f_fwd, va, qa, f_bwd, va]
        out_specs += [fwd, bwd]
        out_shape += [jax.ShapeDtypeStruct(qa.shape, BF16)] * 2
    if gdn_args is not None:
        qkv3, gates3 = gdn_args
        columns = gdn_columns
        fwd_map = (lambda b, n: (b, n, 0)) if columns == 1 else (lambda b, n: (b, 0, n))
        bwd_map = lambda b, n: fwd_map(b, n_steps - 1 - n)
        spec = lambda width, imap: pl.BlockSpec((group, CHUNK, width), imap)
        in_specs += [spec(3 * W_GROUP, fwd_map), spec(GATE_LANES, fwd_map),
                     spec(3 * W_GROUP, bwd_map), spec(GATE_LANES, bwd_map)]
        args += [qkv3, gates3, qkv3, gates3]
        out_specs += [spec(W_GROUP, fwd_map), spec(W_GROUP, bwd_map)]
        out_shape += [jax.ShapeDtypeStruct((batch, qkv3.shape[1], columns * W_GROUP), gdn_out_dtype)] * 2
    n_mixers = int(hgrn_args is not None) + int(gdn_args is not None)
    if states0 is not None:
        in_specs, args = in_specs + [state_spec] * n_mixers, args + list(states0)
    if emit_state:
        out_specs = out_specs + [state_spec] * n_mixers
        out_shape = out_shape + [jax.ShapeDtypeStruct((batch,) + state_shape[1:], F32)] * n_mixers
    return pl.pallas_call(
        functools.partial(_scan_kernel, n_steps=n_steps, group=group, hgrn=hgrn_args is not None,
                          gdn=gdn_args is not None, has_s0=states0 is not None, emit_state=emit_state),
        grid=(batch // group, n_steps),
        in_specs=in_specs, out_specs=out_specs, out_shape=out_shape,
        scratch_shapes=[pltpu.VMEM(state_shape, F32)] * n_mixers,
        compiler_params=pltpu.CompilerParams(dimension_semantics=("arbitrary", "arbitrary"),
                                             vmem_limit_bytes=VMEM_LIMIT),
        name="scan",
    )(*args)


FFN_SPLIT = 2
MXU_K_TILE = 256


def _ffn_pieces(d_ff):
    k_tiles = -(-d_ff // MXU_K_TILE)
    bounds = [min(d_ff, MXU_K_TILE * -(-k_tiles * s // FFN_SPLIT)) for s in range(FFN_SPLIT + 1)]
    return list(zip(bounds[:-1], bounds[1:]))


def _tail_kernel(*refs, latent):
    it = iter(refs)
    (x_ref, oaf_ref, oab_ref, obf_ref, obb_ref, ga_ref, zb_ref, mod_ref, na_ref, nb_ref, wo_ref, n2_ref,
     wg_ref, wu_ref, wd_ref, nf_ref) = (next(it) for _ in range(16))
    y_ref = next(it)
    ob_scr = next(it) if latent else None
    tm = x_ref.shape[0]

    def gated_norm(o, w_ref, gate_ref, h):
        sl = slice(h * D_HEAD, (h + 1) * D_HEAD)
        o = o * lax.rsqrt(jnp.mean(o * o, axis=-1, keepdims=True) + EPS)
        return (o * w_ref[:, sl] * gate_ref[:, sl].astype(F32)).astype(BF16)

    def both(f_ref, b_ref, h):
        sl = slice(h * D_HEAD, (h + 1) * D_HEAD)
        return f_ref[:, sl].astype(F32) + b_ref[:, sl].astype(F32)

    if latent:
        n_rows = tm // GRID_W
        for c in range(GRID_W):
            for h in range(N_HEADS):
                sl = slice(c * W_GROUP + h * D_HEAD, c * W_GROUP + (h + 1) * D_HEAD)
                ob_scr[h, pl.ds(c, n_rows, stride=ROW_PITCH), :] = obf_ref[0, :, sl] + obb_ref[0, :, sl]
        o_b = [jnp.concatenate([ob_scr[h, r * ROW_PITCH:r * ROW_PITCH + GRID_W, :] for r in range(n_rows)], axis=0)
               for h in range(N_HEADS)]
    else:
        o_b = [both(obf_ref, obb_ref, h) for h in range(N_HEADS)]
    o_a = [both(oaf_ref, oab_ref, h) for h in range(N_HEADS)]
    mixed = jnp.concatenate([gated_norm(o, na_ref, ga_ref, h) for h, o in enumerate(o_a)]
                            + [gated_norm(o, nb_ref, zb_ref, h) for h, o in enumerate(o_b)], axis=-1)
    m = mod_ref[0]
    x1 = x_ref[...] + m[2:3] * jnp.dot(mixed, wo_ref[...], preferred_element_type=F32)
    y = x1 * lax.rsqrt(jnp.mean(x1 * x1, axis=-1, keepdims=True) + EPS) * n2_ref[...]
    h2 = (y * (1.0 + m[4:5]) + m[3:4]).astype(BF16)

    ff = None
    for lo, hi in _ffn_pieces(wg_ref.shape[1]):
        cols = slice(lo, hi)
        gate = jnp.dot(h2, wg_ref[:, cols], preferred_element_type=F32)
        up = jnp.dot(h2, wu_ref[:, cols], preferred_element_type=F32)
        part = jnp.dot((_silu(gate) * up).astype(BF16), wd_ref[cols, :], preferred_element_type=F32)
        ff = part if ff is None else ff + part
    x2 = x1 + m[5:6] * ff
    y_ref[...] = x2 * lax.rsqrt(jnp.mean(x2 * x2, axis=-1, keepdims=True) + EPS) * nf_ref[...]


def _tail(x2d, oaf, oab, obf, obb, ga, zb, mod3, mod_row_of_tile, p, tm, latent, seq_len):
    n_tok = x2d.shape[0]
    assert p["w_gate"].shape[1] % LANE == 0
    tok = lambda width: pl.BlockSpec((tm, width), lambda i: (i, 0))
    ob_spec, scratch = tok(W_GROUP), []
    if latent:
        tiles_per_seq = seq_len // tm
        ob_spec = pl.BlockSpec((1, tm // GRID_W, GRID_W * W_GROUP), lambda i: (i // tiles_per_seq, i % tiles_per_seq, 0))
        scratch = [pltpu.VMEM((N_HEADS, tm // GRID_W * ROW_PITCH, D_HEAD), F32)]
    consts = [p["norm_a"], p["norm_b"], p["w_out"], p["norm2"], p["w_gate"], p["w_up"], p["w_down"], p["norm_f"]]
    return pl.pallas_call(
        functools.partial(_tail_kernel, latent=latent),
        grid=(n_tok // tm,),
        in_specs=[tok(D_MODEL), tok(W_GROUP), tok(W_GROUP), ob_spec, ob_spec, tok(W_GROUP), tok(W_GROUP),
                  pl.BlockSpec((1, 6, D_MODEL), lambda i: (mod_row_of_tile(i), 0, 0))]
                 + [_resident(a.shape) for a in consts],
        out_specs=tok(D_MODEL),
        out_shape=jax.ShapeDtypeStruct((n_tok, D_MODEL), F32),
        scratch_shapes=scratch,
        compiler_params=pltpu.CompilerParams(dimension_semantics=("arbitrary",),
                                             vmem_limit_bytes=VMEM_LIMIT),
        name="tail",
    )(x2d, oaf, oab, obf, obb, ga, zb, mod3, *consts)


def _stream(x, mod3, mod_row_of_tile, s0_a, s0_b, p, latent):
    batch, seq, _ = x.shape
    x2d = x.reshape(batch * seq, D_MODEL)
    tm = 512
    mod_row = functools.partial(mod_row_of_tile, tm=tm)
    qa, f_fwd, f_bwd, va, ga, qkv, zb, gates = _inproj(x2d, mod3, mod_row, p, tm, latent, seq)
    hgrn_args = [a.reshape(batch, seq, W_GROUP) for a in (qa, f_fwd, f_bwd, va)]
    n_chunks = seq // CHUNK
    if latent:
        assert seq // GRID_W == CHUNK and n_chunks == GRID_W
        oaf, oab, obf, obb = _scan(hgrn_args, (qkv, gates), (s0_a, s0_b), batch, group=4, n_steps=n_chunks,
                                   gdn_columns=GRID_W, emit_state=False, gdn_out_dtype=F32)
        states = (None, None)
    else:
        gdn_args = (qkv.reshape(batch, seq, 3 * W_GROUP), gates.reshape(batch, seq, GATE_LANES))
        oaf, oab, obf, obb, new_a, new_b = _scan(hgrn_args, gdn_args, None, batch, group=4, n_steps=n_chunks,
                                                 gdn_columns=1, emit_state=True, gdn_out_dtype=BF16)
        obf, obb = (o.reshape(batch * seq, W_GROUP) for o in (obf, obb))
        states = (new_a, new_b)
    oaf, oab = (o.reshape(batch * seq, W_GROUP) for o in (oaf, oab))
    y = _tail(x2d, oaf, oab, obf, obb, ga, zb, mod3, mod_row, p, tm, latent, seq)
    return y.reshape(batch, seq, D_MODEL), states


def kernel(x_prompt, x_sample, c, state_hgrn, state_gdn, c_ctx, w_ada, b_ada, norm1, norm2, w_in, conv_w,
           hgrn_lb, gdn_A_log, gdn_dt_bias, hgrn_out_norm, gdn_out_norm, w_out, w_gate, w_up, w_down, norm_f):
    depth = w_in.shape[0]
    assert depth == 1 and hgrn_lb.shape[0] == 2
    dec_batch, dec_seq, _ = x_sample.shape
    l = 0

    n_main = N_MAIN_GROUPS * W_GROUP
    pad8 = jnp.zeros((1, 2 * N_HEADS), F32)
    gparams = jnp.concatenate(
        [jnp.concatenate([pad8, a.reshape(1, 2 * N_HEADS).astype(F32),
                          jnp.zeros((1, GATE_LANES - 4 * N_HEADS), F32)], axis=1)
         for a in (gdn_A_log[l], gdn_dt_bias[l])], axis=0)
    p = {
        "norm1": norm1[l].reshape(1, D_MODEL), "norm2": norm2[l].reshape(1, D_MODEL),
        "lbp": hgrn_lb.reshape(2, 2 * W_GROUP), "gparams": gparams,
        "w_main": w_in[l].astype(BF16),
        "w_gates": jnp.pad(w_in[l][:, n_main:], ((0, 0), (0, GATE_LANES - 4 * N_HEADS))).astype(BF16),
        "conv_w": conv_w[l],
        "norm_a": hgrn_out_norm[l].reshape(1, W_GROUP), "norm_b": gdn_out_norm[l].reshape(1, W_GROUP),
        "w_out": w_out[l].astype(BF16), "w_gate": w_gate[l].astype(BF16), "w_up": w_up[l].astype(BF16),
        "w_down": w_down[l].astype(BF16), "norm_f": norm_f.reshape(1, D_MODEL),
    }

    n_mod_rows = 8
    cvec = jnp.concatenate([c_ctx[None], c, jnp.zeros((n_mod_rows - 1 - dec_batch, D_MODEL), F32)], axis=0)
    mod3 = _modulation(cvec, w_ada[l], b_ada[l]).reshape(n_mod_rows, 6, D_MODEL)

    y_prompt, (new_a, new_b) = _stream(x_prompt, mod3, lambda i, tm: 0, None, None, p, latent=False)
    y_sample, _ = _stream(x_sample, mod3, lambda i, tm: 1 + i // (dec_seq // tm), state_hgrn[:, l],
                          state_gdn[:, l], p, latent=True)
    return y_prompt, y_sample, new_a[:, None], new_b[:, None]
```

```python
import functools

import jax
import jax.numpy as jnp
from jax import lax
from jax.experimental import pallas as pl
from jax.experimental.pallas import tpu as pltpu

F32 = jnp.float32
BF16 = jnp.bfloat16

D_MODEL = 1024
N_HEADS = 4
D_HEAD = 128
W_GROUP = N_HEADS * D_HEAD
CHUNK = 64
GRID_W = 64
CONV_W = 3
EPS = 1e-6
N_MAIN_GROUPS = 9
GATE_LANES = 128
VMEM_LIMIT = 56 * 1024 * 1024


NEG_LOG2_E = -1.4426950408889634


def _sigmoid(x):
    return 1.0 / (1.0 + jnp.exp2(x * NEG_LOG2_E))


def _silu(x):
    return x * _sigmoid(x)


def _dot(a, b):
    return jnp.dot(a.astype(BF16), b.astype(BF16), preferred_element_type=F32)


def _dot_nt(a, b):
    return lax.dot_general(a.astype(BF16), b.astype(BF16), (((1,), (1,)), ((), ())),
                           preferred_element_type=F32)


def _split3(x):
    x1 = x.astype(BF16)
    r = x - x1.astype(F32)
    x2 = r.astype(BF16)
    x3 = (r - x2.astype(F32)).astype(BF16)
    return x1, x2, x3


def _cumsum_rows(tri3, x):
    return jnp.dot(tri3, jnp.concatenate(_split3(x), axis=0), preferred_element_type=F32)


def _tri3(direction):
    tri = _tri_masks(direction)[0].astype(BF16)
    return jnp.concatenate([tri, tri, tri], axis=1)


def _tri_masks(direction):
    r = lax.broadcasted_iota(jnp.int32, (CHUNK, CHUNK), 0)
    c = lax.broadcasted_iota(jnp.int32, (CHUNK, CHUNK), 1)
    if direction == 0:
        return c <= r, c < r
    return c >= r, c > r


def _resident(shape):
    return pl.BlockSpec(shape, lambda i: (0,) * len(shape), pipeline_mode=pl.Buffered(1))


def _mod_kernel(c_ref, w_ref, b_ref, o_ref):
    s = _silu(c_ref[...])
    o_ref[...] = _dot(s, w_ref[...]) + b_ref[...]


def _modulation(cvec, w_ada, b_ada):
    n_rows, d = cvec.shape
    n_out = w_ada.shape[1]
    tn = 1536
    return pl.pallas_call(
        _mod_kernel,
        grid=(n_out // tn,),
        in_specs=[pl.BlockSpec((n_rows, d), lambda j: (0, 0)),
                  pl.BlockSpec((d, tn), lambda j: (0, j)),
                  pl.BlockSpec((1, tn), lambda j: (0, j))],
        out_specs=pl.BlockSpec((n_rows, tn), lambda j: (0, j)),
        out_shape=jax.ShapeDtypeStruct((n_rows, n_out), F32),
        compiler_params=pltpu.CompilerParams(dimension_semantics=("arbitrary",),
                                             vmem_limit_bytes=VMEM_LIMIT),
        name="mod",
    )(cvec, w_ada, b_ada.reshape(1, n_out))


HALO = GRID_W
LANE = 128
QKV_TILES = 3 * W_GROUP // LANE
ROW_PITCH = GRID_W + 8


def _inproj_kernel(*refs, latent, seq_len):
    it = iter(refs)
    x_ref = next(it)
    xp_ref, xn_ref = (next(it), next(it)) if latent else (None, None)
    mod_ref, n1_ref, lb_ref, gp_ref, cw_ref, w_ref, wg_ref = (next(it) for _ in range(7))
    qa_ref, ff_ref, fb_ref, va_ref, ga_ref, qkv_ref, zb_ref, gates_ref = (next(it) for _ in range(8))
    qkv_scr, gates_scr = (next(it), next(it)) if latent else (None, None)
    tm = x_ref.shape[0]
    m = mod_ref[0]

    norm_scale = n1_ref[...] * (1.0 + m[1:2])

    def normed(x):
        return (x * lax.rsqrt(jnp.mean(x * x, axis=-1, keepdims=True) + EPS) * norm_scale + m[0:1]).astype(BF16)

    hb = normed(x_ref[...])

    w_qkv = w_ref[:, 5 * W_GROUP:8 * W_GROUP]
    if latent:
        i = pl.program_id(0)
        tiles_per_seq = seq_len // tm
        ext = jnp.dot(jnp.concatenate([normed(xp_ref[...]), hb, normed(xn_ref[...])], axis=0), w_qkv,
                      preferred_element_type=F32)
        above = jnp.where(i % tiles_per_seq == 0, 0.0, ext[:HALO])
        below = jnp.where(i % tiles_per_seq == tiles_per_seq - 1, 0.0, ext[HALO + tm:])
        cur = ext[HALO:HALO + tm]
        prev = jnp.concatenate([above, ext[HALO:tm]], axis=0)
        nxt = jnp.concatenate([ext[2 * HALO:HALO + tm], below], axis=0)
    else:
        cur = jnp.dot(hb, w_qkv, preferred_element_type=F32)
        pos = lax.broadcasted_iota(jnp.int32, cur.shape, 0) % seq_len
        prev = jnp.where(pos == 0, 0.0, pltpu.roll(cur, 1, axis=0))
        nxt = jnp.where(pos == seq_len - 1, 0.0, pltpu.roll(cur, tm - 1, axis=0))
    cw = cw_ref[...]
    y = _silu(prev * cw[0:1] + cur * cw[1:2] + nxt * cw[2:3])
    tiles = []
    for j in range(QKV_TILES):
        t = y[:, j * LANE:(j + 1) * LANE]
        if j < 2 * N_HEADS:
            inv = lax.rsqrt(jnp.sum(t * t, axis=-1, keepdims=True) + EPS)
            t = t * (inv * (D_HEAD ** -0.5) if j < N_HEADS else inv)
        tiles.append(t)

    raw = jnp.dot(hb, wg_ref[...], preferred_element_type=F32)
    gp = gp_ref[...]
    z = raw + gp[1:2]
    softplus = jnp.maximum(z, 0.0) + jnp.log(1.0 + jnp.exp(-jnp.abs(z)))
    lane = lax.broadcasted_iota(jnp.int32, raw.shape, 1)
    gates = jnp.where(lane < 2 * N_HEADS, _sigmoid(raw),
                      jnp.where(lane < 4 * N_HEADS, -jnp.exp(gp[0:1]) * softplus, 0.0))

    if not latent:
        for j, t in enumerate(tiles):
            qkv_ref[:, j * LANE:(j + 1) * LANE] = t
        gates_ref[...] = gates
    else:
        n_rows = tm // GRID_W
        for r in range(n_rows):
            rows, dst = slice(r * GRID_W, (r + 1) * GRID_W), slice(r * ROW_PITCH, r * ROW_PITCH + GRID_W)
            for j, t in enumerate(tiles):
                qkv_scr[j, dst, :] = t[rows]
            gates_scr[dst, :] = gates[rows]
        for c in range(GRID_W):
            for j in range(QKV_TILES):
                qkv_ref[0, :, (c * QKV_TILES + j) * LANE:(c * QKV_TILES + j + 1) * LANE] = (
                    qkv_scr[j, pl.ds(c, n_rows, stride=ROW_PITCH), :])
            gates_ref[0, :, c * GATE_LANES:(c + 1) * GATE_LANES] = gates_scr[pl.ds(c, n_rows, stride=ROW_PITCH), :]

    def proj(j):
        return jnp.dot(hb, w_ref[:, j * W_GROUP:(j + 1) * W_GROUP], preferred_element_type=F32)

    lbp = lb_ref[...]
    e = jnp.exp(lbp - jnp.max(lbp, axis=0, keepdims=True))
    lb = e[0:1] / jnp.sum(e, axis=0, keepdims=True)
    lb_f, lb_b = lb[:, :W_GROUP], lb[:, W_GROUP:]

    qa_ref[...] = _silu(proj(0)).astype(BF16)
    ff_ref[...] = lb_f + (1.0 - lb_f) * _sigmoid(proj(1))
    fb_ref[...] = lb_b + (1.0 - lb_b) * _sigmoid(proj(2))
    va_ref[...] = proj(3).astype(BF16)
    ga_ref[...] = _silu(proj(4)).astype(BF16)
    zb_ref[...] = _silu(proj(8)).astype(BF16)


def _inproj(x2d, mod3, mod_row_of_tile, p, tm, latent, seq_len):
    n_tok = x2d.shape[0]
    tok = lambda width: pl.BlockSpec((tm, width), lambda i: (i, 0))
    widths = [W_GROUP] * 5 + [3 * W_GROUP, W_GROUP, GATE_LANES]
    out_specs = [tok(w) for w in widths]
    dtypes = [BF16, F32, F32, BF16, BF16, F32, BF16, F32]
    out_shape = [jax.ShapeDtypeStruct((n_tok, w), dt) for w, dt in zip(widths, dtypes)]
    in_specs, args, scratch = [tok(D_MODEL)], [x2d], []
    if latent:
        n_halo_blocks, per_tile, rows = n_tok // HALO, tm // HALO, tm // GRID_W
        tiles_per_seq = seq_len // tm
        in_specs += [pl.BlockSpec((HALO, D_MODEL), lambda i: (jnp.maximum(i * per_tile - 1, 0), 0)),
                     pl.BlockSpec((HALO, D_MODEL), lambda i: (jnp.minimum((i + 1) * per_tile, n_halo_blocks - 1), 0))]
        args += [x2d, x2d]
        col = lambda width: pl.BlockSpec((1, rows, GRID_W * width),
                                         lambda i: (i // tiles_per_seq, i % tiles_per_seq, 0))
        col_shape = lambda width: jax.ShapeDtypeStruct((n_tok // seq_len, seq_len // GRID_W, GRID_W * width), F32)
        for k, width in ((5, 3 * W_GROUP), (7, GATE_LANES)):
            out_specs[k], out_shape[k] = col(width), col_shape(width)
        scratch = [pltpu.VMEM((QKV_TILES, rows * ROW_PITCH, LANE), F32), pltpu.VMEM((rows * ROW_PITCH, GATE_LANES), F32)]
    consts = [p["norm1"], p["lbp"], p["gparams"], p["conv_w"], p["w_main"], p["w_gates"]]
    in_specs += [pl.BlockSpec((1, 6, D_MODEL), lambda i: (mod_row_of_tile(i), 0, 0))] + [_resident(a.shape) for a in consts]
    return pl.pallas_call(
        functools.partial(_inproj_kernel, latent=latent, seq_len=seq_len),
        grid=(n_tok // tm,),
        in_specs=in_specs, out_specs=out_specs, out_shape=out_shape, scratch_shapes=scratch,
        compiler_params=pltpu.CompilerParams(dimension_semantics=("arbitrary",),
                                             vmem_limit_bytes=VMEM_LIMIT),
        name="inproj",
    )(*args, mod3, *consts)


INVERSE_BASE_BLOCK = 8


def _hgrn_stages(qf_ref, ff_ref, vf_ref, qb_ref, fb_ref, vb_ref, of_ref, ob_ref, s_ref, group):
    pre = []
    for b in range(group):
        for d, f_ref in enumerate((ff_ref, fb_ref)):
            pre.append((b, d, _cumsum_rows(_tri3(d), jnp.log(f_ref[b]))))
    yield
    chains = []
    for b, d, cum_all in pre:
        q_ref, f_ref, v_ref, o_ref = (qf_ref, ff_ref, vf_ref, of_ref) if d == 0 else (qb_ref, fb_ref, vb_ref, ob_ref)
        incl, _ = _tri_masks(d)
        for h in range(N_HEADS):
            sl = slice(h * D_HEAD, (h + 1) * D_HEAD)
            k = 1.0 - f_ref[b, :, sl]
            G = cum_all[:, sl]
            g_last_row = G[CHUNK - 1:CHUNK] if d == 0 else G[0:1]
            chains.append(dict(
                b=b, d=d, h=h, sl=sl, o_ref=o_ref, incl=incl, vb=v_ref[b, :, sl].astype(BF16),
                decay=jnp.exp(jnp.broadcast_to(g_last_row, (8, D_HEAD)).T[:, 0:1]),
                q_dec=(q_ref[b, :, sl].astype(F32) * jnp.exp(G)).astype(BF16), k_dec=k * jnp.exp2(G * NEG_LOG2_E),
                k_tail_t=(k * jnp.exp(g_last_row - G)).T.astype(BF16)))
    attns = [jnp.where(ch["incl"], _dot_nt(ch["q_dec"], ch["k_dec"]), 0.0).astype(BF16) for ch in chains]
    yield
    states = [s_ref[ch["b"], ch["d"], ch["h"]] for ch in chains]
    outs = [_dot(jnp.concatenate([ch["q_dec"], attn], axis=1), jnp.concatenate([s.astype(BF16), ch["vb"]], axis=0))
            for ch, attn, s in zip(chains, attns, states)]
    yield
    upds = [_dot(ch["k_tail_t"], ch["vb"]) for ch in chains]
    for ch, o, s, u in zip(chains, outs, states, upds):
        ch["o_ref"][ch["b"], :, ch["sl"]] = o.astype(BF16)
        s_ref[ch["b"], ch["d"], ch["h"]] = ch["decay"] * s + u


PAIR = 2


def _pair_index():
    r = lax.broadcasted_iota(jnp.int32, (CHUNK, PAIR * CHUNK), 0)
    lane = lax.broadcasted_iota(jnp.int32, (CHUNK, PAIR * CHUNK), 1)
    return r, lane % CHUNK, lane // CHUNK


def _block_diag(packed):
    head = lax.broadcasted_iota(jnp.int32, packed.shape, 1) // (packed.shape[1] // PAIR)
    return jnp.concatenate([jnp.where(head == h, packed, jnp.zeros_like(packed)) for h in range(PAIR)], axis=0)


def _unit_triangular_inverses(lowers):
    r, c, _ = _pair_index()
    same_block = lambda block: r // block == c // block
    pdot = lambda a, b: _dot(a, _block_diag(b.astype(BF16)))
    eye = jnp.where(r == c, 1.0, 0.0)
    diag = same_block(INVERSE_BASE_BLOCK)
    ds = [jnp.where(diag, lo, 0.0) for lo in lowers]
    ts = [eye - d for d in ds]
    ps = [pdot(d, d) for d in ds]
    yield
    power = 4
    while power < INVERSE_BASE_BLOCK:
        tps = [pdot(jnp.concatenate([t.astype(BF16), p.astype(BF16)], axis=0), p) for t, p in zip(ts, ps)]
        yield
        ts = [t + tp[:CHUNK] for t, tp in zip(ts, tps)]
        ps = [tp[CHUNK:] for tp in tps]
        power *= 2
    ts = [t + pdot(t, p) for t, p in zip(ts, ps)]
    yield
    block = INVERSE_BASE_BLOCK
    while block < CHUNK:
        off_mask = same_block(2 * block) & jnp.logical_not(same_block(block))
        ws = [pdot(t, jnp.where(off_mask, lo, 0.0)) for t, lo in zip(ts, lowers)]
        yield
        ts = [t - pdot(w, t) for t, w in zip(ts, ws)]
        yield
        block *= 2
    return ts


def _gdn_stages(xf_ref, gf_ref, xb_ref, gb_ref, of_ref, ob_ref, s_ref, group):
    pre = []
    for b in range(group):
        for d, g_ref in enumerate((gf_ref, gb_ref)):
            gates = g_ref[b]
            pre.append((b, d, gates, _cumsum_rows(_tri3(d), gates)))
    yield
    r, c, head = _pair_index()
    lane_head = lax.broadcasted_iota(jnp.int32, (CHUNK, PAIR * D_HEAD), 1) // D_HEAD
    pairs = []
    for b, d, gates, cum in pre:
        x_ref = xf_ref if d == 0 else xb_ref
        incl, strict = (c <= r, c < r) if d == 0 else (c >= r, c > r)
        cum_t = jnp.concatenate([cum] * PAIR, axis=0).T
        for hp in range(N_HEADS // PAIR):
            hs = [PAIR * hp + i for i in range(PAIR)]
            jb, jg = [d * N_HEADS + h for h in hs], [2 * N_HEADS + d * N_HEADS + h for h in hs]
            part = lambda k: x_ref[b, :, (k * N_HEADS + hs[0]) * D_HEAD:(k * N_HEADS + hs[0] + PAIR) * D_HEAD]
            q2, k2, v2 = part(0), part(1), part(2)
            beta2 = jnp.where(lane_head == 0, gates[:, jb[0]:jb[0] + 1], gates[:, jb[1]:jb[1] + 1])
            g_cum = [cum[:, j:j + 1] for j in jg]
            g_col = jnp.where(head == 0, g_cum[0], g_cum[1])
            g_row = jnp.where(head[0:1] == 0, cum_t[jg[0]:jg[0] + 1, :], cum_t[jg[1]:jg[1] + 1, :])
            diff = g_col - g_row
            pairs.append(dict(
                b=b, d=d, hs=hs, q2=q2, k2=k2, v2=v2, beta2=beta2, g_cum=g_cum, strict=strict, k_beta2=k2 * beta2,
                decay_mask=jnp.where(incl, jnp.exp(jnp.where(incl, diff, 0.0)), 0.0)))

    kqs = [_dot_nt(jnp.concatenate([pr["k_beta2"].astype(BF16), pr["q2"].astype(BF16)], axis=0),
                   _block_diag(pr["k2"].astype(BF16))) for pr in pairs]
    yield
    lowers = [jnp.where(pr["strict"], kq[:CHUNK] * pr["decay_mask"], 0.0) for pr, kq in zip(pairs, kqs)]
    ts = yield from _unit_triangular_inverses(lowers)
    for pr, kq in zip(pairs, kqs):
        e_g = [jnp.exp(g) for g in pr["g_cum"]]
        e_g2 = jnp.where(lane_head == 0, e_g[0], e_g[1])
        pr["rhs"] = jnp.concatenate(
            [jnp.concatenate([(pr["v2"] * pr["beta2"])[:, i * D_HEAD:(i + 1) * D_HEAD],
                              (pr["k_beta2"] * e_g2)[:, i * D_HEAD:(i + 1) * D_HEAD]], axis=1) for i in range(PAIR)],
            axis=0).astype(BF16)
        attn2 = (kq[CHUNK:] * pr["decay_mask"]).astype(BF16)
        pr["attn"] = [attn2[:, i * CHUNK:(i + 1) * CHUNK] for i in range(PAIR)]
        g_last = [g[CHUNK - 1:CHUNK] if pr["d"] == 0 else g[0:1] for g in pr["g_cum"]]
        q_dec2 = (pr["q2"] * e_g2).astype(BF16)
        pr["q_dec"] = [q_dec2[:, i * D_HEAD:(i + 1) * D_HEAD] for i in range(PAIR)]
        pr["k_tail_t"] = [(pr["k2"][:, i * D_HEAD:(i + 1) * D_HEAD] * jnp.exp(g_last[i] - pr["g_cum"][i])).T.astype(BF16)
                          for i in range(PAIR)]
        pr["decay"] = [jnp.exp(g) for g in g_last]
    xs = [_dot(_block_diag(t.astype(BF16)), pr["rhs"]) for pr, t in zip(pairs, ts)]
    yield
    chains = []
    for pr, x in zip(pairs, xs):
        for i, h in enumerate(pr["hs"]):
            xi = x[i * CHUNK:(i + 1) * CHUNK]
            chains.append(dict(b=pr["b"], d=pr["d"], h=h, u=xi[:, :D_HEAD], w=xi[:, D_HEAD:].astype(BF16),
                               attn=pr["attn"][i], q_dec=pr["q_dec"][i], k_tail_t=pr["k_tail_t"][i],
                               decay=pr["decay"][i]))

    o_refs = (of_ref, ob_ref)
    states = [s_ref[ch["b"], ch["d"], ch["h"]] for ch in chains]
    sbs = [s.astype(BF16) for s in states]
    wqs = [_dot(jnp.concatenate([ch["w"], ch["q_dec"]], axis=0), sb) for ch, sb in zip(chains, sbs)]
    yield
    vbs = [(ch["u"] - wq[:CHUNK]).astype(BF16) for ch, wq in zip(chains, wqs)]
    outs = [wq[CHUNK:] + _dot(ch["attn"], vb) for ch, wq, vb in zip(chains, wqs, vbs)]
    yield
    upd = [_dot(ch["k_tail_t"], vb) for ch, vb in zip(chains, vbs)]
    for ch, s, u, o in zip(chains, states, upd, outs):
        o_refs[ch["d"]][ch["b"], :, ch["h"] * D_HEAD:(ch["h"] + 1) * D_HEAD] = o.astype(of_ref.dtype)
        s_ref[ch["b"], ch["d"], ch["h"]] = s * ch["decay"] + u


def _run_interleaved(primary, secondary, every):
    live = [primary, secondary]
    count = 0
    while live:
        gen = primary if primary in live and (secondary not in live or count < every) else secondary
        count = count + 1 if gen is primary else 0
        try:
            next(gen)
        except StopIteration:
            live.remove(gen)


def _scan_kernel(*refs, n_steps, group, hgrn, gdn, has_s0, emit_state):
    it = iter(refs)
    take = lambda k: [next(it) for _ in range(k)]
    h_in = take(6) if hgrn else None
    g_in = take(4) if gdn else None
    s0 = take(int(hgrn) + int(gdn)) if has_s0 else None
    h_out = take(2) if hgrn else None
    g_out = take(2) if gdn else None
    st = take(int(hgrn) + int(gdn)) if emit_state else None
    s_refs = take(int(hgrn) + int(gdn))
    n = pl.program_id(1)

    @pl.when(n == 0)
    def _():
        for k, s_ref in enumerate(s_refs):
            s_ref[...] = s0[k][...] if has_s0 else jnp.zeros_like(s_ref)

    gens = []
    if gdn:
        gens.append(_gdn_stages(*g_in, *g_out, s_refs[-1], group))
    if hgrn:
        gens.append(_hgrn_stages(*h_in, *h_out, s_refs[0], group))
    if len(gens) == 2:
        _run_interleaved(gens[0], gens[1], every=1)
    else:
        for _ in gens[0]:
            pass

    if emit_state:
        @pl.when(n == n_steps - 1)
        def _():
            for st_ref, s_ref in zip(st, s_refs):
                st_ref[...] = s_ref[...]


def _scan(hgrn_args, gdn_args, states0, batch, group, n_steps, gdn_columns, emit_state, gdn_out_dtype):
    state_shape = (group, 2, N_HEADS, D_HEAD, D_HEAD)
    state_spec = pl.BlockSpec(state_shape, lambda b, n: (b, 0, 0, 0, 0))
    in_specs, args, out_specs, out_shape = [], [], [], []
    if hgrn_args is not None:
        qa, f_fwd, f_bwd, va = hgrn_args
        fwd = pl.BlockSpec((group, CHUNK, W_GROUP), lambda b, n: (b, n, 0))
        bwd = pl.BlockSpec((group, CHUNK, W_GROUP), lambda b, n: (b, n_steps - 1 - n, 0))
        in_specs += [fwd, fwd, fwd, bwd, bwd, bwd]
        args += [qa, f_fwd, va, qa, f_bwd, va]
        out_specs += [fwd, bwd]
        out_shape += [jax.ShapeDtypeStruct(qa.shape, BF16)] * 2
    if gdn_args is not None:
        qkv3, gates3 = gdn_args
        columns = gdn_columns
        fwd_map = (lambda b, n: (b, n, 0)) if columns == 1 else (lambda b, n: (b, 0, n))
        bwd_map = lambda b, n: fwd_map(b, n_steps - 1 - n)
        spec = lambda width, imap: pl.BlockSpec((group, CHUNK, width), imap)
        in_specs += [spec(3 * W_GROUP, fwd_map), spec(GATE_LANES, fwd_map),
                     spec(3 * W_GROUP, bwd_map), spec(GATE_LANES, bwd_map)]
        args += [qkv3, gates3, qkv3, gates3]
        out_specs += [spec(W_GROUP, fwd_map), spec(W_GROUP, bwd_map)]
        out_shape += [jax.ShapeDtypeStruct((batch, qkv3.shape[1], columns * W_GROUP), gdn_out_dtype)] * 2
    n_mixers = int(hgrn_args is not None) + int(gdn_args is not None)
    if states0 is not None:
        in_specs, args = in_specs + [state_spec] * n_mixers, args + list(states0)
    if emit_state:
        out_specs = out_specs + [state_spec] * n_mixers
        out_shape = out_shape + [jax.ShapeDtypeStruct((batch,) + state_shape[1:], F32)] * n_mixers
    return pl.pallas_call(
        functools.partial(_scan_kernel, n_steps=n_steps, group=group, hgrn=hgrn_args is not None,
                          gdn=gdn_args is not None, has_s0=states0 is not None, emit_state=emit_state),
        grid=(batch // group, n_steps),
        in_specs=in_specs, out_specs=out_specs, out_shape=out_shape,
        scratch_shapes=[pltpu.VMEM(state_shape, F32)] * n_mixers,
        compiler_params=pltpu.CompilerParams(dimension_semantics=("arbitrary", "arbitrary"),
                                             vmem_limit_bytes=VMEM_LIMIT),
        name="scan",
    )(*args)


FFN_SPLIT = 2
MXU_K_TILE = 256


def _ffn_pieces(d_ff):
    k_tiles = -(-d_ff // MXU_K_TILE)
    bounds = [min(d_ff, MXU_K_TILE * -(-k_tiles * s // FFN_SPLIT)) for s in range(FFN_SPLIT + 1)]
    return list(zip(bounds[:-1], bounds[1:]))


def _tail_kernel(*refs, latent):
    it = iter(refs)
    (x_ref, oaf_ref, oab_ref, obf_ref, obb_ref, ga_ref, zb_ref, mod_ref, na_ref, nb_ref, wo_ref, n2_ref,
     wg_ref, wu_ref, wd_ref, nf_ref) = (next(it) for _ in range(16))
    y_ref = next(it)
    ob_scr = next(it) if latent else None
    tm = x_ref.shape[0]

    def gated_norm(o, w_ref, gate_ref, h):
        sl = slice(h * D_HEAD, (h + 1) * D_HEAD)
        o = o * lax.rsqrt(jnp.mean(o * o, axis=-1, keepdims=True) + EPS)
        return (o * w_ref[:, sl] * gate_ref[:, sl].astype(F32)).astype(BF16)

    def both(f_ref, b_ref, h):
        sl = slice(h * D_HEAD, (h + 1) * D_HEAD)
        return f_ref[:, sl].astype(F32) + b_ref[:, sl].astype(F32)

    if latent:
        n_rows = tm // GRID_W
        for c in range(GRID_W):
            for h in range(N_HEADS):
                sl = slice(c * W_GROUP + h * D_HEAD, c * W_GROUP + (h + 1) * D_HEAD)
                ob_scr[h, pl.ds(c, n_rows, stride=ROW_PITCH), :] = obf_ref[0, :, sl] + obb_ref[0, :, sl]
        o_b = [jnp.concatenate([ob_scr[h, r * ROW_PITCH:r * ROW_PITCH + GRID_W, :] for r in range(n_rows)], axis=0)
               for h in range(N_HEADS)]
    else:
        o_b = [both(obf_ref, obb_ref, h) for h in range(N_HEADS)]
    o_a = [both(oaf_ref, oab_ref, h) for h in range(N_HEADS)]
    mixed = jnp.concatenate([gated_norm(o, na_ref, ga_ref, h) for h, o in enumerate(o_a)]
                            + [gated_norm(o, nb_ref, zb_ref, h) for h, o in enumerate(o_b)], axis=-1)
    m = mod_ref[0]
    x1 = x_ref[...] + m[2:3] * jnp.dot(mixed, wo_ref[...], preferred_element_type=F32)
    y = x1 * lax.rsqrt(jnp.mean(x1 * x1, axis=-1, keepdims=True) + EPS) * n2_ref[...]
    h2 = (y * (1.0 + m[4:5]) + m[3:4]).astype(BF16)

    ff = None
    for lo, hi in _ffn_pieces(wg_ref.shape[1]):
        cols = slice(lo, hi)
        gate = jnp.dot(h2, wg_ref[:, cols], preferred_element_type=F32)
        up = jnp.dot(h2, wu_ref[:, cols], preferred_element_type=F32)
        part = jnp.dot((_silu(gate) * up).astype(BF16), wd_ref[cols, :], preferred_element_type=F32)
        ff = part if ff is None else ff + part
    x2 = x1 + m[5:6] * ff
    y_ref[...] = x2 * lax.rsqrt(jnp.mean(x2 * x2, axis=-1, keepdims=True) + EPS) * nf_ref[...]


def _tail(x2d, oaf, oab, obf, obb, ga, zb, mod3, mod_row_of_tile, p, tm, latent, seq_len):
    n_tok = x2d.shape[0]
    assert p["w_gate"].shape[1] % LANE == 0
    tok = lambda width: pl.BlockSpec((tm, width), lambda i: (i, 0))
    ob_spec, scratch = tok(W_GROUP), []
    if latent:
        tiles_per_seq = seq_len // tm
        ob_spec = pl.BlockSpec((1, tm // GRID_W, GRID_W * W_GROUP), lambda i: (i // tiles_per_seq, i % tiles_per_seq, 0))
        scratch = [pltpu.VMEM((N_HEADS, tm // GRID_W * ROW_PITCH, D_HEAD), F32)]
    consts = [p["norm_a"], p["norm_b"], p["w_out"], p["norm2"], p["w_gate"], p["w_up"], p["w_down"], p["norm_f"]]
    return pl.pallas_call(
        functools.partial(_tail_kernel, latent=latent),
        grid=(n_tok // tm,),
        in_specs=[tok(D_MODEL), tok(W_GROUP), tok(W_GROUP), ob_spec, ob_spec, tok(W_GROUP), tok(W_GROUP),
                  pl.BlockSpec((1, 6, D_MODEL), lambda i: (mod_row_of_tile(i), 0, 0))]
                 + [_resident(a.shape) for a in consts],
        out_specs=tok(D_MODEL),
        out_shape=jax.ShapeDtypeStruct((n_tok, D_MODEL), F32),
        scratch_shapes=scratch,
        compiler_params=pltpu.CompilerParams(dimension_semantics=("arbitrary",),
                                             vmem_limit_bytes=VMEM_LIMIT),
        name="tail",
    )(x2d, oaf, oab, obf, obb, ga, zb, mod3, *consts)


def _stream(x, mod3, mod_row_of_tile, s0_a, s0_b, p, latent):
    batch, seq, _ = x.shape
    x2d = x.reshape(batch * seq, D_MODEL)
    tm = 512
    mod_row = functools.partial(mod_row_of_tile, tm=tm)
    qa, f_fwd, f_bwd, va, ga, qkv, zb, gates = _inproj(x2d, mod3, mod_row, p, tm, latent, seq)
    hgrn_args = [a.reshape(batch, seq, W_GROUP) for a in (qa, f_fwd, f_bwd, va)]
    n_chunks = seq // CHUNK
    if latent:
        assert seq // GRID_W == CHUNK and n_chunks == GRID_W
        oaf, oab, obf, obb = _scan(hgrn_args, (qkv, gates), (s0_a, s0_b), batch, group=4, n_steps=n_chunks,
                                   gdn_columns=GRID_W, emit_state=False, gdn_out_dtype=F32)
        states = (None, None)
    else:
        gdn_args = (qkv.reshape(batch, seq, 3 * W_GROUP), gates.reshape(batch, seq, GATE_LANES))
        oaf, oab, obf, obb, new_a, new_b = _scan(hgrn_args, gdn_args, None, batch, group=4, n_steps=n_chunks,
                                                 gdn_columns=1, emit_state=True, gdn_out_dtype=BF16)
        obf, obb = (o.reshape(batch * seq, W_GROUP) for o in (obf, obb))
        states = (new_a, new_b)
    oaf, oab = (o.reshape(batch * seq, W_GROUP) for o in (oaf, oab))
    y = _tail(x2d, oaf, oab, obf, obb, ga, zb, mod3, mod_row, p, tm, latent, seq)
    return y.reshape(batch, seq, D_MODEL), states


def kernel(x_prompt, x_sample, c, state_hgrn, state_gdn, c_ctx, w_ada, b_ada, norm1, norm2, w_in, conv_w,
           hgrn_lb, gdn_A_log, gdn_dt_bias, hgrn_out_norm, gdn_out_norm, w_out, w_gate, w_up, w_down, norm_f):
    depth = w_in.shape[0]
    assert depth == 1 and hgrn_lb.shape[0] == 2 and conv_w.shape[1] == CONV_W
    dec_batch, dec_seq, _ = x_sample.shape
    l = 0

    n_main = N_MAIN_GROUPS * W_GROUP
    pad8 = jnp.zeros((1, 2 * N_HEADS), F32)
    gparams = jnp.concatenate(
        [jnp.concatenate([pad8, a.reshape(1, 2 * N_HEADS).astype(F32),
                          jnp.zeros((1, GATE_LANES - 4 * N_HEADS), F32)], axis=1)
         for a in (gdn_A_log[l], gdn_dt_bias[l])], axis=0)
    p = {
        "norm1": norm1[l].reshape(1, D_MODEL), "norm2": norm2[l].reshape(1, D_MODEL),
        "lbp": hgrn_lb.reshape(2, 2 * W_GROUP), "gparams": gparams,
        "w_main": w_in[l].astype(BF16),
        "w_gates": jnp.pad(w_in[l][:, n_main:], ((0, 0), (0, GATE_LANES - 4 * N_HEADS))).astype(BF16),
        "conv_w": conv_w[l],
        "norm_a": hgrn_out_norm[l].reshape(1, W_GROUP), "norm_b": gdn_out_norm[l].reshape(1, W_GROUP),
        "w_out": w_out[l].astype(BF16), "w_gate": w_gate[l].astype(BF16), "w_up": w_up[l].astype(BF16),
        "w_down": w_down[l].astype(BF16), "norm_f": norm_f.reshape(1, D_MODEL),
    }

    n_mod_rows = 8
    cvec = jnp.concatenate([c_ctx[None], c, jnp.zeros((n_mod_rows - 1 - dec_batch, D_MODEL), F32)], axis=0)
    mod3 = _modulation(cvec, w_ada[l], b_ada[l]).reshape(n_mod_rows, 6, D_MODEL)

    y_prompt, (new_a, new_b) = _stream(x_prompt, mod3, lambda i, tm: 0, None, None, p, latent=False)
    y_sample, _ = _stream(x_sample, mod3, lambda i, tm: 1 + i // (dec_seq // tm), state_hgrn[:, l],
                          state_gdn[:, l], p, latent=True)
    return y_prompt, y_sample, new_a[:, None], new_b[:, None]
```

```python
import functools

import jax
import jax.numpy as jnp
from jax import lax
from jax.experimental import pallas as pl
from jax.experimental.pallas import tpu as pltpu

F32 = jnp.float32
BF16 = jnp.bfloat16

D_MODEL = 1024
N_HEADS = 4
D_HEAD = 128
W_GROUP = N_HEADS * D_HEAD
CHUNK = 64
GRID_W = 64
CONV_W = 3
EPS = 1e-6
N_MAIN_GROUPS = 9
GATE_LANES = 128
VMEM_LIMIT = 56 * 1024 * 1024


NEG_LOG2_E = -1.4426950408889634


def _sigmoid(x):
    return 1.0 / (1.0 + jnp.exp2(x * NEG_LOG2_E))


def _silu(x):
    return x * _sigmoid(x)


def _dot(a, b):
    return jnp.dot(a.astype(BF16), b.astype(BF16), preferred_element_type=F32)


def _dot_nt(a, b):
    return lax.dot_general(a.astype(BF16), b.astype(BF16), (((1,), (1,)), ((), ())),
                           preferred_element_type=F32)


def _split3(x):
    x1 = x.astype(BF16)
    r = x - x1.astype(F32)
    x2 = r.astype(BF16)
    x3 = (r - x2.astype(F32)).astype(BF16)
    return x1, x2, x3


def _cumsum_rows(tri3, x):
    return jnp.dot(tri3, jnp.concatenate(_split3(x), axis=0), preferred_element_type=F32)


def _tri3(direction):
    tri = _tri_masks(direction)[0].astype(BF16)
    return jnp.concatenate([tri, tri, tri], axis=1)


def _tri_masks(direction):
    r = lax.broadcasted_iota(jnp.int32, (CHUNK, CHUNK), 0)
    c = lax.broadcasted_iota(jnp.int32, (CHUNK, CHUNK), 1)
    if direction == 0:
        return c <= r, c < r
    return c >= r, c > r


def _resident(shape):
    return pl.BlockSpec(shape, lambda i: (0,) * len(shape), pipeline_mode=pl.Buffered(1))


def _mod_kernel(c_ref, w_ref, b_ref, o_ref):
    s = _silu(c_ref[...])
    o_ref[...] = _dot(s, w_ref[...]) + b_ref[...]


def _modulation(cvec, w_ada, b_ada):
    n_rows, d = cvec.shape
    n_out = w_ada.shape[1]
    tn = 1536
    return pl.pallas_call(
        _mod_kernel,
        grid=(n_out // tn,),
        in_specs=[pl.BlockSpec((n_rows, d), lambda j: (0, 0)),
                  pl.BlockSpec((d, tn), lambda j: (0, j)),
                  pl.BlockSpec((1, tn), lambda j: (0, j))],
        out_specs=pl.BlockSpec((n_rows, tn), lambda j: (0, j)),
        out_shape=jax.ShapeDtypeStruct((n_rows, n_out), F32),
        compiler_params=pltpu.CompilerParams(dimension_semantics=("arbitrary",),
                                             vmem_limit_bytes=VMEM_LIMIT),
        name="mod",
    )(cvec, w_ada, b_ada.reshape(1, n_out))


HALO = GRID_W
LANE = 128
QKV_TILES = 3 * W_GROUP // LANE
ROW_PITCH = GRID_W + 8


def _inproj_kernel(*refs, latent, seq_len):
    it = iter(refs)
    x_ref = next(it)
    xp_ref, xn_ref = (next(it), next(it)) if latent else (None, None)
    mod_ref, n1_ref, lb_ref, gp_ref, cw_ref, w_ref, wg_ref = (next(it) for _ in range(7))
    qa_ref, ff_ref, fb_ref, va_ref, ga_ref, qkv_ref, zb_ref, gates_ref = (next(it) for _ in range(8))
    qkv_scr, gates_scr = (next(it), next(it)) if latent else (None, None)
    tm = x_ref.shape[0]
    m = mod_ref[0]

    norm_scale = n1_ref[...] * (1.0 + m[1:2])

    def normed(x):
        return (x * lax.rsqrt(jnp.mean(x * x, axis=-1, keepdims=True) + EPS) * norm_scale + m[0:1]).astype(BF16)

    hb = normed(x_ref[...])

    w_qkv = w_ref[:, 5 * W_GROUP:8 * W_GROUP]
    if latent:
        i = pl.program_id(0)
        tiles_per_seq = seq_len // tm
        ext = jnp.dot(jnp.concatenate([normed(xp_ref[...]), hb, normed(xn_ref[...])], axis=0), w_qkv,
                      preferred_element_type=F32)
        above = jnp.where(i % tiles_per_seq == 0, 0.0, ext[:HALO])
        below = jnp.where(i % tiles_per_seq == tiles_per_seq - 1, 0.0, ext[HALO + tm:])
        cur = ext[HALO:HALO + tm]
        prev = jnp.concatenate([above, ext[HALO:tm]], axis=0)
        nxt = jnp.concatenate([ext[2 * HALO:HALO + tm], below], axis=0)
    else:
        cur = jnp.dot(hb, w_qkv, preferred_element_type=F32)
        pos = lax.broadcasted_iota(jnp.int32, cur.shape, 0) % seq_len
        prev = jnp.where(pos == 0, 0.0, pltpu.roll(cur, 1, axis=0))
        nxt = jnp.where(pos == seq_len - 1, 0.0, pltpu.roll(cur, tm - 1, axis=0))
    cw = cw_ref[...]
    y = _silu(prev * cw[0:1] + cur * cw[1:2] + nxt * cw[2:3])
    tiles = []
    for j in range(QKV_TILES):
        t = y[:, j * LANE:(j + 1) * LANE]
        if j < 2 * N_HEADS:
            inv = lax.rsqrt(jnp.sum(t * t, axis=-1, keepdims=True) + EPS)
            t = t * (inv * (D_HEAD ** -0.5) if j < N_HEADS else inv)
        tiles.append(t)

    raw = jnp.dot(hb, wg_ref[...], preferred_element_type=F32)
    gp = gp_ref[...]
    z = raw + gp[1:2]
    softplus = jnp.maximum(z, 0.0) + jnp.log(1.0 + jnp.exp(-jnp.abs(z)))
    lane = lax.broadcasted_iota(jnp.int32, raw.shape, 1)
    gates = jnp.where(lane < 2 * N_HEADS, _sigmoid(raw),
                      jnp.where(lane < 4 * N_HEADS, -jnp.exp(gp[0:1]) * softplus, 0.0))

    if not latent:
        for j, t in enumerate(tiles):
            qkv_ref[:, j * LANE:(j + 1) * LANE] = t
        gates_ref[...] = gates
    else:
        n_rows = tm // GRID_W
        for r in range(n_rows):
            rows, dst = slice(r * GRID_W, (r + 1) * GRID_W), slice(r * ROW_PITCH, r * ROW_PITCH + GRID_W)
            for j, t in enumerate(tiles):
                qkv_scr[j, dst, :] = t[rows]
            gates_scr[dst, :] = gates[rows]
        for c in range(GRID_W):
            for j in range(QKV_TILES):
                qkv_ref[0, :, (c * QKV_TILES + j) * LANE:(c * QKV_TILES + j + 1) * LANE] = (
                    qkv_scr[j, pl.ds(c, n_rows, stride=ROW_PITCH), :])
            gates_ref[0, :, c * GATE_LANES:(c + 1) * GATE_LANES] = gates_scr[pl.ds(c, n_rows, stride=ROW_PITCH), :]

    def proj(j):
        return jnp.dot(hb, w_ref[:, j * W_GROUP:(j + 1) * W_GROUP], preferred_element_type=F32)

    lbp = lb_ref[...]
    e = jnp.exp(lbp - jnp.max(lbp, axis=0, keepdims=True))
    lb = e[0:1] / jnp.sum(e, axis=0, keepdims=True)
    lb_f, lb_b = lb[:, :W_GROUP], lb[:, W_GROUP:]

    qa_ref[...] = _silu(proj(0)).astype(BF16)
    ff_ref[...] = lb_f + (1.0 - lb_f) * _sigmoid(proj(1))
    fb_ref[...] = lb_b + (1.0 - lb_b) * _sigmoid(proj(2))
    ga_ref[...] = _silu(proj(4)).astype(BF16)
    zb_ref[...] = _silu(proj(8)).astype(BF16)
    va_ref[...] = proj(3).astype(BF16)


def _inproj(x2d, mod3, mod_row_of_tile, p, tm, latent, seq_len):
    n_tok = x2d.shape[0]
    tok = lambda width: pl.BlockSpec((tm, width), lambda i: (i, 0))
    widths = [W_GROUP] * 5 + [3 * W_GROUP, W_GROUP, GATE_LANES]
    out_specs = [tok(w) for w in widths]
    dtypes = [BF16, F32, F32, BF16, BF16, F32, BF16, F32]
    out_shape = [jax.ShapeDtypeStruct((n_tok, w), dt) for w, dt in zip(widths, dtypes)]
    in_specs, args, scratch = [tok(D_MODEL)], [x2d], []
    if latent:
        n_halo_blocks, per_tile, rows = n_tok // HALO, tm // HALO, tm // GRID_W
        tiles_per_seq = seq_len // tm
        in_specs += [pl.BlockSpec((HALO, D_MODEL), lambda i: (jnp.maximum(i * per_tile - 1, 0), 0)),
                     pl.BlockSpec((HALO, D_MODEL), lambda i: (jnp.minimum((i + 1) * per_tile, n_halo_blocks - 1), 0))]
        args += [x2d, x2d]
        col = lambda width: pl.BlockSpec((1, rows, GRID_W * width),
                                         lambda i: (i // tiles_per_seq, i % tiles_per_seq, 0))
        col_shape = lambda width: jax.ShapeDtypeStruct((n_tok // seq_len, seq_len // GRID_W, GRID_W * width), F32)
        for k, width in ((5, 3 * W_GROUP), (7, GATE_LANES)):
            out_specs[k], out_shape[k] = col(width), col_shape(width)
        scratch = [pltpu.VMEM((QKV_TILES, rows * ROW_PITCH, LANE), F32), pltpu.VMEM((rows * ROW_PITCH, GATE_LANES), F32)]
    consts = [p["norm1"], p["lbp"], p["gparams"], p["conv_w"], p["w_main"], p["w_gates"]]
    in_specs += [pl.BlockSpec((1, 6, D_MODEL), lambda i: (mod_row_of_tile(i), 0, 0))] + [_resident(a.shape) for a in consts]
    return pl.pallas_call(
        functools.partial(_inproj_kernel, latent=latent, seq_len=seq_len),
        grid=(n_tok // tm,),
        in_specs=in_specs, out_specs=out_specs, out_shape=out_shape, scratch_shapes=scratch,
        compiler_params=pltpu.CompilerParams(dimension_semantics=("arbitrary",),
                                             vmem_limit_bytes=VMEM_LIMIT),
        name="inproj",
    )(*args, mod3, *consts)


INVERSE_BASE_BLOCK = 8


def _hgrn_stages(qf_ref, ff_ref, vf_ref, qb_ref, fb_ref, vb_ref, of_ref, ob_ref, s_ref, group):
    pre = []
    for b in range(group):
        for d, f_ref in enumerate((ff_ref, fb_ref)):
            pre.append((b, d, _cumsum_rows(_tri3(d), jnp.log(f_ref[b]))))
    yield
    chains = []
    for b, d, cum_all in pre:
        q_ref, f_ref, v_ref, o_ref = (qf_ref, ff_ref, vf_ref, of_ref) if d == 0 else (qb_ref, fb_ref, vb_ref, ob_ref)
        incl, _ = _tri_masks(d)
        for h in range(N_HEADS):
            sl = slice(h * D_HEAD, (h + 1) * D_HEAD)
            k = 1.0 - f_ref[b, :, sl]
            G = cum_all[:, sl]
            g_last_row = G[CHUNK - 1:CHUNK] if d == 0 else G[0:1]
            chains.append(dict(
                b=b, d=d, h=h, sl=sl, o_ref=o_ref, incl=incl, vb=v_ref[b, :, sl].astype(BF16),
                decay=jnp.exp(jnp.broadcast_to(g_last_row, (8, D_HEAD)).T[:, 0:1]),
                q_dec=(q_ref[b, :, sl].astype(F32) * jnp.exp(G)).astype(BF16), k_dec=k * jnp.exp2(G * NEG_LOG2_E),
                k_tail_t=(k * jnp.exp(g_last_row - G)).T.astype(BF16)))
    attns = [jnp.where(ch["incl"], _dot_nt(ch["q_dec"], ch["k_dec"]), 0.0).astype(BF16) for ch in chains]
    yield
    states = [s_ref[ch["b"], ch["d"], ch["h"]] for ch in chains]
    outs = [_dot(jnp.concatenate([ch["q_dec"], attn], axis=1), jnp.concatenate([s.astype(BF16), ch["vb"]], axis=0))
            for ch, attn, s in zip(chains, attns, states)]
    upds = [_dot(ch["k_tail_t"], ch["vb"]) for ch in chains]
    for ch, o, s, u in zip(chains, outs, states, upds):
        ch["o_ref"][ch["b"], :, ch["sl"]] = o.astype(BF16)
        s_ref[ch["b"], ch["d"], ch["h"]] = ch["decay"] * s + u


PAIR = 2


def _pair_index():
    r = lax.broadcasted_iota(jnp.int32, (CHUNK, PAIR * CHUNK), 0)
    lane = lax.broadcasted_iota(jnp.int32, (CHUNK, PAIR * CHUNK), 1)
    return r, lane % CHUNK, lane // CHUNK


def _block_diag(packed):
    head = lax.broadcasted_iota(jnp.int32, packed.shape, 1) // (packed.shape[1] // PAIR)
    return jnp.concatenate([jnp.where(head == h, packed, jnp.zeros_like(packed)) for h in range(PAIR)], axis=0)


def _unit_triangular_inverses(lowers):
    r, c, _ = _pair_index()
    same_block = lambda block: r // block == c // block
    pdot = lambda a, b: _dot(a, _block_diag(b.astype(BF16)))
    eye = jnp.where(r == c, 1.0, 0.0)
    diag = same_block(INVERSE_BASE_BLOCK)
    ds = [jnp.where(diag, lo, 0.0) for lo in lowers]
    ts = [eye - d for d in ds]
    ps = [pdot(d, d) for d in ds]
    yield
    power = 4
    while power < INVERSE_BASE_BLOCK:
        tps = [pdot(jnp.concatenate([t.astype(BF16), p.astype(BF16)], axis=0), p) for t, p in zip(ts, ps)]
        yield
        ts = [t + tp[:CHUNK] for t, tp in zip(ts, tps)]
        ps = [tp[CHUNK:] for tp in tps]
        power *= 2
    ts = [t + pdot(t, p) for t, p in zip(ts, ps)]
    yield
    block = INVERSE_BASE_BLOCK
    while block < CHUNK:
        off_mask = same_block(2 * block) & jnp.logical_not(same_block(block))
        ws = [pdot(t, jnp.where(off_mask, lo, 0.0)) for t, lo in zip(ts, lowers)]
        yield
        ts = [t - pdot(w, t) for t, w in zip(ts, ws)]
        yield
        block *= 2
    return ts


def _gdn_stages(xf_ref, gf_ref, xb_ref, gb_ref, of_ref, ob_ref, s_ref, group):
    r, c, head = _pair_index()
    lane_head = lax.broadcasted_iota(jnp.int32, (CHUNK, PAIR * D_HEAD), 1) // D_HEAD
    pairs = []
    for b in range(group):
        for d, (x_ref, g_ref) in enumerate(((xf_ref, gf_ref), (xb_ref, gb_ref))):
            gates = g_ref[b]
            cum = _cumsum_rows(_tri3(d), gates)
            for hp in range(N_HEADS // PAIR):
                hs = [PAIR * hp + i for i in range(PAIR)]
                jb = [d * N_HEADS + h for h in hs]
                part = lambda k: x_ref[b, :, (k * N_HEADS + hs[0]) * D_HEAD:(k * N_HEADS + hs[0] + PAIR) * D_HEAD]
                q2, k2, v2 = part(0), part(1), part(2)
                beta2 = jnp.where(lane_head == 0, gates[:, jb[0]:jb[0] + 1], gates[:, jb[1]:jb[1] + 1])
                k_beta2 = k2 * beta2
                kq = _dot_nt(jnp.concatenate([k_beta2.astype(BF16), q2.astype(BF16)], axis=0),
                             _block_diag(k2.astype(BF16)))
                pairs.append(dict(b=b, d=d, hs=hs, q2=q2, k2=k2, v2=v2, beta2=beta2, k_beta2=k_beta2, cum=cum, kq=kq))
    yield
    cum_ts = {}
    for pr in pairs:
        d, cum = pr["d"], pr["cum"]
        incl, pr["strict"] = (c <= r, c < r) if d == 0 else (c >= r, c > r)
        if (pr["b"], d) not in cum_ts:
            cum_ts[pr["b"], d] = jnp.concatenate([cum] * PAIR, axis=0).T
        cum_t = cum_ts[pr["b"], d]
        jg = [2 * N_HEADS + d * N_HEADS + h for h in pr["hs"]]
        pr["g_cum"] = [cum[:, j:j + 1] for j in jg]
        g_col = jnp.where(head == 0, pr["g_cum"][0], pr["g_cum"][1])
        g_row = jnp.where(head[0:1] == 0, cum_t[jg[0]:jg[0] + 1, :], cum_t[jg[1]:jg[1] + 1, :])
        diff = g_col - g_row
        pr["decay_mask"] = jnp.where(incl, jnp.exp(jnp.where(incl, diff, 0.0)), 0.0)
    kqs = [pr["kq"] for pr in pairs]

    lowers = [jnp.where(pr["strict"], kq[:CHUNK] * pr["decay_mask"], 0.0) for pr, kq in zip(pairs, kqs)]
    ts = yield from _unit_triangular_inverses(lowers)
    for pr, kq in zip(pairs, kqs):
        e_g = [jnp.exp(g) for g in pr["g_cum"]]
        e_g2 = jnp.where(lane_head == 0, e_g[0], e_g[1])
        pr["rhs"] = jnp.concatenate(
            [jnp.concatenate([(pr["v2"] * pr["beta2"])[:, i * D_HEAD:(i + 1) * D_HEAD],
                              (pr["k_beta2"] * e_g2)[:, i * D_HEAD:(i + 1) * D_HEAD]], axis=1) for i in range(PAIR)],
            axis=0).astype(BF16)
        attn2 = (kq[CHUNK:] * pr["decay_mask"]).astype(BF16)
        pr["attn"] = [attn2[:, i * CHUNK:(i + 1) * CHUNK] for i in range(PAIR)]
        g_last = [g[CHUNK - 1:CHUNK] if pr["d"] == 0 else g[0:1] for g in pr["g_cum"]]
        q_dec2 = (pr["q2"] * e_g2).astype(BF16)
        pr["q_dec"] = [q_dec2[:, i * D_HEAD:(i + 1) * D_HEAD] for i in range(PAIR)]
        pr["k_tail_t"] = [(pr["k2"][:, i * D_HEAD:(i + 1) * D_HEAD] * jnp.exp(g_last[i] - pr["g_cum"][i])).T.astype(BF16)
                          for i in range(PAIR)]
        pr["decay"] = [jnp.exp(g) for g in g_last]
    xs = [_dot(_block_diag(t.astype(BF16)), pr["rhs"]) for pr, t in zip(pairs, ts)]
    yield
    chains = []
    for pr, x in zip(pairs, xs):
        for i, h in enumerate(pr["hs"]):
            xi = x[i * CHUNK:(i + 1) * CHUNK]
            chains.append(dict(b=pr["b"], d=pr["d"], h=h, u=xi[:, :D_HEAD], w=xi[:, D_HEAD:].astype(BF16),
                               attn=pr["attn"][i], q_dec=pr["q_dec"][i], k_tail_t=pr["k_tail_t"][i],
                               decay=pr["decay"][i]))

    o_refs = (of_ref, ob_ref)
    states = [s_ref[ch["b"], ch["d"], ch["h"]] for ch in chains]
    sbs = [s.astype(BF16) for s in states]
    wqs = [_dot(jnp.concatenate([ch["w"], ch["q_dec"]], axis=0), sb) for ch, sb in zip(chains, sbs)]
    yield
    vbs = [(ch["u"] - wq[:CHUNK]).astype(BF16) for ch, wq in zip(chains, wqs)]
    outs = [wq[CHUNK:] + _dot(ch["attn"], vb) for ch, wq, vb in zip(chains, wqs, vbs)]
    yield
    upd = [_dot(ch["k_tail_t"], vb) for ch, vb in zip(chains, vbs)]
    for ch, s, u, o in zip(chains, states, upd, outs):
        o_refs[ch["d"]][ch["b"], :, ch["h"] * D_HEAD:(ch["h"] + 1) * D_HEAD] = o.astype(of_ref.dtype)
        s_ref[ch["b"], ch["d"], ch["h"]] = s * ch["decay"] + u


def _run_interleaved(primary, secondary, every):
    live = [primary, secondary]
    count = 0
    while live:
        gen = primary if primary in live and (secondary not in live or count < every) else secondary
        count = count + 1 if gen is primary else 0
        try:
            next(gen)
        except StopIteration:
            live.remove(gen)


def _scan_kernel(*refs, n_steps, group, hgrn, gdn, has_s0, emit_state):
    it = iter(refs)
    take = lambda k: [next(it) for _ in range(k)]
    h_in = take(6) if hgrn else None
    g_in = take(4) if gdn else None
    s0 = take(int(hgrn) + int(gdn)) if has_s0 else None
    h_out = take(2) if hgrn else None
    g_out = take(2) if gdn else None
    st = take(int(hgrn) + int(gdn)) if emit_state else None
    s_refs = take(int(hgrn) + int(gdn))
    n = pl.program_id(1)

    @pl.when(n == 0)
    def _():
        for k, s_ref in enumerate(s_refs):
            s_ref[...] = s0[k][...] if has_s0 else jnp.zeros_like(s_ref)

    gens = []
    if gdn:
        gens.append(_gdn_stages(*g_in, *g_out, s_refs[-1], group))
    if hgrn:
        gens.append(_hgrn_stages(*h_in, *h_out, s_refs[0], group))
    if len(gens) == 2:
        _run_interleaved(gens[0], gens[1], every=1)
    else:
        for _ in gens[0]:
            pass

    if emit_state:
        @pl.when(n == n_steps - 1)
        def _():
            for st_ref, s_ref in zip(st, s_refs):
                st_ref[...] = s_ref[...]


def _scan(hgrn_args, gdn_args, states0, batch, group, n_steps, gdn_columns, emit_state, gdn_out_dtype):
    state_shape = (group, 2, N_HEADS, D_HEAD, D_HEAD)
    state_spec = pl.BlockSpec(state_shape, lambda b, n: (b, 0, 0, 0, 0))
    in_specs, args, out_specs, out_shape = [], [], [], []
    if hgrn_args is not None:
        qa, f_fwd, f_bwd, va = hgrn_args
        fwd = pl.BlockSpec((group, CHUNK, W_GROUP), lambda b, n: (b, n, 0))
        bwd = pl.BlockSpec((group, CHUNK, W_GROUP), lambda b, n: (b, n_steps - 1 - n, 0))
        in_specs += [fwd, fwd, fwd, bwd, bwd, bwd]
        args += [qa, f_fwd, va, qa, f_bwd, va]
        out_specs += [fwd, bwd]
        out_shape += [jax.ShapeDtypeStruct(qa.shape, BF16)] * 2
    if gdn_args is not None:
        qkv3, gates3 = gdn_args
        columns = gdn_columns
        fwd_map = (lambda b, n: (b, n, 0)) if columns == 1 else (lambda b, n: (b, 0, n))
        bwd_map = lambda b, n: fwd_map(b, n_steps - 1 - n)
        spec = lambda width, imap: pl.BlockSpec((group, CHUNK, width), imap)
        in_specs += [spec(3 * W_GROUP, fwd_map), spec(GATE_LANES, fwd_map),
                     spec(3 * W_GROUP, bwd_map), spec(GATE_LANES, bwd_map)]
        args += [qkv3, gates3, qkv3, gates3]
        out_specs += [spec(W_GROUP, fwd_map), spec(W_GROUP, bwd_map)]
        out_shape += [jax.ShapeDtypeStruct((batch, qkv3.shape[1], columns * W_GROUP), gdn_out_dtype)] * 2
    n_mixers = int(hgrn_args is not None) + int(gdn_args is not None)
    if states0 is not None:
        in_specs, args = in_specs + [state_spec] * n_mixers, args + list(states0)
    if emit_state:
        out_specs = out_specs + [state_spec] * n_mixers
        out_shape = out_shape + [jax.ShapeDtypeStruct((batch,) + state_shape[1:], F32)] * n_mixers
    return pl.pallas_call(
        functools.partial(_scan_kernel, n_steps=n_steps, group=group, hgrn=hgrn_args is not None,
                          gdn=gdn_args is not None, has_s0=states0 is not None, emit_state=emit_state),
        grid=(batch // group, n_steps),
        in_specs=in_specs, out_specs=out_specs, out_shape=out_shape,
        scratch_shapes=[pltpu.VMEM(state_shape, F32)] * n_mixers,
        compiler_params=pltpu.CompilerParams(dimension_semantics=("arbitrary", "arbitrary"),
                                             vmem_limit_bytes=VMEM_LIMIT),
        name="scan",
    )(*args)


FFN_SPLIT = 2
MXU_K_TILE = 256


def _ffn_pieces(d_ff):
    k_tiles = -(-d_ff // MXU_K_TILE)
    bounds = [min(d_ff, MXU_K_TILE * -(-k_tiles * s // FFN_SPLIT)) for s in range(FFN_SPLIT + 1)]
    return list(zip(bounds[:-1], bounds[1:]))


def _tail_kernel(*refs, latent):
    it = iter(refs)
    (x_ref, oaf_ref, oab_ref, obf_ref, obb_ref, ga_ref, zb_ref, mod_ref, na_ref, nb_ref, wo_ref, n2_ref,
     wg_ref, wu_ref, wd_ref, nf_ref) = (next(it) for _ in range(16))
    y_ref = next(it)
    ob_scr = next(it) if latent else None
    tm = x_ref.shape[0]

    def gated_norm(o, w_ref, gate_ref, h):
        sl = slice(h * D_HEAD, (h + 1) * D_HEAD)
        o = o * lax.rsqrt(jnp.mean(o * o, axis=-1, keepdims=True) + EPS)
        return (o * w_ref[:, sl] * gate_ref[:, sl].astype(F32)).astype(BF16)

    def both(f_ref, b_ref, h):
        sl = slice(h * D_HEAD, (h + 1) * D_HEAD)
        return f_ref[:, sl].astype(F32) + b_ref[:, sl].astype(F32)

    if latent:
        n_rows = tm // GRID_W
        for c in range(GRID_W):
            for h in range(N_HEADS):
                sl = slice(c * W_GROUP + h * D_HEAD, c * W_GROUP + (h + 1) * D_HEAD)
                ob_scr[h, pl.ds(c, n_rows, stride=ROW_PITCH), :] = obf_ref[0, :, sl] + obb_ref[0, :, sl]
        o_b = [jnp.concatenate([ob_scr[h, r * ROW_PITCH:r * ROW_PITCH + GRID_W, :] for r in range(n_rows)], axis=0)
               for h in range(N_HEADS)]
    else:
        o_b = [both(obf_ref, obb_ref, h) for h in range(N_HEADS)]
    o_a = [both(oaf_ref, oab_ref, h) for h in range(N_HEADS)]
    mixed = jnp.concatenate([gated_norm(o, na_ref, ga_ref, h) for h, o in enumerate(o_a)]
                            + [gated_norm(o, nb_ref, zb_ref, h) for h, o in enumerate(o_b)], axis=-1)
    m = mod_ref[0]
    x1 = x_ref[...] + m[2:3] * jnp.dot(mixed, wo_ref[...], preferred_element_type=F32)
    y = x1 * lax.rsqrt(jnp.mean(x1 * x1, axis=-1, keepdims=True) + EPS) * n2_ref[...]
    h2 = (y * (1.0 + m[4:5]) + m[3:4]).astype(BF16)

    ff = None
    for lo, hi in _ffn_pieces(wg_ref.shape[1]):
        cols = slice(lo, hi)
        gate = jnp.dot(h2, wg_ref[:, cols], preferred_element_type=F32)
        up = jnp.dot(h2, wu_ref[:, cols], preferred_element_type=F32)
        part = jnp.dot((_silu(gate) * up).astype(BF16), wd_ref[cols, :], preferred_element_type=F32)
        ff = part if ff is None else ff + part
    x2 = x1 + m[5:6] * ff
    y_ref[...] = x2 * lax.rsqrt(jnp.mean(x2 * x2, axis=-1, keepdims=True) + EPS) * nf_ref[...]


def _tail(x2d, oaf, oab, obf, obb, ga, zb, mod3, mod_row_of_tile, p, tm, latent, seq_len):
    n_tok = x2d.shape[0]
    assert p["w_gate"].shape[1] % LANE == 0
    tok = lambda width: pl.BlockSpec((tm, width), lambda i: (i, 0))
    ob_spec, scratch = tok(W_GROUP), []
    if latent:
        tiles_per_seq = seq_len // tm
        ob_spec = pl.BlockSpec((1, tm // GRID_W, GRID_W * W_GROUP), lambda i: (i // tiles_per_seq, i % tiles_per_seq, 0))
        scratch = [pltpu.VMEM((N_HEADS, tm // GRID_W * ROW_PITCH, D_HEAD), F32)]
    consts = [p["norm_a"], p["norm_b"], p["w_out"], p["norm2"], p["w_gate"], p["w_up"], p["w_down"], p["norm_f"]]
    return pl.pallas_call(
        functools.partial(_tail_kernel, latent=latent),
        grid=(n_tok // tm,),
        in_specs=[tok(D_MODEL), tok(W_GROUP), tok(W_GROUP), ob_spec, ob_spec, tok(W_GROUP), tok(W_GROUP),
                  pl.BlockSpec((1, 6, D_MODEL), lambda i: (mod_row_of_tile(i), 0, 0))]
                 + [_resident(a.shape) for a in consts],
        out_specs=tok(D_MODEL),
        out_shape=jax.ShapeDtypeStruct((n_tok, D_MODEL), F32),
        scratch_shapes=scratch,
        compiler_params=pltpu.CompilerParams(dimension_semantics=("arbitrary",),
                                             vmem_limit_bytes=VMEM_LIMIT),
        name="tail",
    )(x2d, oaf, oab, obf, obb, ga, zb, mod3, *consts)


def _stream(x, mod3, mod_row_of_tile, s0_a, s0_b, p, latent):
    batch, seq, _ = x.shape
    x2d = x.reshape(batch * seq, D_MODEL)
    tm = 512
    mod_row = functools.partial(mod_row_of_tile, tm=tm)
    qa, f_fwd, f_bwd, va, ga, qkv, zb, gates = _inproj(x2d, mod3, mod_row, p, tm, latent, seq)
    hgrn_args = [a.reshape(batch, seq, W_GROUP) for a in (qa, f_fwd, f_bwd, va)]
    n_chunks = seq // CHUNK
    if latent:
        assert seq // GRID_W == CHUNK and n_chunks == GRID_W
        oaf, oab, obf, obb = _scan(hgrn_args, (qkv, gates), (s0_a, s0_b), batch, group=4, n_steps=n_chunks,
                                   gdn_columns=GRID_W, emit_state=False, gdn_out_dtype=F32)
        states = (None, None)
    else:
        gdn_args = (qkv.reshape(batch, seq, 3 * W_GROUP), gates.reshape(batch, seq, GATE_LANES))
        oaf, oab, obf, obb, new_a, new_b = _scan(hgrn_args, gdn_args, None, batch, group=4, n_steps=n_chunks,
                                                 gdn_columns=1, emit_state=True, gdn_out_dtype=BF16)
        obf, obb = (o.reshape(batch * seq, W_GROUP) for o in (obf, obb))
        states = (new_a, new_b)
    oaf, oab = (o.reshape(batch * seq, W_GROUP) for o in (oaf, oab))
    y = _tail(x2d, oaf, oab, obf, obb, ga, zb, mod3, mod_row, p, tm, latent, seq)
    return y.reshape(batch, seq, D_MODEL), states


def kernel(x_prompt, x_sample, c, state_hgrn, state_gdn, c_ctx, w_ada, b_ada, norm1, norm2, w_in, conv_w,
           hgrn_lb, gdn_A_log, gdn_dt_bias, hgrn_out_norm, gdn_out_norm, w_out, w_gate, w_up, w_down, norm_f):
    depth = w_in.shape[0]
    assert depth == 1 and hgrn_lb.shape[0] == 2 and conv_w.shape[1] == CONV_W
    dec_batch, dec_seq, _ = x_sample.shape
    l = 0

    n_main = N_MAIN_GROUPS * W_GROUP
    pad8 = jnp.zeros((1, 2 * N_HEADS), F32)
    gparams = jnp.concatenate(
        [jnp.concatenate([pad8, a.reshape(1, 2 * N_HEADS).astype(F32),
                          jnp.zeros((1, GATE_LANES - 4 * N_HEADS), F32)], axis=1)
         for a in (gdn_A_log[l], gdn_dt_bias[l])], axis=0)
    p = {
        "norm1": norm1[l].reshape(1, D_MODEL), "norm2": norm2[l].reshape(1, D_MODEL),
        "lbp": hgrn_lb.reshape(2, 2 * W_GROUP), "gparams": gparams,
        "w_main": w_in[l].astype(BF16),
        "w_gates": jnp.pad(w_in[l][:, n_main:], ((0, 0), (0, GATE_LANES - 4 * N_HEADS))).astype(BF16),
        "conv_w": conv_w[l],
        "norm_a": hgrn_out_norm[l].reshape(1, W_GROUP), "norm_b": gdn_out_norm[l].reshape(1, W_GROUP),
        "w_out": w_out[l].astype(BF16), "w_gate": w_gate[l].astype(BF16), "w_up": w_up[l].astype(BF16),
        "w_down": w_down[l].astype(BF16), "norm_f": norm_f.reshape(1, D_MODEL),
    }

    n_mod_rows = 8
    cvec = jnp.concatenate([c_ctx[None], c, jnp.zeros((n_mod_rows - 1 - dec_batch, D_MODEL), F32)], axis=0)
    mod3 = _modulation(cvec, w_ada[l], b_ada[l]).reshape(n_mod_rows, 6, D_MODEL)

    y_prompt, (new_a, new_b) = _stream(x_prompt, mod3, lambda i, tm: 0, None, None, p, latent=False)
    y_sample, _ = _stream(x_sample, mod3, lambda i, tm: 1 + i // (dec_seq // tm), state_hgrn[:, l],
                          state_gdn[:, l], p, latent=True)
    return y_prompt, y_sample, new_a[:, None], new_b[:, None]
```

```python
import functools

import jax
import jax.numpy as jnp
from jax import lax
from jax.experimental import pallas as pl
from jax.experimental.pallas import tpu as pltpu

F32 = jnp.float32
BF16 = jnp.bfloat16

D_MODEL = 1024
N_HEADS = 4
D_HEAD = 128
W_GROUP = N_HEADS * D_HEAD
CHUNK = 64
GRID_W = 64
CONV_W = 3
EPS = 1e-6
N_MAIN_GROUPS = 9
GATE_LANES = 128
SUBLANES = 8
VMEM_LIMIT = 56 * 1024 * 1024


NEG_LOG2_E = -1.4426950408889634


def _sigmoid(x):
    return 1.0 / (1.0 + jnp.exp2(x * NEG_LOG2_E))


def _silu(x):
    return x * _sigmoid(x)


def _dot(a, b):
    return jnp.dot(a.astype(BF16), b.astype(BF16), preferred_element_type=F32)


def _dot_nt(a, b):
    return lax.dot_general(a.astype(BF16), b.astype(BF16), (((1,), (1,)), ((), ())),
                           preferred_element_type=F32)


def _split3(x):
    x1 = x.astype(BF16)
    r = x - x1.astype(F32)
    x2 = r.astype(BF16)
    x3 = (r - x2.astype(F32)).astype(BF16)
    return x1, x2, x3


def _cumsum_rows(tri3, x):
    return jnp.dot(tri3, jnp.concatenate(_split3(x), axis=0), preferred_element_type=F32)


def _tri3(direction):
    tri = _tri_masks(direction)[0].astype(BF16)
    return jnp.concatenate([tri, tri, tri], axis=1)


def _tri_masks(direction):
    r = lax.broadcasted_iota(jnp.int32, (CHUNK, CHUNK), 0)
    c = lax.broadcasted_iota(jnp.int32, (CHUNK, CHUNK), 1)
    if direction == 0:
        return c <= r, c < r
    return c >= r, c > r


def _resident(shape):
    return pl.BlockSpec(shape, lambda i: (0,) * len(shape), pipeline_mode=pl.Buffered(1))


def _mod_kernel(c_ref, w_ref, b_ref, o_ref):
    s = _silu(c_ref[...])
    o_ref[...] = _dot(s, w_ref[...]) + b_ref[...]


def _modulation(cvec, w_ada, b_ada):
    n_rows, d = cvec.shape
    n_out = w_ada.shape[1]
    tn = 1536
    return pl.pallas_call(
        _mod_kernel,
        grid=(n_out // tn,),
        in_specs=[pl.BlockSpec((n_rows, d), lambda j: (0, 0)),
                  pl.BlockSpec((d, tn), lambda j: (0, j)),
                  pl.BlockSpec((1, tn), lambda j: (0, j))],
        out_specs=pl.BlockSpec((n_rows, tn), lambda j: (0, j)),
        out_shape=jax.ShapeDtypeStruct((n_rows, n_out), F32),
        compiler_params=pltpu.CompilerParams(dimension_semantics=("arbitrary",),
                                             vmem_limit_bytes=VMEM_LIMIT),
        name="mod",
    )(cvec, w_ada, b_ada.reshape(1, n_out))


HALO = GRID_W
LANE = 128
QKV_TILES = 3 * W_GROUP // LANE
ROW_PITCH = GRID_W + 8


def _inproj_kernel(*refs, latent, seq_len):
    it = iter(refs)
    x_ref = next(it)
    xp_ref, xn_ref = (next(it), next(it)) if latent else (None, None)
    mod_ref, n1_ref, lb_ref, gp_ref, cw_ref, w_ref, wg_ref = (next(it) for _ in range(7))
    b16_ref, f32_ref, qkv_ref, gates_ref = (next(it) for _ in range(4))
    qkv_scr, gates_scr = (next(it), next(it)) if latent else (None, None)
    tm = x_ref.shape[0]
    m = mod_ref[0]

    norm_scale = n1_ref[...] * (1.0 + m[1:2])

    def normed(x):
        return (x * lax.rsqrt(jnp.mean(x * x, axis=-1, keepdims=True) + EPS) * norm_scale + m[0:1]).astype(BF16)

    hb = normed(x_ref[...])

    w_qkv = w_ref[:, 5 * W_GROUP:8 * W_GROUP]
    if latent:
        i = pl.program_id(0)
        tiles_per_seq = seq_len // tm
        ext = jnp.dot(jnp.concatenate([normed(xp_ref[...]), hb, normed(xn_ref[...])], axis=0), w_qkv,
                      preferred_element_type=F32)
        above = jnp.where(i % tiles_per_seq == 0, 0.0, ext[:HALO])
        below = jnp.where(i % tiles_per_seq == tiles_per_seq - 1, 0.0, ext[HALO + tm:])
        cur = ext[HALO:HALO + tm]
        prev = jnp.concatenate([above, ext[HALO:tm]], axis=0)
        nxt = jnp.concatenate([ext[2 * HALO:HALO + tm], below], axis=0)
    else:
        cur = jnp.dot(hb, w_qkv, preferred_element_type=F32)
        pos = lax.broadcasted_iota(jnp.int32, cur.shape, 0) % seq_len
        prev = jnp.where(pos == 0, 0.0, pltpu.roll(cur, 1, axis=0))
        nxt = jnp.where(pos == seq_len - 1, 0.0, pltpu.roll(cur, tm - 1, axis=0))
    cw = cw_ref[...]
    y = _silu(prev * cw[0:1] + cur * cw[1:2] + nxt * cw[2:3])
    tiles = []
    for j in range(QKV_TILES):
        t = y[:, j * LANE:(j + 1) * LANE]
        if j < 2 * N_HEADS:
            inv = lax.rsqrt(jnp.sum(t * t, axis=-1, keepdims=True) + EPS)
            t = t * (inv * (D_HEAD ** -0.5) if j < N_HEADS else inv)
        tiles.append(t)

    raw = jnp.dot(hb, wg_ref[...], preferred_element_type=F32)
    gp = gp_ref[...]
    z = raw + gp[1:2]
    softplus = jnp.maximum(z, 0.0) + jnp.log(1.0 + jnp.exp(-jnp.abs(z)))
    lane = lax.broadcasted_iota(jnp.int32, raw.shape, 1)
    gates = jnp.where(lane < 2 * N_HEADS, _sigmoid(raw),
                      jnp.where(lane < 4 * N_HEADS, -jnp.exp(gp[0:1]) * softplus, 0.0))

    if not latent:
        for j, t in enumerate(tiles):
            qkv_ref[:, j * LANE:(j + 1) * LANE] = t
        gates_ref[...] = gates
    else:
        n_rows = tm // GRID_W
        for r in range(n_rows):
            rows, dst = slice(r * GRID_W, (r + 1) * GRID_W), slice(r * ROW_PITCH, r * ROW_PITCH + GRID_W)
            for j, t in enumerate(tiles):
                qkv_scr[j, dst, :] = t[rows]
            gates_scr[dst, :] = gates[rows]
        for c in range(GRID_W):
            for j in range(QKV_TILES):
                qkv_ref[0, :, (c * QKV_TILES + j) * LANE:(c * QKV_TILES + j + 1) * LANE] = (
                    qkv_scr[j, pl.ds(c, n_rows, stride=ROW_PITCH), :])
            gates_ref[0, :, c * GATE_LANES:(c + 1) * GATE_LANES] = gates_scr[pl.ds(c, n_rows, stride=ROW_PITCH), :]

    def proj(j):
        return jnp.dot(hb, w_ref[:, j * W_GROUP:(j + 1) * W_GROUP], preferred_element_type=F32)

    lbp = lb_ref[...]
    e = jnp.exp(lbp - jnp.max(lbp, axis=0, keepdims=True))
    lb = e[0:1] / jnp.sum(e, axis=0, keepdims=True)
    lb_f, lb_b = lb[:, :W_GROUP], lb[:, W_GROUP:]

    col = lambda j: slice(j * W_GROUP, (j + 1) * W_GROUP)
    b16_ref[:, col(0)] = _silu(proj(0)).astype(BF16)
    f32_ref[:, col(0)] = lb_f + (1.0 - lb_f) * _sigmoid(proj(1))
    f32_ref[:, col(1)] = lb_b + (1.0 - lb_b) * _sigmoid(proj(2))
    b16_ref[:, col(2)] = _silu(proj(4)).astype(BF16)
    b16_ref[:, col(3)] = _silu(proj(8)).astype(BF16)
    b16_ref[:, col(1)] = proj(3).astype(BF16)


def _inproj(x2d, mod3, mod_row_of_tile, p, tm, latent, seq_len):
    n_tok = x2d.shape[0]
    tok = lambda width: pl.BlockSpec((tm, width), lambda i: (i, 0))
    widths = [4 * W_GROUP, 2 * W_GROUP, 3 * W_GROUP, GATE_LANES]
    out_specs = [tok(w) for w in widths]
    dtypes = [BF16, F32, F32, F32]
    out_shape = [jax.ShapeDtypeStruct((n_tok, w), dt) for w, dt in zip(widths, dtypes)]
    in_specs, args, scratch = [tok(D_MODEL)], [x2d], []
    if latent:
        n_halo_blocks, per_tile, rows = n_tok // HALO, tm // HALO, tm // GRID_W
        tiles_per_seq = seq_len // tm
        in_specs += [pl.BlockSpec((HALO, D_MODEL), lambda i: (jnp.maximum(i * per_tile - 1, 0), 0)),
                     pl.BlockSpec((HALO, D_MODEL), lambda i: (jnp.minimum((i + 1) * per_tile, n_halo_blocks - 1), 0))]
        args += [x2d, x2d]
        col = lambda width: pl.BlockSpec((1, rows, GRID_W * width),
                                         lambda i: (i // tiles_per_seq, i % tiles_per_seq, 0))
        col_shape = lambda width: jax.ShapeDtypeStruct((n_tok // seq_len, seq_len // GRID_W, GRID_W * width), F32)
        for k, width in ((2, 3 * W_GROUP), (3, GATE_LANES)):
            out_specs[k], out_shape[k] = col(width), col_shape(width)
        scratch = [pltpu.VMEM((QKV_TILES, rows * ROW_PITCH, LANE), F32), pltpu.VMEM((rows * ROW_PITCH, GATE_LANES), F32)]
    consts = [p["norm1"], p["lbp"], p["gparams"], p["conv_w"], p["w_main"], p["w_gates"]]
    in_specs += [pl.BlockSpec((1, 6, D_MODEL), lambda i: (mod_row_of_tile(i), 0, 0))] + [_resident(a.shape) for a in consts]
    return pl.pallas_call(
        functools.partial(_inproj_kernel, latent=latent, seq_len=seq_len),
        grid=(n_tok // tm,),
        in_specs=in_specs, out_specs=out_specs, out_shape=out_shape, scratch_shapes=scratch,
        compiler_params=pltpu.CompilerParams(dimension_semantics=("arbitrary",),
                                             vmem_limit_bytes=VMEM_LIMIT),
        name="inproj",
    )(*args, mod3, *consts)


INVERSE_BASE_BLOCK = 8


def _hgrn_stages(qvf_ref, ff_ref, qvb_ref, fb_ref, of_ref, ob_ref, s_ref, group):
    pre = []
    for b in range(group):
        for d, f_ref in enumerate((ff_ref, fb_ref)):
            pre.append((b, d, _cumsum_rows(_tri3(d), jnp.log(f_ref[b]))))
    yield
    chains = []
    for b, d, cum_all in pre:
        qv_ref, f_ref, o_ref = (qvf_ref, ff_ref, of_ref) if d == 0 else (qvb_ref, fb_ref, ob_ref)
        incl, _ = _tri_masks(d)
        for h in range(N_HEADS):
            sl = slice(h * D_HEAD, (h + 1) * D_HEAD)
            k = 1.0 - f_ref[b, :, sl]
            G = cum_all[:, sl]
            g_last_row = G[CHUNK - 1:CHUNK] if d == 0 else G[0:1]
            chains.append(dict(
                b=b, d=d, h=h, sl=sl, o_ref=o_ref, incl=incl,
                vb=qv_ref[b, :, W_GROUP + h * D_HEAD:W_GROUP + (h + 1) * D_HEAD].astype(BF16),
                decay=jnp.exp(jnp.broadcast_to(g_last_row, (SUBLANES, D_HEAD)).T[:, 0:1]),
                q_dec=(qv_ref[b, :, sl].astype(F32) * jnp.exp(G)).astype(BF16), k_dec=k * jnp.exp2(G * NEG_LOG2_E),
                k_tail_t=(k * jnp.exp(g_last_row - G)).T.astype(BF16)))
    attns = [jnp.where(ch["incl"], _dot_nt(ch["q_dec"], ch["k_dec"]), 0.0).astype(BF16) for ch in chains]
    yield
    states = [s_ref[ch["b"], ch["d"], ch["h"]] for ch in chains]
    outs = [_dot(jnp.concatenate([ch["q_dec"], attn], axis=1), jnp.concatenate([s.astype(BF16), ch["vb"]], axis=0))
            for ch, attn, s in zip(chains, attns, states)]
    upds = [_dot(ch["k_tail_t"], ch["vb"]) for ch in chains]
    for ch, o, s, u in zip(chains, outs, states, upds):
        ch["o_ref"][ch["b"], :, ch["sl"]] = o.astype(BF16)
        s_ref[ch["b"], ch["d"], ch["h"]] = ch["decay"] * s + u


PAIR = 2


def _pair_index():
    r = lax.broadcasted_iota(jnp.int32, (CHUNK, PAIR * CHUNK), 0)
    lane = lax.broadcasted_iota(jnp.int32, (CHUNK, PAIR * CHUNK), 1)
    return r, lane % CHUNK, lane // CHUNK


def _block_diag(packed):
    head = lax.broadcasted_iota(jnp.int32, packed.shape, 1) // (packed.shape[1] // PAIR)
    return jnp.concatenate([jnp.where(head == h, packed, jnp.zeros_like(packed)) for h in range(PAIR)], axis=0)


def _unit_triangular_inverses(lowers):
    r, c, _ = _pair_index()
    same_block = lambda block: r // block == c // block
    pdot = lambda a, b: _dot(a, _block_diag(b.astype(BF16)))
    eye = jnp.where(r == c, 1.0, 0.0)
    diag = same_block(INVERSE_BASE_BLOCK)
    ds = [jnp.where(diag, lo, 0.0) for lo in lowers]
    ts = [eye - d for d in ds]
    ps = [pdot(d, d) for d in ds]
    yield
    power = 4
    while power < INVERSE_BASE_BLOCK:
        tps = [pdot(jnp.concatenate([t.astype(BF16), p.astype(BF16)], axis=0), p) for t, p in zip(ts, ps)]
        yield
        ts = [t + tp[:CHUNK] for t, tp in zip(ts, tps)]
        ps = [tp[CHUNK:] for tp in tps]
        power *= 2
    ts = [t + pdot(t, p) for t, p in zip(ts, ps)]
    yield
    block = INVERSE_BASE_BLOCK
    while block < CHUNK:
        off_mask = same_block(2 * block) & jnp.logical_not(same_block(block))
        ws = [pdot(t, jnp.where(off_mask, lo, 0.0)) for t, lo in zip(ts, lowers)]
        yield
        ts = [t - pdot(w, t) for t, w in zip(ts, ws)]
        yield
        block *= 2
    return ts


def _gdn_stages(xf_ref, gf_ref, xb_ref, gb_ref, of_ref, ob_ref, s_ref, group):
    r, c, head = _pair_index()
    lane_head = lax.broadcasted_iota(jnp.int32, (CHUNK, PAIR * D_HEAD), 1) // D_HEAD
    pairs = []
    for b in range(group):
        for d, (x_ref, g_ref) in enumerate(((xf_ref, gf_ref), (xb_ref, gb_ref))):
            gates = g_ref[b]
            cum = _cumsum_rows(_tri3(d), gates)
            for hp in range(N_HEADS // PAIR):
                hs = [PAIR * hp + i for i in range(PAIR)]
                jb = [d * N_HEADS + h for h in hs]
                part = lambda k: x_ref[b, :, (k * N_HEADS + hs[0]) * D_HEAD:(k * N_HEADS + hs[0] + PAIR) * D_HEAD]
                q2, k2, v2 = part(0), part(1), part(2)
                beta2 = jnp.where(lane_head == 0, gates[:, jb[0]:jb[0] + 1], gates[:, jb[1]:jb[1] + 1])
                k_beta2 = k2 * beta2
                kq = _dot_nt(jnp.concatenate([k_beta2.astype(BF16), q2.astype(BF16)], axis=0),
                             _block_diag(k2.astype(BF16)))
                pairs.append(dict(b=b, d=d, hs=hs, q2=q2, k2=k2, v2=v2, beta2=beta2, k_beta2=k_beta2, cum=cum, kq=kq))
    yield
    cum_ts = {}
    for pr in pairs:
        d, cum = pr["d"], pr["cum"]
        incl, pr["strict"] = (c <= r, c < r) if d == 0 else (c >= r, c > r)
        if (pr["b"], d) not in cum_ts:
            cum_ts[pr["b"], d] = jnp.concatenate([cum] * PAIR, axis=0).T
        cum_t = cum_ts[pr["b"], d]
        jg = [2 * N_HEADS + d * N_HEADS + h for h in pr["hs"]]
        pr["g_cum"] = [cum[:, j:j + 1] for j in jg]
        g_col = jnp.where(head == 0, pr["g_cum"][0], pr["g_cum"][1])
        g_row = jnp.where(head[0:1] == 0, cum_t[jg[0]:jg[0] + 1, :], cum_t[jg[1]:jg[1] + 1, :])
        diff = g_col - g_row
        pr["decay_mask"] = jnp.where(incl, jnp.exp(jnp.where(incl, diff, 0.0)), 0.0)
    kqs = [pr["kq"] for pr in pairs]

    lowers = [jnp.where(pr["strict"], kq[:CHUNK] * pr["decay_mask"], 0.0) for pr, kq in zip(pairs, kqs)]
    ts = yield from _unit_triangular_inverses(lowers)
    for pr, kq in zip(pairs, kqs):
        e_g = [jnp.exp(g) for g in pr["g_cum"]]
        e_g2 = jnp.where(lane_head == 0, e_g[0], e_g[1])
        pr["rhs"] = jnp.concatenate(
            [jnp.concatenate([(pr["v2"] * pr["beta2"])[:, i * D_HEAD:(i + 1) * D_HEAD],
                              (pr["k_beta2"] * e_g2)[:, i * D_HEAD:(i + 1) * D_HEAD]], axis=1) for i in range(PAIR)],
            axis=0).astype(BF16)
        attn2 = (kq[CHUNK:] * pr["decay_mask"]).astype(BF16)
        pr["attn"] = [attn2[:, i * CHUNK:(i + 1) * CHUNK] for i in range(PAIR)]
        g_last = [g[CHUNK - 1:CHUNK] if pr["d"] == 0 else g[0:1] for g in pr["g_cum"]]
        q_dec2 = (pr["q2"] * e_g2).astype(BF16)
        pr["q_dec"] = [q_dec2[:, i * D_HEAD:(i + 1) * D_HEAD] for i in range(PAIR)]
        pr["k_tail_t"] = [(pr["k2"][:, i * D_HEAD:(i + 1) * D_HEAD] * jnp.exp(g_last[i] - pr["g_cum"][i])).T.astype(BF16)
                          for i in range(PAIR)]
        pr["decay"] = [jnp.exp(g) for g in g_last]
    xs = [_dot(_block_diag(t.astype(BF16)), pr["rhs"]) for pr, t in zip(pairs, ts)]
    yield
    chains = []
    for pr, x in zip(pairs, xs):
        for i, h in enumerate(pr["hs"]):
            xi = x[i * CHUNK:(i + 1) * CHUNK]
            chains.append(dict(b=pr["b"], d=pr["d"], h=h, u=xi[:, :D_HEAD], w=xi[:, D_HEAD:].astype(BF16),
                               attn=pr["attn"][i], q_dec=pr["q_dec"][i], k_tail_t=pr["k_tail_t"][i],
                               decay=pr["decay"][i]))

    o_refs = (of_ref, ob_ref)
    states = [s_ref[ch["b"], ch["d"], ch["h"]] for ch in chains]
    sbs = [s.astype(BF16) for s in states]
    wqs = [_dot(jnp.concatenate([ch["w"], ch["q_dec"]], axis=0), sb) for ch, sb in zip(chains, sbs)]
    yield
    vbs = [(ch["u"] - wq[:CHUNK]).astype(BF16) for ch, wq in zip(chains, wqs)]
    outs = [wq[CHUNK:] + _dot(ch["attn"], vb) for ch, wq, vb in zip(chains, wqs, vbs)]
    yield
    upd = [_dot(ch["k_tail_t"], vb) for ch, vb in zip(chains, vbs)]
    for ch, s, u, o in zip(chains, states, upd, outs):
        o_refs[ch["d"]][ch["b"], :, ch["h"] * D_HEAD:(ch["h"] + 1) * D_HEAD] = o.astype(of_ref.dtype)
        s_ref[ch["b"], ch["d"], ch["h"]] = s * ch["decay"] + u


def _run_interleaved(primary, secondary, every):
    live = [primary, secondary]
    count = 0
    while live:
        gen = primary if primary in live and (secondary not in live or count < every) else secondary
        count = count + 1 if gen is primary else 0
        try:
            next(gen)
        except StopIteration:
            live.remove(gen)


def _scan_kernel(*refs, n_steps, group, hgrn, gdn, has_s0, emit_state):
    it = iter(refs)
    take = lambda k: [next(it) for _ in range(k)]
    h_in = take(4) if hgrn else None
    g_in = take(4) if gdn else None
    s0 = take(int(hgrn) + int(gdn)) if has_s0 else None
    h_out = take(2) if hgrn else None
    g_out = take(2) if gdn else None
    st = take(int(hgrn) + int(gdn)) if emit_state else None
    s_refs = take(int(hgrn) + int(gdn))
    n = pl.program_id(1)

    @pl.when(n == 0)
    def _():
        for k, s_ref in enumerate(s_refs):
            s_ref[...] = s0[k][...] if has_s0 else jnp.zeros_like(s_ref)

    gens = []
    if gdn:
        gens.append(_gdn_stages(*g_in, *g_out, s_refs[-1], group))
    if hgrn:
        gens.append(_hgrn_stages(*h_in, *h_out, s_refs[0], group))
    if len(gens) == 2:
        _run_interleaved(gens[0], gens[1], every=1)
    else:
        for _ in gens[0]:
            pass

    if emit_state:
        @pl.when(n == n_steps - 1)
        def _():
            for st_ref, s_ref in zip(st, s_refs):
                st_ref[...] = s_ref[...]


def _scan(hgrn_args, gdn_args, states0, batch, group, n_steps, gdn_columns, emit_state, gdn_out_dtype):
    state_shape = (group, 2, N_HEADS, D_HEAD, D_HEAD)
    state_spec = pl.BlockSpec(state_shape, lambda b, n: (b, 0, 0, 0, 0))
    in_specs, args, out_specs, out_shape = [], [], [], []
    if hgrn_args is not None:
        qv, f = hgrn_args
        fwd = lambda width, j: pl.BlockSpec((group, CHUNK, width), lambda b, n: (b, n, j))
        bwd = lambda width, j: pl.BlockSpec((group, CHUNK, width), lambda b, n: (b, n_steps - 1 - n, j))
        in_specs += [fwd(2 * W_GROUP, 0), fwd(W_GROUP, 0), bwd(2 * W_GROUP, 0), bwd(W_GROUP, 1)]
        args += [qv, f, qv, f]
        out_specs += [fwd(W_GROUP, 0), bwd(W_GROUP, 0)]
        out_shape += [jax.ShapeDtypeStruct(qv.shape[:2] + (W_GROUP,), BF16)] * 2
    if gdn_args is not None:
        qkv3, gates3 = gdn_args
        columns = gdn_columns
        fwd_map = (lambda b, n: (b, n, 0)) if columns == 1 else (lambda b, n: (b, 0, n))
        bwd_map = lambda b, n: fwd_map(b, n_steps - 1 - n)
        spec = lambda width, imap: pl.BlockSpec((group, CHUNK, width), imap)
        in_specs += [spec(3 * W_GROUP, fwd_map), spec(GATE_LANES, fwd_map),
                     spec(3 * W_GROUP, bwd_map), spec(GATE_LANES, bwd_map)]
        args += [qkv3, gates3, qkv3, gates3]
        out_specs += [spec(W_GROUP, fwd_map), spec(W_GROUP, bwd_map)]
        out_shape += [jax.ShapeDtypeStruct((batch, qkv3.shape[1], columns * W_GROUP), gdn_out_dtype)] * 2
    n_mixers = int(hgrn_args is not None) + int(gdn_args is not None)
    if states0 is not None:
        in_specs, args = in_specs + [state_spec] * n_mixers, args + list(states0)
    if emit_state:
        out_specs = out_specs + [state_spec] * n_mixers
        out_shape = out_shape + [jax.ShapeDtypeStruct((batch,) + state_shape[1:], F32)] * n_mixers
    return pl.pallas_call(
        functools.partial(_scan_kernel, n_steps=n_steps, group=group, hgrn=hgrn_args is not None,
                          gdn=gdn_args is not None, has_s0=states0 is not None, emit_state=emit_state),
        grid=(batch // group, n_steps),
        in_specs=in_specs, out_specs=out_specs, out_shape=out_shape,
        scratch_shapes=[pltpu.VMEM(state_shape, F32)] * n_mixers,
        compiler_params=pltpu.CompilerParams(dimension_semantics=("arbitrary", "arbitrary"),
                                             vmem_limit_bytes=VMEM_LIMIT),
        name="scan",
    )(*args)


FFN_SPLIT = 2
MXU_K_TILE = 256


def _ffn_pieces(d_ff):
    k_tiles = -(-d_ff // MXU_K_TILE)
    bounds = [min(d_ff, MXU_K_TILE * -(-k_tiles * s // FFN_SPLIT)) for s in range(FFN_SPLIT + 1)]
    return list(zip(bounds[:-1], bounds[1:]))


def _tail_kernel(*refs, latent):
    it = iter(refs)
    (x_ref, oaf_ref, oab_ref, obf_ref, obb_ref, gz_ref, mod_ref, na_ref, nb_ref, wo_ref, n2_ref,
     wg_ref, wu_ref, wd_ref, nf_ref) = (next(it) for _ in range(15))
    y_ref = next(it)
    ob_scr = next(it) if latent else None
    tm = x_ref.shape[0]

    def gated_norm(o, w_ref, mixer, h):
        sl = slice(h * D_HEAD, (h + 1) * D_HEAD)
        gate = gz_ref[:, mixer * W_GROUP + h * D_HEAD:mixer * W_GROUP + (h + 1) * D_HEAD]
        o = o * lax.rsqrt(jnp.mean(o * o, axis=-1, keepdims=True) + EPS)
        return (o * w_ref[:, sl] * gate.astype(F32)).astype(BF16)

    def both(f_ref, b_ref, h):
        sl = slice(h * D_HEAD, (h + 1) * D_HEAD)
        return f_ref[:, sl].astype(F32) + b_ref[:, sl].astype(F32)

    if latent:
        n_rows = tm // GRID_W
        for c in range(GRID_W):
            for h in range(N_HEADS):
                sl = slice(c * W_GROUP + h * D_HEAD, c * W_GROUP + (h + 1) * D_HEAD)
                ob_scr[h, pl.ds(c, n_rows, stride=ROW_PITCH), :] = obf_ref[0, :, sl] + obb_ref[0, :, sl]
        o_b = [jnp.concatenate([ob_scr[h, r * ROW_PITCH:r * ROW_PITCH + GRID_W, :] for r in range(n_rows)], axis=0)
               for h in range(N_HEADS)]
    else:
        o_b = [both(obf_ref, obb_ref, h) for h in range(N_HEADS)]
    o_a = [both(oaf_ref, oab_ref, h) for h in range(N_HEADS)]
    mixed_a = jnp.concatenate([gated_norm(o, na_ref, 0, h) for h, o in enumerate(o_a)], axis=-1)
    mixed_b = jnp.concatenate([gated_norm(o, nb_ref, 1, h) for h, o in enumerate(o_b)], axis=-1)
    m = mod_ref[0]
    y_mix = (jnp.dot(mixed_a, wo_ref[:W_GROUP, :], preferred_element_type=F32)
             + jnp.dot(mixed_b, wo_ref[W_GROUP:, :], preferred_element_type=F32))
    x1 = x_ref[...] + m[2:3] * y_mix
    y = x1 * lax.rsqrt(jnp.mean(x1 * x1, axis=-1, keepdims=True) + EPS) * n2_ref[...]
    h2 = (y * (1.0 + m[4:5]) + m[3:4]).astype(BF16)

    ff = None
    for lo, hi in _ffn_pieces(wg_ref.shape[1]):
        cols = slice(lo, hi)
        gate = jnp.dot(h2, wg_ref[:, cols], preferred_element_type=F32)
        up = jnp.dot(h2, wu_ref[:, cols], preferred_element_type=F32)
        part = jnp.dot((_silu(gate) * up).astype(BF16), wd_ref[cols, :], preferred_element_type=F32)
        ff = part if ff is None else ff + part
    x2 = x1 + m[5:6] * ff
    y_ref[...] = x2 * lax.rsqrt(jnp.mean(x2 * x2, axis=-1, keepdims=True) + EPS) * nf_ref[...]


def _tail(x2d, oaf, oab, obf, obb, hb16, mod3, mod_row_of_tile, p, tm, latent, seq_len):
    n_tok = x2d.shape[0]
    assert p["w_gate"].shape[1] % LANE == 0
    tok = lambda width: pl.BlockSpec((tm, width), lambda i: (i, 0))
    ob_spec, scratch = tok(W_GROUP), []
    if latent:
        tiles_per_seq = seq_len // tm
        ob_spec = pl.BlockSpec((1, tm // GRID_W, GRID_W * W_GROUP), lambda i: (i // tiles_per_seq, i % tiles_per_seq, 0))
        scratch = [pltpu.VMEM((N_HEADS, tm // GRID_W * ROW_PITCH, D_HEAD), F32)]
    consts = [p["norm_a"], p["norm_b"], p["w_out"], p["norm2"], p["w_gate"], p["w_up"], p["w_down"], p["norm_f"]]
    return pl.pallas_call(
        functools.partial(_tail_kernel, latent=latent),
        grid=(n_tok // tm,),
        in_specs=[tok(D_MODEL), tok(W_GROUP), tok(W_GROUP), ob_spec, ob_spec,
                  pl.BlockSpec((tm, 2 * W_GROUP), lambda i: (i, 1)),
                  pl.BlockSpec((1, 6, D_MODEL), lambda i: (mod_row_of_tile(i), 0, 0))]
                 + [_resident(a.shape) for a in consts],
        out_specs=tok(D_MODEL),
        out_shape=jax.ShapeDtypeStruct((n_tok, D_MODEL), F32),
        scratch_shapes=scratch,
        compiler_params=pltpu.CompilerParams(dimension_semantics=("arbitrary",),
                                             vmem_limit_bytes=VMEM_LIMIT),
        name="tail",
    )(x2d, oaf, oab, obf, obb, hb16, mod3, *consts)


def _stream(x, mod3, mod_row_of_tile, s0_a, s0_b, p, latent):
    batch, seq, _ = x.shape
    x2d = x.reshape(batch * seq, D_MODEL)
    tm = 512
    mod_row = functools.partial(mod_row_of_tile, tm=tm)
    hb16, hf32, qkv, gates = _inproj(x2d, mod3, mod_row, p, tm, latent, seq)
    hgrn_args = [a.reshape(batch, seq, a.shape[-1]) for a in (hb16, hf32)]
    n_chunks = seq // CHUNK
    if latent:
        assert seq // GRID_W == CHUNK and n_chunks == GRID_W
        oaf, oab, obf, obb = _scan(hgrn_args, (qkv, gates), (s0_a, s0_b), batch, group=4, n_steps=n_chunks,
                                   gdn_columns=GRID_W, emit_state=False, gdn_out_dtype=F32)
        states = (None, None)
    else:
        gdn_args = (qkv.reshape(batch, seq, 3 * W_GROUP), gates.reshape(batch, seq, GATE_LANES))
        oaf, oab, obf, obb, new_a, new_b = _scan(hgrn_args, gdn_args, None, batch, group=4, n_steps=n_chunks,
                                                 gdn_columns=1, emit_state=True, gdn_out_dtype=BF16)
        obf, obb = (o.reshape(batch * seq, W_GROUP) for o in (obf, obb))
        states = (new_a, new_b)
    oaf, oab = (o.reshape(batch * seq, W_GROUP) for o in (oaf, oab))
    y = _tail(x2d, oaf, oab, obf, obb, hb16, mod3, mod_row, p, tm, latent, seq)
    return y.reshape(batch, seq, D_MODEL), states


def kernel(x_prompt, x_sample, c, state_hgrn, state_gdn, c_ctx, w_ada, b_ada, norm1, norm2, w_in, conv_w,
           hgrn_lb, gdn_A_log, gdn_dt_bias, hgrn_out_norm, gdn_out_norm, w_out, w_gate, w_up, w_down, norm_f):
    depth = w_in.shape[0]
    assert depth == 1 and hgrn_lb.shape[0] == 2 and conv_w.shape[1] == CONV_W
    dec_batch, dec_seq, _ = x_sample.shape
    l = 0

    n_main = N_MAIN_GROUPS * W_GROUP
    pad8 = jnp.zeros((1, 2 * N_HEADS), F32)
    gparams = jnp.concatenate(
        [jnp.concatenate([pad8, a.reshape(1, 2 * N_HEADS).astype(F32),
                          jnp.zeros((1, GATE_LANES - 4 * N_HEADS), F32)], axis=1)
         for a in (gdn_A_log[l], gdn_dt_bias[l])], axis=0)
    p = {
        "norm1": norm1[l].reshape(1, D_MODEL), "norm2": norm2[l].reshape(1, D_MODEL),
        "lbp": hgrn_lb.reshape(2, 2 * W_GROUP), "gparams": gparams,
        "w_main": w_in[l].astype(BF16),
        "w_gates": jnp.pad(w_in[l][:, n_main:], ((0, 0), (0, GATE_LANES - 4 * N_HEADS))).astype(BF16),
        "conv_w": conv_w[l],
        "norm_a": hgrn_out_norm[l].reshape(1, W_GROUP), "norm_b": gdn_out_norm[l].reshape(1, W_GROUP),
        "w_out": w_out[l].astype(BF16), "w_gate": w_gate[l].astype(BF16), "w_up": w_up[l].astype(BF16),
        "w_down": w_down[l].astype(BF16), "norm_f": norm_f.reshape(1, D_MODEL),
    }

    n_mod_rows = 8
    cvec = jnp.concatenate([c_ctx[None], c, jnp.zeros((n_mod_rows - 1 - dec_batch, D_MODEL), F32)], axis=0)
    mod3 = _modulation(cvec, w_ada[l], b_ada[l]).reshape(n_mod_rows, 6, D_MODEL)

    y_prompt, (new_a, new_b) = _stream(x_prompt, mod3, lambda i, tm: 0, None, None, p, latent=False)
    y_sample, _ = _stream(x_sample, mod3, lambda i, tm: 1 + i // (dec_seq // tm), state_hgrn[:, l],
                          state_gdn[:, l], p, latent=True)
    return y_prompt, y_sample, new_a[:, None], new_b[:, None]
```

```python
import functools

import jax
import jax.numpy as jnp
from jax import lax
from jax.experimental import pallas as pl
from jax.experimental.pallas import tpu as pltpu

F32 = jnp.float32
BF16 = jnp.bfloat16

D_MODEL = 1024
N_HEADS = 4
D_HEAD = 128
W_GROUP = N_HEADS * D_HEAD
CHUNK = 64
GRID_W = 64
CONV_W = 3
EPS = 1e-6
N_MAIN_GROUPS = 9
GATE_LANES = 128
SUBLANES = 8
MIB = 1024 * 1024
VMEM_LIMIT = {"mod": 20 * MIB, "inproj": 40 * MIB, "scan": 36 * MIB, "tail": 48 * MIB}


NEG_LOG2_E = -1.4426950408889634


def _sigmoid(x):
    return 1.0 / (1.0 + jnp.exp2(x * NEG_LOG2_E))


def _silu(x):
    return x * _sigmoid(x)


def _dot(a, b):
    return jnp.dot(a.astype(BF16), b.astype(BF16), preferred_element_type=F32)


def _dot_nt(a, b):
    return lax.dot_general(a.astype(BF16), b.astype(BF16), (((1,), (1,)), ((), ())),
                           preferred_element_type=F32)


def _split3(x):
    x1 = x.astype(BF16)
    r = x - x1.astype(F32)
    x2 = r.astype(BF16)
    x3 = (r - x2.astype(F32)).astype(BF16)
    return x1, x2, x3


def _cumsum_rows(tri3, x):
    return jnp.dot(tri3, jnp.concatenate(_split3(x), axis=0), preferred_element_type=F32)


def _tri3(direction):
    tri = _tri_masks(direction)[0].astype(BF16)
    return jnp.concatenate([tri, tri, tri], axis=1)


def _tri_masks(direction):
    r = lax.broadcasted_iota(jnp.int32, (CHUNK, CHUNK), 0)
    c = lax.broadcasted_iota(jnp.int32, (CHUNK, CHUNK), 1)
    if direction == 0:
        return c <= r, c < r
    return c >= r, c > r


def _resident(shape):
    return pl.BlockSpec(shape, lambda i: (0,) * len(shape), pipeline_mode=pl.Buffered(1))


def _mod_kernel(c_ref, w_ref, b_ref, o_ref):
    s = _silu(c_ref[...])
    o_ref[...] = _dot(s, w_ref[...]) + b_ref[...]


def _modulation(cvec, w_ada, b_ada):
    n_rows, d = cvec.shape
    n_out = w_ada.shape[1]
    tn = 1536
    return pl.pallas_call(
        _mod_kernel,
        grid=(n_out // tn,),
        in_specs=[pl.BlockSpec((n_rows, d), lambda j: (0, 0)),
                  pl.BlockSpec((d, tn), lambda j: (0, j)),
                  pl.BlockSpec((1, tn), lambda j: (0, j))],
        out_specs=pl.BlockSpec((n_rows, tn), lambda j: (0, j)),
        out_shape=jax.ShapeDtypeStruct((n_rows, n_out), F32),
        compiler_params=pltpu.CompilerParams(dimension_semantics=("arbitrary",),
                                             vmem_limit_bytes=VMEM_LIMIT["mod"]),
        name="mod",
    )(cvec, w_ada, b_ada.reshape(1, n_out))


HALO = GRID_W
LANE = 128
QKV_TILES = 3 * W_GROUP // LANE
ROW_PITCH = GRID_W + 8


def _inproj_kernel(*refs, latent, seq_len):
    it = iter(refs)
    x_ref = next(it)
    xp_ref, xn_ref = (next(it), next(it)) if latent else (None, None)
    mod_ref, n1_ref, lb_ref, gp_ref, cw_ref, w_ref, wg_ref = (next(it) for _ in range(7))
    qa_ref, ff_ref, fb_ref, va_ref, ga_ref, qkv_ref, zb_ref, gates_ref = (next(it) for _ in range(8))
    qkv_scr, gates_scr = (next(it), next(it)) if latent else (None, None)
    tm = x_ref.shape[0]
    m = mod_ref[0]

    norm_scale = n1_ref[...] * (1.0 + m[1:2])

    def normed(x):
        return (x * lax.rsqrt(jnp.mean(x * x, axis=-1, keepdims=True) + EPS) * norm_scale + m[0:1]).astype(BF16)

    hb = normed(x_ref[...])

    w_qkv = w_ref[:, 5 * W_GROUP:8 * W_GROUP]
    if latent:
        i = pl.program_id(0)
        tiles_per_seq = seq_len // tm
        ext = jnp.dot(jnp.concatenate([normed(xp_ref[...]), hb, normed(xn_ref[...])], axis=0), w_qkv,
                      preferred_element_type=F32)
        above = jnp.where(i % tiles_per_seq == 0, 0.0, ext[:HALO])
        below = jnp.where(i % tiles_per_seq == tiles_per_seq - 1, 0.0, ext[HALO + tm:])
        cur = ext[HALO:HALO + tm]
        prev = jnp.concatenate([above, ext[HALO:tm]], axis=0)
        nxt = jnp.concatenate([ext[2 * HALO:HALO + tm], below], axis=0)
    else:
        cur = jnp.dot(hb, w_qkv, preferred_element_type=F32)
        pos = lax.broadcasted_iota(jnp.int32, cur.shape, 0) % seq_len
        prev = jnp.where(pos == 0, 0.0, pltpu.roll(cur, 1, axis=0))
        nxt = jnp.where(pos == seq_len - 1, 0.0, pltpu.roll(cur, tm - 1, axis=0))
    cw = cw_ref[...]
    y = _silu(prev * cw[0:1] + cur * cw[1:2] + nxt * cw[2:3])
    tiles = []
    for j in range(QKV_TILES):
        t = y[:, j * LANE:(j + 1) * LANE]
        if j < 2 * N_HEADS:
            inv = lax.rsqrt(jnp.sum(t * t, axis=-1, keepdims=True) + EPS)
            t = t * (inv * (D_HEAD ** -0.5) if j < N_HEADS else inv)
        tiles.append(t)

    raw = jnp.dot(hb, wg_ref[...], preferred_element_type=F32)
    gp = gp_ref[...]
    z = raw + gp[1:2]
    softplus = jnp.maximum(z, 0.0) + jnp.log(1.0 + jnp.exp(-jnp.abs(z)))
    lane = lax.broadcasted_iota(jnp.int32, raw.shape, 1)
    gates = jnp.where(lane < 2 * N_HEADS, _sigmoid(raw),
                      jnp.where(lane < 4 * N_HEADS, -jnp.exp(gp[0:1]) * softplus, 0.0))

    if not latent:
        for j, t in enumerate(tiles):
            qkv_ref[:, j * LANE:(j + 1) * LANE] = t
        gates_ref[...] = gates
    else:
        n_rows = tm // GRID_W
        for r in range(n_rows):
            rows, dst = slice(r * GRID_W, (r + 1) * GRID_W), slice(r * ROW_PITCH, r * ROW_PITCH + GRID_W)
            for j, t in enumerate(tiles):
                qkv_scr[j, dst, :] = t[rows]
            gates_scr[dst, :] = gates[rows]
        for c in range(GRID_W):
            for j in range(QKV_TILES):
                qkv_ref[0, :, (c * QKV_TILES + j) * LANE:(c * QKV_TILES + j + 1) * LANE] = (
                    qkv_scr[j, pl.ds(c, n_rows, stride=ROW_PITCH), :])
            gates_ref[0, :, c * GATE_LANES:(c + 1) * GATE_LANES] = gates_scr[pl.ds(c, n_rows, stride=ROW_PITCH), :]

    def proj(j):
        return jnp.dot(hb, w_ref[:, j * W_GROUP:(j + 1) * W_GROUP], preferred_element_type=F32)

    lbp = lb_ref[...]
    e = jnp.exp(lbp - jnp.max(lbp, axis=0, keepdims=True))
    lb = e[0:1] / jnp.sum(e, axis=0, keepdims=True)
    lb_f, lb_b = lb[:, :W_GROUP], lb[:, W_GROUP:]

    qa_ref[...] = _silu(proj(0)).astype(BF16)
    ff_ref[...] = lb_f + (1.0 - lb_f) * _sigmoid(proj(1))
    fb_ref[...] = lb_b + (1.0 - lb_b) * _sigmoid(proj(2))
    ga_ref[...] = _silu(proj(4)).astype(BF16)
    zb_ref[...] = _silu(proj(8)).astype(BF16)
    va_ref[...] = proj(3).astype(BF16)


def _inproj(x2d, mod3, mod_row_of_tile, p, tm, latent, seq_len):
    n_tok = x2d.shape[0]
    tok = lambda width: pl.BlockSpec((tm, width), lambda i: (i, 0))
    widths = [W_GROUP] * 5 + [3 * W_GROUP, W_GROUP, GATE_LANES]
    out_specs = [tok(w) for w in widths]
    dtypes = [BF16, F32, F32, BF16, BF16, F32, BF16, F32]
    out_shape = [jax.ShapeDtypeStruct((n_tok, w), dt) for w, dt in zip(widths, dtypes)]
    in_specs, args, scratch = [tok(D_MODEL)], [x2d], []
    if latent:
        n_halo_blocks, per_tile, rows = n_tok // HALO, tm // HALO, tm // GRID_W
        tiles_per_seq = seq_len // tm
        in_specs += [pl.BlockSpec((HALO, D_MODEL), lambda i: (jnp.maximum(i * per_tile - 1, 0), 0)),
                     pl.BlockSpec((HALO, D_MODEL), lambda i: (jnp.minimum((i + 1) * per_tile, n_halo_blocks - 1), 0))]
        args += [x2d, x2d]
        col = lambda width: pl.BlockSpec((1, rows, GRID_W * width),
                                         lambda i: (i // tiles_per_seq, i % tiles_per_seq, 0))
        col_shape = lambda width: jax.ShapeDtypeStruct((n_tok // seq_len, seq_len // GRID_W, GRID_W * width), F32)
        for k, width in ((5, 3 * W_GROUP), (7, GATE_LANES)):
            out_specs[k], out_shape[k] = col(width), col_shape(width)
        scratch = [pltpu.VMEM((QKV_TILES, rows * ROW_PITCH, LANE), F32), pltpu.VMEM((rows * ROW_PITCH, GATE_LANES), F32)]
    consts = [p["norm1"], p["lbp"], p["gparams"], p["conv_w"], p["w_main"], p["w_gates"]]
    in_specs += [pl.BlockSpec((1, 6, D_MODEL), lambda i: (mod_row_of_tile(i), 0, 0))] + [_resident(a.shape) for a in consts]
    return pl.pallas_call(
        functools.partial(_inproj_kernel, latent=latent, seq_len=seq_len),
        grid=(n_tok // tm,),
        in_specs=in_specs, out_specs=out_specs, out_shape=out_shape, scratch_shapes=scratch,
        compiler_params=pltpu.CompilerParams(dimension_semantics=("arbitrary",),
                                             vmem_limit_bytes=VMEM_LIMIT["inproj"]),
        name="inproj",
    )(*args, mod3, *consts)


INVERSE_BASE_BLOCK = 8


def _hgrn_stages(qf_ref, ff_ref, vf_ref, qb_ref, fb_ref, vb_ref, of_ref, ob_ref, s_ref, group):
    pre = []
    for b in range(group):
        for d, f_ref in enumerate((ff_ref, fb_ref)):
            pre.append((b, d, _cumsum_rows(_tri3(d), jnp.log(f_ref[b]))))
    yield
    chains = []
    for b, d, cum_all in pre:
        q_ref, f_ref, v_ref, o_ref = (qf_ref, ff_ref, vf_ref, of_ref) if d == 0 else (qb_ref, fb_ref, vb_ref, ob_ref)
        incl, _ = _tri_masks(d)
        for h in range(N_HEADS):
            sl = slice(h * D_HEAD, (h + 1) * D_HEAD)
            k = 1.0 - f_ref[b, :, sl]
            G = cum_all[:, sl]
            g_last_row = G[CHUNK - 1:CHUNK] if d == 0 else G[0:1]
            chains.append(dict(
                b=b, d=d, h=h, sl=sl, o_ref=o_ref, incl=incl, vb=v_ref[b, :, sl].astype(BF16),
                decay=jnp.exp(jnp.broadcast_to(g_last_row, (SUBLANES, D_HEAD)).T[:, 0:1]),
                q_dec=(q_ref[b, :, sl].astype(F32) * jnp.exp(G)).astype(BF16), k_dec=k * jnp.exp2(G * NEG_LOG2_E),
                k_tail_t=(k * jnp.exp(g_last_row - G)).T.astype(BF16)))
    attns = [_dot_nt(ch["q_dec"], ch["k_dec"]).astype(BF16) for ch in chains]
    attns = [jnp.where(ch["incl"], a, jnp.zeros_like(a)) for ch, a in zip(chains, attns)]
    yield
    states = [s_ref[ch["b"], ch["d"], ch["h"]] for ch in chains]
    outs = [_dot(jnp.concatenate([ch["q_dec"], attn], axis=1), jnp.concatenate([s.astype(BF16), ch["vb"]], axis=0))
            for ch, attn, s in zip(chains, attns, states)]
    upds = [_dot(ch["k_tail_t"], ch["vb"]) for ch in chains]
    for ch, o, s, u in zip(chains, outs, states, upds):
        ch["o_ref"][ch["b"], :, ch["sl"]] = o.astype(BF16)
        s_ref[ch["b"], ch["d"], ch["h"]] = ch["decay"] * s + u


PAIR = 2


def _pair_index():
    r = lax.broadcasted_iota(jnp.int32, (CHUNK, PAIR * CHUNK), 0)
    lane = lax.broadcasted_iota(jnp.int32, (CHUNK, PAIR * CHUNK), 1)
    return r, lane % CHUNK, lane // CHUNK


def _block_diag(packed):
    head = lax.broadcasted_iota(jnp.int32, packed.shape, 1) // (packed.shape[1] // PAIR)
    return jnp.concatenate([jnp.where(head == h, packed, jnp.zeros_like(packed)) for h in range(PAIR)], axis=0)


def _unit_triangular_inverses(lowers):
    r, c, _ = _pair_index()
    same_block = lambda block: r // block == c // block
    pdot = lambda a, b: _dot(a, _block_diag(b.astype(BF16)))
    eye = jnp.where(r == c, 1.0, 0.0)
    diag = same_block(INVERSE_BASE_BLOCK)
    lowers_b = [lo.astype(BF16) for lo in lowers]
    ds = [jnp.where(diag, lo, 0.0) for lo in lowers]
    ts = [eye - d for d in ds]
    ps = [pdot(d, d) for d in ds]
    yield
    power = 4
    while power < INVERSE_BASE_BLOCK:
        tps = [pdot(jnp.concatenate([t.astype(BF16), p.astype(BF16)], axis=0), p) for t, p in zip(ts, ps)]
        yield
        ts = [t + tp[:CHUNK] for t, tp in zip(ts, tps)]
        ps = [tp[CHUNK:] for tp in tps]
        power *= 2
    ts = [t + pdot(t, p) for t, p in zip(ts, ps)]
    yield
    block = INVERSE_BASE_BLOCK
    while block < CHUNK:
        off_mask = same_block(2 * block) & jnp.logical_not(same_block(block))
        ws = [pdot(t, jnp.where(off_mask, lo, jnp.zeros_like(lo))) for t, lo in zip(ts, lowers_b)]
        yield
        ts = [t - pdot(w, t) for t, w in zip(ts, ws)]
        yield
        block *= 2
    return ts


def _gdn_stages(xf_ref, gf_ref, xb_ref, gb_ref, of_ref, ob_ref, s_ref, group):
    r, c, head = _pair_index()
    lane_head = lax.broadcasted_iota(jnp.int32, (CHUNK, PAIR * D_HEAD), 1) // D_HEAD
    pairs = []
    for b in range(group):
        for d, (x_ref, g_ref) in enumerate(((xf_ref, gf_ref), (xb_ref, gb_ref))):
            gates = g_ref[b]
            cum = _cumsum_rows(_tri3(d), gates)
            for hp in range(N_HEADS // PAIR):
                hs = [PAIR * hp + i for i in range(PAIR)]
                jb = [d * N_HEADS + h for h in hs]
                part = lambda k: x_ref[b, :, (k * N_HEADS + hs[0]) * D_HEAD:(k * N_HEADS + hs[0] + PAIR) * D_HEAD]
                q2, k2, v2 = part(0), part(1), part(2)
                beta2 = jnp.where(lane_head == 0, gates[:, jb[0]:jb[0] + 1], gates[:, jb[1]:jb[1] + 1])
                k_beta2 = k2 * beta2
                kq = _dot_nt(jnp.concatenate([k_beta2.astype(BF16), q2.astype(BF16)], axis=0),
                             _block_diag(k2.astype(BF16)))
                pairs.append(dict(b=b, d=d, hs=hs, q2=q2, k2=k2, v2=v2, beta2=beta2, k_beta2=k_beta2, cum=cum, kq=kq))
    yield
    cum_ts = {}
    for pr in pairs:
        d, cum = pr["d"], pr["cum"]
        incl, pr["strict"] = (c <= r, c < r) if d == 0 else (c >= r, c > r)
        if (pr["b"], d) not in cum_ts:
            cum_ts[pr["b"], d] = jnp.concatenate([cum] * PAIR, axis=0).T
        cum_t = cum_ts[pr["b"], d]
        jg = [2 * N_HEADS + d * N_HEADS + h for h in pr["hs"]]
        pr["g_cum"] = [cum[:, j:j + 1] for j in jg]
        g_col = jnp.where(head == 0, pr["g_cum"][0], pr["g_cum"][1])
        g_row = jnp.where(head[0:1] == 0, cum_t[jg[0]:jg[0] + 1, :], cum_t[jg[1]:jg[1] + 1, :])
        diff = g_col - g_row
        pr["decay_mask"] = jnp.where(incl, jnp.exp(jnp.where(incl, diff, 0.0)), 0.0)
    kqs = [pr["kq"] for pr in pairs]

    lowers = [jnp.where(pr["strict"], kq[:CHUNK] * pr["decay_mask"], 0.0) for pr, kq in zip(pairs, kqs)]
    ts = yield from _unit_triangular_inverses(lowers)
    for pr, kq in zip(pairs, kqs):
        e_g = [jnp.exp(g) for g in pr["g_cum"]]
        e_g2 = jnp.where(lane_head == 0, e_g[0], e_g[1])
        pr["rhs"] = jnp.concatenate(
            [jnp.concatenate([(pr["v2"] * pr["beta2"])[:, i * D_HEAD:(i + 1) * D_HEAD],
                              (pr["k_beta2"] * e_g2)[:, i * D_HEAD:(i + 1) * D_HEAD]], axis=1) for i in range(PAIR)],
            axis=0).astype(BF16)
        attn2 = (kq[CHUNK:] * pr["decay_mask"]).astype(BF16)
        pr["attn"] = [attn2[:, i * CHUNK:(i + 1) * CHUNK] for i in range(PAIR)]
        g_last = [g[CHUNK - 1:CHUNK] if pr["d"] == 0 else g[0:1] for g in pr["g_cum"]]
        q_dec2 = (pr["q2"] * e_g2).astype(BF16)
        pr["q_dec"] = [q_dec2[:, i * D_HEAD:(i + 1) * D_HEAD] for i in range(PAIR)]
        pr["k_tail_t"] = [(pr["k2"][:, i * D_HEAD:(i + 1) * D_HEAD] * jnp.exp(g_last[i] - pr["g_cum"][i])).T.astype(BF16)
                          for i in range(PAIR)]
        pr["decay"] = [jnp.exp(g) for g in g_last]
    xs = [_dot(_block_diag(t.astype(BF16)), pr["rhs"]) for pr, t in zip(pairs, ts)]
    yield
    chains = []
    for pr, x in zip(pairs, xs):
        for i, h in enumerate(pr["hs"]):
            xi = x[i * CHUNK:(i + 1) * CHUNK]
            chains.append(dict(b=pr["b"], d=pr["d"], h=h, u=xi[:, :D_HEAD], w=xi[:, D_HEAD:].astype(BF16),
                               attn=pr["attn"][i], q_dec=pr["q_dec"][i], k_tail_t=pr["k_tail_t"][i],
                               decay=pr["decay"][i]))

    o_refs = (of_ref, ob_ref)
    states = [s_ref[ch["b"], ch["d"], ch["h"]] for ch in chains]
    sbs = [s.astype(BF16) for s in states]
    wqs = [_dot(jnp.concatenate([ch["w"], ch["q_dec"]], axis=0), sb) for ch, sb in zip(chains, sbs)]
    yield
    vbs = [(ch["u"] - wq[:CHUNK]).astype(BF16) for ch, wq in zip(chains, wqs)]
    outs = [wq[CHUNK:] + _dot(ch["attn"], vb) for ch, wq, vb in zip(chains, wqs, vbs)]
    yield
    upd = [_dot(ch["k_tail_t"], vb) for ch, vb in zip(chains, vbs)]
    for ch, s, u, o in zip(chains, states, upd, outs):
        o_refs[ch["d"]][ch["b"], :, ch["h"] * D_HEAD:(ch["h"] + 1) * D_HEAD] = o.astype(of_ref.dtype)
        s_ref[ch["b"], ch["d"], ch["h"]] = s * ch["decay"] + u


def _run_interleaved(primary, secondary, every):
    live = [primary, secondary]
    count = 0
    while live:
        gen = primary if primary in live and (secondary not in live or count < every) else secondary
        count = count + 1 if gen is primary else 0
        try:
            next(gen)
        except StopIteration:
            live.remove(gen)


def _scan_kernel(*refs, n_steps, group, hgrn, gdn, has_s0, emit_state):
    it = iter(refs)
    take = lambda k: [next(it) for _ in range(k)]
    h_in = take(6) if hgrn else None
    g_in = take(4) if gdn else None
    s0 = take(int(hgrn) + int(gdn)) if has_s0 else None
    h_out = take(2) if hgrn else None
    g_out = take(2) if gdn else None
    st = take(int(hgrn) + int(gdn)) if emit_state else None
    s_refs = take(int(hgrn) + int(gdn))
    n = pl.program_id(1)

    @pl.when(n == 0)
    def _():
        for k, s_ref in enumerate(s_refs):
            s_ref[...] = s0[k][...] if has_s0 else jnp.zeros_like(s_ref)

    gens = []
    if gdn:
        gens.append(_gdn_stages(*g_in, *g_out, s_refs[-1], group))
    if hgrn:
        gens.append(_hgrn_stages(*h_in, *h_out, s_refs[0], group))
    if len(gens) == 2:
        _run_interleaved(gens[0], gens[1], every=1)
    else:
        for _ in gens[0]:
            pass

    if emit_state:
        @pl.when(n == n_steps - 1)
        def _():
            for st_ref, s_ref in zip(st, s_refs):
                st_ref[...] = s_ref[...]


def _scan(hgrn_args, gdn_args, states0, batch, group, n_steps, gdn_columns, emit_state, gdn_out_dtype):
    state_shape = (group, 2, N_HEADS, D_HEAD, D_HEAD)
    state_spec = pl.BlockSpec(state_shape, lambda b, n: (b, 0, 0, 0, 0))
    in_specs, args, out_specs, out_shape = [], [], [], []
    if hgrn_args is not None:
        qa, f_fwd, f_bwd, va = hgrn_args
        fwd = pl.BlockSpec((group, CHUNK, W_GROUP), lambda b, n: (b, n, 0))
        bwd = pl.BlockSpec((group, CHUNK, W_GROUP), lambda b, n: (b, n_steps - 1 - n, 0))
        in_specs += [fwd, fwd, fwd, bwd, bwd, bwd]
        args += [qa, f_fwd, va, qa, f_bwd, va]
        out_specs += [fwd, bwd]
        out_shape += [jax.ShapeDtypeStruct(qa.shape, BF16)] * 2
    if gdn_args is not None:
        qkv3, gates3 = gdn_args
        columns = gdn_columns
        fwd_map = (lambda b, n: (b, n, 0)) if columns == 1 else (lambda b, n: (b, 0, n))
        bwd_map = lambda b, n: fwd_map(b, n_steps - 1 - n)
        spec = lambda width, imap: pl.BlockSpec((group, CHUNK, width), imap)
        in_specs += [spec(3 * W_GROUP, fwd_map), spec(GATE_LANES, fwd_map),
                     spec(3 * W_GROUP, bwd_map), spec(GATE_LANES, bwd_map)]
        args += [qkv3, gates3, qkv3, gates3]
        out_specs += [spec(W_GROUP, fwd_map), spec(W_GROUP, bwd_map)]
        out_shape += [jax.ShapeDtypeStruct((batch, qkv3.shape[1], columns * W_GROUP), gdn_out_dtype)] * 2
    n_mixers = int(hgrn_args is not None) + int(gdn_args is not None)
    if states0 is not None:
        in_specs, args = in_specs + [state_spec] * n_mixers, args + list(states0)
    if emit_state:
        out_specs = out_specs + [state_spec] * n_mixers
        out_shape = out_shape + [jax.ShapeDtypeStruct((batch,) + state_shape[1:], F32)] * n_mixers
    return pl.pallas_call(
        functools.partial(_scan_kernel, n_steps=n_steps, group=group, hgrn=hgrn_args is not None,
                          gdn=gdn_args is not None, has_s0=states0 is not None, emit_state=emit_state),
        grid=(batch // group, n_steps),
        in_specs=in_specs, out_specs=out_specs, out_shape=out_shape,
        scratch_shapes=[pltpu.VMEM(state_shape, F32)] * n_mixers,
        compiler_params=pltpu.CompilerParams(dimension_semantics=("arbitrary", "arbitrary"),
                                             vmem_limit_bytes=VMEM_LIMIT["scan"]),
        name="scan",
    )(*args)


FFN_SPLIT = 2
MXU_K_TILE = 256


def _ffn_pieces(d_ff):
    k_tiles = -(-d_ff // MXU_K_TILE)
    bounds = [min(d_ff, MXU_K_TILE * -(-k_tiles * s // FFN_SPLIT)) for s in range(FFN_SPLIT + 1)]
    return list(zip(bounds[:-1], bounds[1:]))


def _tail_kernel(*refs, latent):
    it = iter(refs)
    (x_ref, oaf_ref, oab_ref, obf_ref, obb_ref, ga_ref, zb_ref, mod_ref, na_ref, nb_ref, wo_ref, n2_ref,
     wg_ref, wu_ref, wd_ref, nf_ref) = (next(it) for _ in range(16))
    y_ref = next(it)
    ob_scr = next(it) if latent else None
    tm = x_ref.shape[0]

    def gated_norm(o, w_ref, gate_ref, h):
        sl = slice(h * D_HEAD, (h + 1) * D_HEAD)
        o = o * lax.rsqrt(jnp.mean(o * o, axis=-1, keepdims=True) + EPS)
        return (o * w_ref[:, sl] * gate_ref[:, sl].astype(F32)).astype(BF16)

    def both(f_ref, b_ref, h):
        sl = slice(h * D_HEAD, (h + 1) * D_HEAD)
        return f_ref[:, sl].astype(F32) + b_ref[:, sl].astype(F32)

    if latent:
        n_rows = tm // GRID_W
        for c in range(GRID_W):
            for h in range(N_HEADS):
                sl = slice(c * W_GROUP + h * D_HEAD, c * W_GROUP + (h + 1) * D_HEAD)
                ob_scr[h, pl.ds(c, n_rows, stride=ROW_PITCH), :] = obf_ref[0, :, sl] + obb_ref[0, :, sl]
        o_b = [jnp.concatenate([ob_scr[h, r * ROW_PITCH:r * ROW_PITCH + GRID_W, :] for r in range(n_rows)], axis=0)
               for h in range(N_HEADS)]
    else:
        o_b = [both(obf_ref, obb_ref, h) for h in range(N_HEADS)]
    o_a = [both(oaf_ref, oab_ref, h) for h in range(N_HEADS)]
    mixed_a = jnp.concatenate([gated_norm(o, na_ref, ga_ref, h) for h, o in enumerate(o_a)], axis=-1)
    mixed_b = jnp.concatenate([gated_norm(o, nb_ref, zb_ref, h) for h, o in enumerate(o_b)], axis=-1)
    m = mod_ref[0]
    y_mix = (jnp.dot(mixed_a, wo_ref[:W_GROUP, :], preferred_element_type=F32)
             + jnp.dot(mixed_b, wo_ref[W_GROUP:, :], preferred_element_type=F32))
    x1 = x_ref[...] + m[2:3] * y_mix
    y = x1 * lax.rsqrt(jnp.mean(x1 * x1, axis=-1, keepdims=True) + EPS) * n2_ref[...]
    h2 = (y * (1.0 + m[4:5]) + m[3:4]).astype(BF16)

    ff = None
    for lo, hi in _ffn_pieces(wg_ref.shape[1]):
        cols = slice(lo, hi)
        gate = jnp.dot(h2, wg_ref[:, cols], preferred_element_type=F32)
        up = jnp.dot(h2, wu_ref[:, cols], preferred_element_type=F32)
        part = jnp.dot((_silu(gate) * up).astype(BF16), wd_ref[cols, :], preferred_element_type=F32)
        ff = part if ff is None else ff + part
    x2 = x1 + m[5:6] * ff
    y_ref[...] = x2 * lax.rsqrt(jnp.mean(x2 * x2, axis=-1, keepdims=True) + EPS) * nf_ref[...]


def _tail(x2d, oaf, oab, obf, obb, ga, zb, mod3, mod_row_of_tile, p, tm, latent, seq_len):
    n_tok = x2d.shape[0]
    assert p["w_gate"].shape[1] % LANE == 0
    tok = lambda width: pl.BlockSpec((tm, width), lambda i: (i, 0))
    ob_spec, scratch = tok(W_GROUP), []
    if latent:
        tiles_per_seq = seq_len // tm
        ob_spec = pl.BlockSpec((1, tm // GRID_W, GRID_W * W_GROUP), lambda i: (i // tiles_per_seq, i % tiles_per_seq, 0))
        scratch = [pltpu.VMEM((N_HEADS, tm // GRID_W * ROW_PITCH, D_HEAD), F32)]
    consts = [p["norm_a"], p["norm_b"], p["w_out"], p["norm2"], p["w_gate"], p["w_up"], p["w_down"], p["norm_f"]]
    return pl.pallas_call(
        functools.partial(_tail_kernel, latent=latent),
        grid=(n_tok // tm,),
        in_specs=[tok(D_MODEL), tok(W_GROUP), tok(W_GROUP), ob_spec, ob_spec, tok(W_GROUP), tok(W_GROUP),
                  pl.BlockSpec((1, 6, D_MODEL), lambda i: (mod_row_of_tile(i), 0, 0))]
                 + [_resident(a.shape) for a in consts],
        out_specs=tok(D_MODEL),
        out_shape=jax.ShapeDtypeStruct((n_tok, D_MODEL), F32),
        scratch_shapes=scratch,
        compiler_params=pltpu.CompilerParams(dimension_semantics=("arbitrary",),
                                             vmem_limit_bytes=VMEM_LIMIT["tail"]),
        name="tail",
    )(x2d, oaf, oab, obf, obb, ga, zb, mod3, *consts)


def _stream(x, mod3, mod_row_of_tile, s0_a, s0_b, p, latent):
    batch, seq, _ = x.shape
    x2d = x.reshape(batch * seq, D_MODEL)
    tm = 512
    mod_row = functools.partial(mod_row_of_tile, tm=tm)
    qa, f_fwd, f_bwd, va, ga, qkv, zb, gates = _inproj(x2d, mod3, mod_row, p, tm, latent, seq)
    hgrn_args = [a.reshape(batch, seq, W_GROUP) for a in (qa, f_fwd, f_bwd, va)]
    n_chunks = seq // CHUNK
    if latent:
        assert seq // GRID_W == CHUNK and n_chunks == GRID_W
        oaf, oab, obf, obb = _scan(hgrn_args, (qkv, gates), (s0_a, s0_b), batch, group=4, n_steps=n_chunks,
                                   gdn_columns=GRID_W, emit_state=False, gdn_out_dtype=F32)
        states = (None, None)
    else:
        gdn_args = (qkv.reshape(batch, seq, 3 * W_GROUP), gates.reshape(batch, seq, GATE_LANES))
        oaf, oab, obf, obb, new_a, new_b = _scan(hgrn_args, gdn_args, None, batch, group=4, n_steps=n_chunks,
                                                 gdn_columns=1, emit_state=True, gdn_out_dtype=BF16)
        obf, obb = (o.reshape(batch * seq, W_GROUP) for o in (obf, obb))
        states = (new_a, new_b)
    oaf, oab = (o.reshape(batch * seq, W_GROUP) for o in (oaf, oab))
    y = _tail(x2d, oaf, oab, obf, obb, ga, zb, mod3, mod_row, p, tm, latent, seq)
    return y.reshape(batch, seq, D_MODEL), states


def kernel(x_prompt, x_sample, c, state_hgrn, state_gdn, c_ctx, w_ada, b_ada, norm1, norm2, w_in, conv_w,
           hgrn_lb, gdn_A_log, gdn_dt_bias, hgrn_out_norm, gdn_out_norm, w_out, w_gate, w_up, w_down, norm_f):
    depth = w_in.shape[0]
    assert depth == 1 and hgrn_lb.shape[0] == 2 and conv_w.shape[1] == CONV_W
    dec_batch, dec_seq, _ = x_sample.shape
    l = 0

    n_main = N_MAIN_GROUPS * W_GROUP
    pad8 = jnp.zeros((1, 2 * N_HEADS), F32)
    gparams = jnp.concatenate(
        [jnp.concatenate([pad8, a.reshape(1, 2 * N_HEADS).astype(F32),
                          jnp.zeros((1, GATE_LANES - 4 * N_HEADS), F32)], axis=1)
         for a in (gdn_A_log[l], gdn_dt_bias[l])], axis=0)
    p = {
        "norm1": norm1[l].reshape(1, D_MODEL), "norm2": norm2[l].reshape(1, D_MODEL),
        "lbp": hgrn_lb.reshape(2, 2 * W_GROUP), "gparams": gparams,
        "w_main": w_in[l].astype(BF16),
        "w_gates": jnp.pad(w_in[l][:, n_main:], ((0, 0), (0, GATE_LANES - 4 * N_HEADS))).astype(BF16),
        "conv_w": conv_w[l],
        "norm_a": hgrn_out_norm[l].reshape(1, W_GROUP), "norm_b": gdn_out_norm[l].reshape(1, W_GROUP),
        "w_out": w_out[l].astype(BF16), "w_gate": w_gate[l].astype(BF16), "w_up": w_up[l].astype(BF16),
        "w_down": w_down[l].astype(BF16), "norm_f": norm_f.reshape(1, D_MODEL),
    }

    n_mod_rows = 8
    cvec = jnp.concatenate([c_ctx[None], c, jnp.zeros((n_mod_rows - 1 - dec_batch, D_MODEL), F32)], axis=0)
    mod3 = _modulation(cvec, w_ada[l], b_ada[l]).reshape(n_mod_rows, 6, D_MODEL)

    y_prompt, (new_a, new_b) = _stream(x_prompt, mod3, lambda i, tm: 0, None, None, p, latent=False)
    y_sample, _ = _stream(x_sample, mod3, lambda i, tm: 1 + i // (dec_seq // tm), state_hgrn[:, l],
                          state_gdn[:, l], p, latent=True)
    return y_prompt, y_sample, new_a[:, None], new_b[:, None]
```

```python
import functools

import jax
import jax.numpy as jnp
from jax import lax
from jax.experimental import pallas as pl
from jax.experimental.pallas import tpu as pltpu

F32 = jnp.float32
BF16 = jnp.bfloat16

D_MODEL = 1024
N_HEADS = 4
D_HEAD = 128
W_GROUP = N_HEADS * D_HEAD
CHUNK = 64
GRID_W = 64
CONV_W = 3
EPS = 1e-6
N_MAIN_GROUPS = 9
GATE_LANES = 128
SUBLANES = 8
MIB = 1024 * 1024
VMEM_LIMIT = {"mod": 56 * MIB, "inproj": 40 * MIB, "scan": 36 * MIB, "tail": 48 * MIB}


NEG_LOG2_E = -1.4426950408889634


def _sigmoid(x):
    return 1.0 / (1.0 + jnp.exp2(x * NEG_LOG2_E))


def _silu(x):
    return x * _sigmoid(x)


def _dot(a, b):
    return jnp.dot(a.astype(BF16), b.astype(BF16), preferred_element_type=F32)


def _dot_nt(a, b):
    return lax.dot_general(a.astype(BF16), b.astype(BF16), (((1,), (1,)), ((), ())),
                           preferred_element_type=F32)


def _split3(x):
    x1 = x.astype(BF16)
    r = x - x1.astype(F32)
    x2 = r.astype(BF16)
    x3 = (r - x2.astype(F32)).astype(BF16)
    return x1, x2, x3


def _cumsum_rows(tri3, x):
    return jnp.dot(tri3, jnp.concatenate(_split3(x), axis=0), preferred_element_type=F32)


def _tri3(direction):
    tri = _tri_masks(direction)[0].astype(BF16)
    return jnp.concatenate([tri, tri, tri], axis=1)


def _tri_masks(direction):
    r = lax.broadcasted_iota(jnp.int32, (CHUNK, CHUNK), 0)
    c = lax.broadcasted_iota(jnp.int32, (CHUNK, CHUNK), 1)
    if direction == 0:
        return c <= r, c < r
    return c >= r, c > r


def _resident(shape):
    return pl.BlockSpec(shape, lambda i: (0,) * len(shape), pipeline_mode=pl.Buffered(1))


def _mod_kernel(c_ref, w_ref, b_ref, o_ref):
    s = _silu(c_ref[...])
    o_ref[...] = _dot(s, w_ref[...]) + b_ref[...]


def _modulation(cvec, w_ada, b_ada):
    n_rows, d = cvec.shape
    n_out = w_ada.shape[1]
    tn = 1536
    return pl.pallas_call(
        _mod_kernel,
        grid=(n_out // tn,),
        in_specs=[pl.BlockSpec((n_rows, d), lambda j: (0, 0)),
                  pl.BlockSpec((d, tn), lambda j: (0, j)),
                  pl.BlockSpec((1, tn), lambda j: (0, j))],
        out_specs=pl.BlockSpec((n_rows, tn), lambda j: (0, j)),
        out_shape=jax.ShapeDtypeStruct((n_rows, n_out), F32),
        compiler_params=pltpu.CompilerParams(dimension_semantics=("arbitrary",),
                                             vmem_limit_bytes=VMEM_LIMIT["mod"]),
        name="mod",
    )(cvec, w_ada, b_ada.reshape(1, n_out))


HALO = GRID_W
LANE = 128
QKV_TILES = 3 * W_GROUP // LANE
ROW_PITCH = GRID_W + 8


def _inproj_kernel(*refs, latent, seq_len):
    it = iter(refs)
    x_ref = next(it)
    xp_ref, xn_ref = (next(it), next(it)) if latent else (None, None)
    mod_ref, n1_ref, lb_ref, gp_ref, cw_ref, w_ref, wg_ref = (next(it) for _ in range(7))
    qa_ref, ff_ref, fb_ref, va_ref, ga_ref, qkv_ref, zb_ref, gates_ref = (next(it) for _ in range(8))
    qkv_scr, gates_scr = (next(it), next(it)) if latent else (None, None)
    tm = x_ref.shape[0]
    m = mod_ref[0]

    norm_scale = n1_ref[...] * (1.0 + m[1:2])

    def normed(x):
        return (x * lax.rsqrt(jnp.mean(x * x, axis=-1, keepdims=True) + EPS) * norm_scale + m[0:1]).astype(BF16)

    hb = normed(x_ref[...])

    w_qkv = w_ref[:, 5 * W_GROUP:8 * W_GROUP]
    if latent:
        i = pl.program_id(0)
        tiles_per_seq = seq_len // tm
        ext = jnp.dot(jnp.concatenate([normed(xp_ref[...]), hb, normed(xn_ref[...])], axis=0), w_qkv,
                      preferred_element_type=F32)
        above = jnp.where(i % tiles_per_seq == 0, 0.0, ext[:HALO])
        below = jnp.where(i % tiles_per_seq == tiles_per_seq - 1, 0.0, ext[HALO + tm:])
        cur = ext[HALO:HALO + tm]
        prev = jnp.concatenate([above, ext[HALO:tm]], axis=0)
        nxt = jnp.concatenate([ext[2 * HALO:HALO + tm], below], axis=0)
    else:
        cur = jnp.dot(hb, w_qkv, preferred_element_type=F32)
        pos = lax.broadcasted_iota(jnp.int32, cur.shape, 0) % seq_len
        prev = jnp.where(pos == 0, 0.0, pltpu.roll(cur, 1, axis=0))
        nxt = jnp.where(pos == seq_len - 1, 0.0, pltpu.roll(cur, tm - 1, axis=0))
    cw = cw_ref[...]
    y = _silu(prev * cw[0:1] + cur * cw[1:2] + nxt * cw[2:3])
    tiles = []
    for j in range(QKV_TILES):
        t = y[:, j * LANE:(j + 1) * LANE]
        if j < 2 * N_HEADS:
            inv = lax.rsqrt(jnp.sum(t * t, axis=-1, keepdims=True) + EPS)
            t = t * (inv * (D_HEAD ** -0.5) if j < N_HEADS else inv)
        tiles.append(t)

    raw = jnp.dot(hb, wg_ref[...], preferred_element_type=F32)
    gp = gp_ref[...]
    z = raw + gp[1:2]
    softplus = jnp.maximum(z, 0.0) + jnp.log(1.0 + jnp.exp(-jnp.abs(z)))
    lane = lax.broadcasted_iota(jnp.int32, raw.shape, 1)
    gates = jnp.where(lane < 2 * N_HEADS, _sigmoid(raw),
                      jnp.where(lane < 4 * N_HEADS, -jnp.exp(gp[0:1]) * softplus, 0.0))

    if not latent:
        for j, t in enumerate(tiles):
            qkv_ref[:, j * LANE:(j + 1) * LANE] = t
        gates_ref[...] = gates
    else:
        n_rows = tm // GRID_W
        for r in range(n_rows):
            rows, dst = slice(r * GRID_W, (r + 1) * GRID_W), slice(r * ROW_PITCH, r * ROW_PITCH + GRID_W)
            for j, t in enumerate(tiles):
                qkv_scr[j, dst, :] = t[rows]
            gates_scr[dst, :] = gates[rows]
        for c in range(GRID_W):
            for j in range(QKV_TILES):
                qkv_ref[0, :, (c * QKV_TILES + j) * LANE:(c * QKV_TILES + j + 1) * LANE] = (
                    qkv_scr[j, pl.ds(c, n_rows, stride=ROW_PITCH), :])
            gates_ref[0, :, c * GATE_LANES:(c + 1) * GATE_LANES] = gates_scr[pl.ds(c, n_rows, stride=ROW_PITCH), :]

    def proj(j):
        return jnp.dot(hb, w_ref[:, j * W_GROUP:(j + 1) * W_GROUP], preferred_element_type=F32)

    lbp = lb_ref[...]
    e = jnp.exp(lbp - jnp.max(lbp, axis=0, keepdims=True))
    lb = e[0:1] / jnp.sum(e, axis=0, keepdims=True)
    lb_f, lb_b = lb[:, :W_GROUP], lb[:, W_GROUP:]

    qa_ref[...] = _silu(proj(0)).astype(BF16)
    ff_ref[...] = lb_f + (1.0 - lb_f) * _sigmoid(proj(1))
    fb_ref[...] = lb_b + (1.0 - lb_b) * _sigmoid(proj(2))
    ga_ref[...] = _silu(proj(4)).astype(BF16)
    zb_ref[...] = _silu(proj(8)).astype(BF16)
    va_ref[...] = proj(3).astype(BF16)


def _inproj(x2d, mod3, mod_row_of_tile, p, tm, latent, seq_len):
    n_tok = x2d.shape[0]
    tok = lambda width: pl.BlockSpec((tm, width), lambda i: (i, 0))
    widths = [W_GROUP] * 5 + [3 * W_GROUP, W_GROUP, GATE_LANES]
    out_specs = [tok(w) for w in widths]
    dtypes = [BF16, F32, F32, BF16, BF16, F32, BF16, F32]
    out_shape = [jax.ShapeDtypeStruct((n_tok, w), dt) for w, dt in zip(widths, dtypes)]
    in_specs, args, scratch = [tok(D_MODEL)], [x2d], []
    if latent:
        n_halo_blocks, per_tile, rows = n_tok // HALO, tm // HALO, tm // GRID_W
        tiles_per_seq = seq_len // tm
        in_specs += [pl.BlockSpec((HALO, D_MODEL), lambda i: (jnp.maximum(i * per_tile - 1, 0), 0)),
                     pl.BlockSpec((HALO, D_MODEL), lambda i: (jnp.minimum((i + 1) * per_tile, n_halo_blocks - 1), 0))]
        args += [x2d, x2d]
        col = lambda width: pl.BlockSpec((1, rows, GRID_W * width),
                                         lambda i: (i // tiles_per_seq, i % tiles_per_seq, 0))
        col_shape = lambda width: jax.ShapeDtypeStruct((n_tok // seq_len, seq_len // GRID_W, GRID_W * width), F32)
        for k, width in ((5, 3 * W_GROUP), (7, GATE_LANES)):
            out_specs[k], out_shape[k] = col(width), col_shape(width)
        scratch = [pltpu.VMEM((QKV_TILES, rows * ROW_PITCH, LANE), F32), pltpu.VMEM((rows * ROW_PITCH, GATE_LANES), F32)]
    consts = [p["norm1"], p["lbp"], p["gparams"], p["conv_w"], p["w_main"], p["w_gates"]]
    in_specs += [pl.BlockSpec((1, 6, D_MODEL), lambda i: (mod_row_of_tile(i), 0, 0))] + [_resident(a.shape) for a in consts]
    return pl.pallas_call(
        functools.partial(_inproj_kernel, latent=latent, seq_len=seq_len),
        grid=(n_tok // tm,),
        in_specs=in_specs, out_specs=out_specs, out_shape=out_shape, scratch_shapes=scratch,
        compiler_params=pltpu.CompilerParams(dimension_semantics=("arbitrary",),
                                             vmem_limit_bytes=VMEM_LIMIT["inproj"]),
        name="inproj",
    )(*args, mod3, *consts)


INVERSE_BASE_BLOCK = 8


def _hgrn_stages(qf_ref, ff_ref, vf_ref, qb_ref, fb_ref, vb_ref, of_ref, ob_ref, s_ref, group):
    pre = []
    for b in range(group):
        for d, f_ref in enumerate((ff_ref, fb_ref)):
            pre.append((b, d, _cumsum_rows(_tri3(d), jnp.log(f_ref[b]))))
    yield
    chains = []
    for b, d, cum_all in pre:
        q_ref, f_ref, v_ref, o_ref = (qf_ref, ff_ref, vf_ref, of_ref) if d == 0 else (qb_ref, fb_ref, vb_ref, ob_ref)
        incl, _ = _tri_masks(d)
        for h in range(N_HEADS):
            sl = slice(h * D_HEAD, (h + 1) * D_HEAD)
            k = 1.0 - f_ref[b, :, sl]
            G = cum_all[:, sl]
            g_last_row = G[CHUNK - 1:CHUNK] if d == 0 else G[0:1]
            chains.append(dict(
                b=b, d=d, h=h, sl=sl, o_ref=o_ref, incl=incl, vb=v_ref[b, :, sl].astype(BF16),
                decay=jnp.exp(jnp.broadcast_to(g_last_row, (SUBLANES, D_HEAD)).T[:, 0:1]),
                q_dec=(q_ref[b, :, sl].astype(F32) * jnp.exp(G)).astype(BF16), k_dec=k * jnp.exp2(G * NEG_LOG2_E),
                k_tail_t=(k * jnp.exp(g_last_row - G)).T.astype(BF16)))
    attns = [jnp.where(ch["incl"], _dot_nt(ch["q_dec"], ch["k_dec"]), 0.0).astype(BF16) for ch in chains]
    yield
    states = [s_ref[ch["b"], ch["d"], ch["h"]] for ch in chains]
    outs = [_dot(jnp.concatenate([ch["q_dec"], attn], axis=1), jnp.concatenate([s.astype(BF16), ch["vb"]], axis=0))
            for ch, attn, s in zip(chains, attns, states)]
    upds = [_dot(ch["k_tail_t"], ch["vb"]) for ch in chains]
    for ch, o, s, u in zip(chains, outs, states, upds):
        ch["o_ref"][ch["b"], :, ch["sl"]] = o.astype(BF16)
        s_ref[ch["b"], ch["d"], ch["h"]] = ch["decay"] * s + u


PAIR = 2


def _pair_index():
    r = lax.broadcasted_iota(jnp.int32, (CHUNK, PAIR * CHUNK), 0)
    lane = lax.broadcasted_iota(jnp.int32, (CHUNK, PAIR * CHUNK), 1)
    return r, lane % CHUNK, lane // CHUNK


def _block_diag(packed):
    head = lax.broadcasted_iota(jnp.int32, packed.shape, 1) // (packed.shape[1] // PAIR)
    return jnp.concatenate([jnp.where(head == h, packed, jnp.zeros_like(packed)) for h in range(PAIR)], axis=0)


def _unit_triangular_inverses(lowers):
    r, c, _ = _pair_index()
    same_block = lambda block: r // block == c // block
    pdot = lambda a, b: _dot(a, _block_diag(b.astype(BF16)))
    eye = jnp.where(r == c, 1.0, 0.0)
    diag = same_block(INVERSE_BASE_BLOCK)
    ds = [jnp.where(diag, lo, 0.0) for lo in lowers]
    ts = [eye - d for d in ds]
    ps = [pdot(d, d) for d in ds]
    yield
    power = 4
    while power < INVERSE_BASE_BLOCK:
        tps = [pdot(jnp.concatenate([t.astype(BF16), p.astype(BF16)], axis=0), p) for t, p in zip(ts, ps)]
        yield
        ts = [t + tp[:CHUNK] for t, tp in zip(ts, tps)]
        ps = [tp[CHUNK:] for tp in tps]
        power *= 2
    ts = [t + pdot(t, p) for t, p in zip(ts, ps)]
    yield
    block = INVERSE_BASE_BLOCK
    while block < CHUNK:
        off_mask = same_block(2 * block) & jnp.logical_not(same_block(block))
        ws = [pdot(t, jnp.where(off_mask, lo, 0.0)) for t, lo in zip(ts, lowers)]
        yield
        ts = [t - pdot(w, t) for t, w in zip(ts, ws)]
        yield
        block *= 2
    return ts


def _gdn_stages(xf_ref, gf_ref, xb_ref, gb_ref, of_ref, ob_ref, s_ref, group):
    r, c, head = _pair_index()
    lane_head = lax.broadcasted_iota(jnp.int32, (CHUNK, PAIR * D_HEAD), 1) // D_HEAD
    pairs = []
    for b in range(group):
        for d, (x_ref, g_ref) in enumerate(((xf_ref, gf_ref), (xb_ref, gb_ref))):
            gates = g_ref[b]
            cum = _cumsum_rows(_tri3(d), gates)
            for hp in range(N_HEADS // PAIR):
                hs = [PAIR * hp + i for i in range(PAIR)]
                jb = [d * N_HEADS + h for h in hs]
                part = lambda k: x_ref[b, :, (k * N_HEADS + hs[0]) * D_HEAD:(k * N_HEADS + hs[0] + PAIR) * D_HEAD]
                q2, k2, v2 = part(0), part(1), part(2)
                beta2 = jnp.where(lane_head == 0, gates[:, jb[0]:jb[0] + 1], gates[:, jb[1]:jb[1] + 1])
                k_beta2 = k2 * beta2
                kq = _dot_nt(jnp.concatenate([k_beta2.astype(BF16), q2.astype(BF16)], axis=0),
                             _block_diag(k2.astype(BF16)))
                pairs.append(dict(b=b, d=d, hs=hs, q2=q2, k2=k2, v2=v2, beta2=beta2, k_beta2=k_beta2, cum=cum, kq=kq))
    yield
    cum_ts = {}
    for pr in pairs:
        d, cum = pr["d"], pr["cum"]
        incl, pr["strict"] = (c <= r, c < r) if d == 0 else (c >= r, c > r)
        if (pr["b"], d) not in cum_ts:
            cum_ts[pr["b"], d] = jnp.concatenate([cum] * PAIR, axis=0).T
        cum_t = cum_ts[pr["b"], d]
        jg = [2 * N_HEADS + d * N_HEADS + h for h in pr["hs"]]
        pr["g_cum"] = [cum[:, j:j + 1] for j in jg]
        g_col = jnp.where(head == 0, pr["g_cum"][0], pr["g_cum"][1])
        g_row = jnp.where(head[0:1] == 0, cum_t[jg[0]:jg[0] + 1, :], cum_t[jg[1]:jg[1] + 1, :])
        diff = g_col - g_row
        pr["decay_mask"] = jnp.where(incl, jnp.exp(jnp.where(incl, diff, 0.0)), 0.0)
    kqs = [pr["kq"] for pr in pairs]

    lowers = [jnp.where(pr["strict"], kq[:CHUNK] * pr["decay_mask"], 0.0) for pr, kq in zip(pairs, kqs)]
    ts = yield from _unit_triangular_inverses(lowers)
    for pr, kq in zip(pairs, kqs):
        e_g = [jnp.exp(g) for g in pr["g_cum"]]
        e_g2 = jnp.where(lane_head == 0, e_g[0], e_g[1])
        pr["rhs"] = jnp.concatenate(
            [jnp.concatenate([(pr["v2"] * pr["beta2"])[:, i * D_HEAD:(i + 1) * D_HEAD],
                              (pr["k_beta2"] * e_g2)[:, i * D_HEAD:(i + 1) * D_HEAD]], axis=1) for i in range(PAIR)],
            axis=0).astype(BF16)
        attn2 = (kq[CHUNK:] * pr["decay_mask"]).astype(BF16)
        pr["attn"] = [attn2[:, i * CHUNK:(i + 1) * CHUNK] for i in range(PAIR)]
        g_last = [g[CHUNK - 1:CHUNK] if pr["d"] == 0 else g[0:1] for g in pr["g_cum"]]
        q_dec2 = (pr["q2"] * e_g2).astype(BF16)
        pr["q_dec"] = [q_dec2[:, i * D_HEAD:(i + 1) * D_HEAD] for i in range(PAIR)]
        pr["k_tail_t"] = [(pr["k2"][:, i * D_HEAD:(i + 1) * D_HEAD] * jnp.exp(g_last[i] - pr["g_cum"][i])).T.astype(BF16)
                          for i in range(PAIR)]
        pr["decay"] = [jnp.exp(g) for g in g_last]
    xs = [_dot(_block_diag(t.astype(BF16)), pr["rhs"]) for pr, t in zip(pairs, ts)]
    yield
    chains = []
    for pr, x in zip(pairs, xs):
        for i, h in enumerate(pr["hs"]):
            xi = x[i * CHUNK:(i + 1) * CHUNK]
            chains.append(dict(b=pr["b"], d=pr["d"], h=h, u=xi[:, :D_HEAD], w=xi[:, D_HEAD:].astype(BF16),
                               attn=pr["attn"][i], q_dec=pr["q_dec"][i], k_tail_t=pr["k_tail_t"][i],
                               decay=pr["decay"][i]))

    o_refs = (of_ref, ob_ref)
    states = [s_ref[ch["b"], ch["d"], ch["h"]] for ch in chains]
    sbs = [s.astype(BF16) for s in states]
    wqs = [_dot(jnp.concatenate([ch["w"], ch["q_dec"]], axis=0), sb) for ch, sb in zip(chains, sbs)]
    yield
    vbs = [(ch["u"] - wq[:CHUNK]).astype(BF16) for ch, wq in zip(chains, wqs)]
    outs = [wq[CHUNK:] + _dot(ch["attn"], vb) for ch, wq, vb in zip(chains, wqs, vbs)]
    yield
    upd = [_dot(ch["k_tail_t"], vb) for ch, vb in zip(chains, vbs)]
    for ch, s, u, o in zip(chains, states, upd, outs):
        o_refs[ch["d"]][ch["b"], :, ch["h"] * D_HEAD:(ch["h"] + 1) * D_HEAD] = o.astype(of_ref.dtype)
        s_ref[ch["b"], ch["d"], ch["h"]] = s * ch["decay"] + u


def _run_interleaved(primary, secondary, every):
    live = [primary, secondary]
    count = 0
    while live:
        gen = primary if primary in live and (secondary not in live or count < every) else secondary
        count = count + 1 if gen is primary else 0
        try:
            next(gen)
        except StopIteration:
            live.remove(gen)


def _scan_kernel(*refs, n_steps, group, hgrn, gdn, has_s0, emit_state):
    it = iter(refs)
    take = lambda k: [next(it) for _ in range(k)]
    h_in = take(6) if hgrn else None
    g_in = take(4) if gdn else None
    s0 = take(int(hgrn) + int(gdn)) if has_s0 else None
    h_out = take(2) if hgrn else None
    g_out = take(2) if gdn else None
    st = take(int(hgrn) + int(gdn)) if emit_state else None
    s_refs = take(int(hgrn) + int(gdn))
    n = pl.program_id(1)

    @pl.when(n == 0)
    def _():
        for k, s_ref in enumerate(s_refs):
            s_ref[...] = s0[k][...] if has_s0 else jnp.zeros_like(s_ref)

    gens = []
    if gdn:
        gens.append(_gdn_stages(*g_in, *g_out, s_refs[-1], group))
    if hgrn:
        gens.append(_hgrn_stages(*h_in, *h_out, s_refs[0], group))
    if len(gens) == 2:
        _run_interleaved(gens[0], gens[1], every=1)
    else:
        for _ in gens[0]:
            pass

    if emit_state:
        @pl.when(n == n_steps - 1)
        def _():
            for st_ref, s_ref in zip(st, s_refs):
                st_ref[...] = s_ref[...]


def _scan(hgrn_args, gdn_args, states0, batch, group, n_steps, gdn_columns, emit_state, gdn_out_dtype):
    state_shape = (group, 2, N_HEADS, D_HEAD, D_HEAD)
    state_spec = pl.BlockSpec(state_shape, lambda b, n: (b, 0, 0, 0, 0))
    in_specs, args, out_specs, out_shape = [], [], [], []
    if hgrn_args is not None:
        qa, f_fwd, f_bwd, va = hgrn_args
        fwd = pl.BlockSpec((group, CHUNK, W_GROUP), lambda b, n: (b, n, 0))
        bwd = pl.BlockSpec((group, CHUNK, W_GROUP), lambda b, n: (b, n_steps - 1 - n, 0))
        in_specs += [fwd, fwd, fwd, bwd, bwd, bwd]
        args += [qa, f_fwd, va, qa, f_bwd, va]
        out_specs += [fwd, bwd]
        out_shape += [jax.ShapeDtypeStruct(qa.shape, BF16)] * 2
    if gdn_args is not None:
        qkv3, gates3 = gdn_args
        columns = gdn_columns
        fwd_map = (lambda b, n: (b, n, 0)) if columns == 1 else (lambda b, n: (b, 0, n))
        bwd_map = lambda b, n: fwd_map(b, n_steps - 1 - n)
        spec = lambda width, imap: pl.BlockSpec((group, CHUNK, width), imap)
        in_specs += [spec(3 * W_GROUP, fwd_map), spec(GATE_LANES, fwd_map),
                     spec(3 * W_GROUP, bwd_map), spec(GATE_LANES, bwd_map)]
        args += [qkv3, gates3, qkv3, gates3]
        out_specs += [spec(W_GROUP, fwd_map), spec(W_GROUP, bwd_map)]
        out_shape += [jax.ShapeDtypeStruct((batch, qkv3.shape[1], columns * W_GROUP), gdn_out_dtype)] * 2
    n_mixers = int(hgrn_args is not None) + int(gdn_args is not None)
    if states0 is not None:
        in_specs, args = in_specs + [state_spec] * n_mixers, args + list(states0)
    if emit_state:
        out_specs = out_specs + [state_spec] * n_mixers
        out_shape = out_shape + [jax.ShapeDtypeStruct((batch,) + state_shape[1:], F32)] * n_mixers
    return pl.pallas_call(
        functools.partial(_scan_kernel, n_steps=n_steps, group=group, hgrn=hgrn_args is not None,
                          gdn=gdn_args is not None, has_s0=states0 is not None, emit_state=emit_state),
        grid=(batch // group, n_steps),
        in_specs=in_specs, out_specs=out_specs, out_shape=out_shape,
        scratch_shapes=[pltpu.VMEM(state_shape, F32)] * n_mixers,
        compiler_params=pltpu.CompilerParams(dimension_semantics=("arbitrary", "arbitrary"),
                                             vmem_limit_bytes=VMEM_LIMIT["scan"]),
        name="scan",
    )(*args)


FFN_SPLIT = 2
MXU_K_TILE = 256


def _ffn_pieces(d_ff):
    k_tiles = -(-d_ff // MXU_K_TILE)
    bounds = [min(d_ff, MXU_K_TILE * -(-k_tiles * s // FFN_SPLIT)) for s in range(FFN_SPLIT + 1)]
    return list(zip(bounds[:-1], bounds[1:]))


def _tail_kernel(*refs, latent):
    it = iter(refs)
    (x_ref, oaf_ref, oab_ref, obf_ref, obb_ref, ga_ref, zb_ref, mod_ref, na_ref, nb_ref, wo_ref, n2_ref,
     wg_ref, wu_ref, wd_ref, nf_ref) = (next(it) for _ in range(16))
    y_ref = next(it)
    ob_scr = next(it) if latent else None
    tm = x_ref.shape[0]

    def gated_norm(o, w_ref, gate_ref, h):
        sl = slice(h * D_HEAD, (h + 1) * D_HEAD)
        o = o * lax.rsqrt(jnp.mean(o * o, axis=-1, keepdims=True) + EPS)
        return (o * w_ref[:, sl] * gate_ref[:, sl].astype(F32)).astype(BF16)

    def both(f_ref, b_ref, h):
        sl = slice(h * D_HEAD, (h + 1) * D_HEAD)
        return f_ref[:, sl].astype(F32) + b_ref[:, sl].astype(F32)

    if latent:
        n_rows = tm // GRID_W
        for c in range(GRID_W):
            for h in range(N_HEADS):
                sl = slice(c * W_GROUP + h * D_HEAD, c * W_GROUP + (h + 1) * D_HEAD)
                ob_scr[h, pl.ds(c, n_rows, stride=ROW_PITCH), :] = obf_ref[0, :, sl] + obb_ref[0, :, sl]
        o_b = [jnp.concatenate([ob_scr[h, r * ROW_PITCH:r * ROW_PITCH + GRID_W, :] for r in range(n_rows)], axis=0)
               for h in range(N_HEADS)]
    else:
        o_b = [both(obf_ref, obb_ref, h) for h in range(N_HEADS)]
    o_a = [both(oaf_ref, oab_ref, h) for h in range(N_HEADS)]
    mixed_a = jnp.concatenate([gated_norm(o, na_ref, ga_ref, h) for h, o in enumerate(o_a)], axis=-1)
    mixed_b = jnp.concatenate([gated_norm(o, nb_ref, zb_ref, h) for h, o in enumerate(o_b)], axis=-1)
    m = mod_ref[0]
    y_mix = (jnp.dot(mixed_a, wo_ref[:W_GROUP, :], preferred_element_type=F32)
             + jnp.dot(mixed_b, wo_ref[W_GROUP:, :], preferred_element_type=F32))
    x1 = x_ref[...] + m[2:3] * y_mix
    y = x1 * lax.rsqrt(jnp.mean(x1 * x1, axis=-1, keepdims=True) + EPS) * n2_ref[...]
    h2 = (y * (1.0 + m[4:5]) + m[3:4]).astype(BF16)

    ff = None
    for lo, hi in _ffn_pieces(wg_ref.shape[1]):
        cols = slice(lo, hi)
        gate = jnp.dot(h2, wg_ref[:, cols], preferred_element_type=F32)
        up = jnp.dot(h2, wu_ref[:, cols], preferred_element_type=F32)
        part = jnp.dot((_silu(gate) * up).astype(BF16), wd_ref[cols, :], preferred_element_type=F32)
        ff = part if ff is None else ff + part
    x2 = x1 + m[5:6] * ff
    y_ref[...] = x2 * lax.rsqrt(jnp.mean(x2 * x2, axis=-1, keepdims=True) + EPS) * nf_ref[...]


def _tail(x2d, oaf, oab, obf, obb, ga, zb, mod3, mod_row_of_tile, p, tm, latent, seq_len):
    n_tok = x2d.shape[0]
    assert p["w_gate"].shape[1] % LANE == 0
    tok = lambda width: pl.BlockSpec((tm, width), lambda i: (i, 0))
    ob_spec, scratch = tok(W_GROUP), []
    if latent:
        tiles_per_seq = seq_len // tm
        ob_spec = pl.BlockSpec((1, tm // GRID_W, GRID_W * W_GROUP), lambda i: (i // tiles_per_seq, i % tiles_per_seq, 0))
        scratch = [pltpu.VMEM((N_HEADS, tm // GRID_W * ROW_PITCH, D_HEAD), F32)]
    consts = [p["norm_a"], p["norm_b"], p["w_out"], p["norm2"], p["w_gate"], p["w_up"], p["w_down"], p["norm_f"]]
    return pl.pallas_call(
        functools.partial(_tail_kernel, latent=latent),
        grid=(n_tok // tm,),
        in_specs=[tok(D_MODEL), tok(W_GROUP), tok(W_GROUP), ob_spec, ob_spec, tok(W_GROUP), tok(W_GROUP),
                  pl.BlockSpec((1, 6, D_MODEL), lambda i: (mod_row_of_tile(i), 0, 0))]
                 + [_resident(a.shape) for a in consts],
        out_specs=tok(D_MODEL),
        out_shape=jax.ShapeDtypeStruct((n_tok, D_MODEL), F32),
        scratch_shapes=scratch,
        compiler_params=pltpu.CompilerParams(dimension_semantics=("arbitrary",),
                                             vmem_limit_bytes=VMEM_LIMIT["tail"]),
        name="tail",
    )(x2d, oaf, oab, obf, obb, ga, zb, mod3, *consts)


def _stream(x, mod3, mod_row_of_tile, s0_a, s0_b, p, latent):
    batch, seq, _ = x.shape
    x2d = x.reshape(batch * seq, D_MODEL)
    tm = 512
    mod_row = functools.partial(mod_row_of_tile, tm=tm)
    qa, f_fwd, f_bwd, va, ga, qkv, zb, gates = _inproj(x2d, mod3, mod_row, p, tm, latent, seq)
    hgrn_args = [a.reshape(batch, seq, W_GROUP) for a in (qa, f_fwd, f_bwd, va)]
    n_chunks = seq // CHUNK
    if latent:
        assert seq // GRID_W == CHUNK and n_chunks == GRID_W
        oaf, oab, obf, obb = _scan(hgrn_args, (qkv, gates), (s0_a, s0_b), batch, group=4, n_steps=n_chunks,
                                   gdn_columns=GRID_W, emit_state=False, gdn_out_dtype=F32)
        states = (None, None)
    else:
        gdn_args = (qkv.reshape(batch, seq, 3 * W_GROUP), gates.reshape(batch, seq, GATE_LANES))
        oaf, oab, obf, obb, new_a, new_b = _scan(hgrn_args, gdn_args, None, batch, group=4, n_steps=n_chunks,
                                                 gdn_columns=1, emit_state=True, gdn_out_dtype=BF16)
        obf, obb = (o.reshape(batch * seq, W_GROUP) for o in (obf, obb))
        states = (new_a, new_b)
    oaf, oab = (o.reshape(batch * seq, W_GROUP) for o in (oaf, oab))
    y = _tail(x2d, oaf, oab, obf, obb, ga, zb, mod3, mod_row, p, tm, latent, seq)
    return y.reshape(batch, seq, D_MODEL), states


def kernel(x_prompt, x_sample, c, state_hgrn, state_gdn, c_ctx, w_ada, b_ada, norm1, norm2, w_in, conv_w,
           hgrn_lb, gdn_A_log, gdn_dt_bias, hgrn_out_norm, gdn_out_norm, w_out, w_gate, w_up, w_down, norm_f):
    depth = w_in.shape[0]
    assert depth == 1 and hgrn_lb.shape[0] == 2 and conv_w.shape[1] == CONV_W
    dec_batch, dec_seq, _ = x_sample.shape
    l = 0

    n_main = N_MAIN_GROUPS * W_GROUP
    pad8 = jnp.zeros((1, 2 * N_HEADS), F32)
    gparams = jnp.concatenate(
        [jnp.concatenate([pad8, a.reshape(1, 2 * N_HEADS).astype(F32),
                          jnp.zeros((1, GATE_LANES - 4 * N_HEADS), F32)], axis=1)
         for a in (gdn_A_log[l], gdn_dt_bias[l])], axis=0)
    p = {
        "norm1": norm1[l].reshape(1, D_MODEL), "norm2": norm2[l].reshape(1, D_MODEL),
        "lbp": hgrn_lb.reshape(2, 2 * W_GROUP), "gparams": gparams,
        "w_main": w_in[l].astype(BF16),
        "w_gates": jnp.pad(w_in[l][:, n_main:], ((0, 0), (0, GATE_LANES - 4 * N_HEADS))).astype(BF16),
        "conv_w": conv_w[l],
        "norm_a": hgrn_out_norm[l].reshape(1, W_GROUP), "norm_b": gdn_out_norm[l].reshape(1, W_GROUP),
        "w_out": w_out[l].astype(BF16), "w_gate": w_gate[l].astype(BF16), "w_up": w_up[l].astype(BF16),
        "w_down": w_down[l].astype(BF16), "norm_f": norm_f.reshape(1, D_MODEL),
    }

    n_mod_rows = 8
    cvec = jnp.concatenate([c_ctx[None], c, jnp.zeros((n_mod_rows - 1 - dec_batch, D_MODEL), F32)], axis=0)
    mod3 = _modulation(cvec, w_ada[l], b_ada[l]).reshape(n_mod_rows, 6, D_MODEL)

    y_prompt, (new_a, new_b) = _stream(x_prompt, mod3, lambda i, tm: 0, None, None, p, latent=False)
    y_sample, _ = _stream(x_sample, mod3, lambda i, tm: 1 + i // (dec_seq // tm), state_hgrn[:, l],
                          state_gdn[:, l], p, latent=True)
    return y_prompt, y_sample, new_a[:, None], new_b[:, None]
```
